```python
import math
import jax, jax.numpy as jnp
from jax import lax
import numpy as np

D_MODEL = 1024
BATCH = 1
SEQ = 16384
DEPTH = 2

CHUNK = 64
Q_BLOCK = 128
HEAD_DIM = 64
N_HEADS_DIFF = 4
N_HEADS_SB = 4
N_HEADS_FOX = 4
DIFF_VDIM = 2 * HEAD_DIM
WIDTH_DIFF = N_HEADS_DIFF * DIFF_VDIM
WIDTH_SB = N_HEADS_SB * HEAD_DIM
WIDTH_FOX = N_HEADS_FOX * HEAD_DIM
N_BRANCH = 3
ROPE_THETA = 10000.0
N_EXPERTS = 8
TOP_K = 2
D_FF = 3584
ALPHA = (2 * DEPTH) ** 0.25
BETA = (8 * DEPTH) ** -0.25
N_DENSE = (DEPTH + 1) // 2
N_MOE = DEPTH // 2
SPLIT_SIZES = (
    N_HEADS_DIFF * 2 * HEAD_DIM,
    N_HEADS_DIFF * 2 * HEAD_DIM,
    WIDTH_DIFF,
    WIDTH_SB, WIDTH_SB, WIDTH_SB,
    WIDTH_FOX, WIDTH_FOX, WIDTH_FOX,
    N_HEADS_FOX,
    N_BRANCH * D_MODEL,
)
N_IN = sum(SPLIT_SIZES)

kernel_name = "hybrid_diff_stickbreak_fox_moe_deepnorm"


def layer_norm(x, g, b, eps=1e-5):
    xf = x.astype(jnp.float32)
    mu = jnp.mean(xf, axis=-1, keepdims=True)
    var = jnp.mean(jnp.square(xf - mu), axis=-1, keepdims=True)
    y = (xf - mu) * lax.rsqrt(var + eps) * g.astype(jnp.float32) + b.astype(jnp.float32)
    return y.astype(x.dtype)


def rms_norm(x, g, eps=1e-6):
    xf = x.astype(jnp.float32)
    y = xf * lax.rsqrt(jnp.mean(jnp.square(xf), axis=-1, keepdims=True) + eps) * g.astype(jnp.float32)
    return y.astype(x.dtype)


def rope(x, pos):
    half = x.shape[-1] // 2
    inv = ROPE_THETA ** (-jnp.arange(half, dtype=jnp.float32) / half)
    ang = pos[:, None] * inv[None, :]
    cos, sin = jnp.cos(ang), jnp.sin(ang)
    xf = x.astype(jnp.float32)
    x1, x2 = xf[..., :half], xf[..., half:]
    return jnp.concatenate([x1 * cos - x2 * sin, x2 * cos + x1 * sin], axis=-1).astype(x.dtype)


def _heads(t, n_heads, d):
    B, S, _ = t.shape
    return t.reshape(B, S, n_heads, d).transpose(0, 2, 1, 3)


def _merge_heads(o):
    B, H, S, d = o.shape
    return o.transpose(0, 2, 1, 3).reshape(B, S, H * d)


def _q_blocks(q):
    B, H, S, d = q.shape
    return q.reshape(B, H, S // Q_BLOCK, Q_BLOCK, d).transpose(2, 0, 1, 3, 4)


def _from_blocks(o):
    n, B, H, QB, d = o.shape
    return o.transpose(1, 2, 0, 3, 4).reshape(B, H, n * QB, d)


def diff_attention(q, k, v, lam):
    B, H, _, S, d = q.shape
    n_blk = S // Q_BLOCK
    k_chunk = jnp.arange(S) // CHUNK
    q_blk = q.reshape(B, H, 2, n_blk, Q_BLOCK, d).transpose(3, 0, 1, 2, 4, 5)
    scale = d ** -0.5

    def body(args):
        qb, i = args
        q_chunk = (i * Q_BLOCK + jnp.arange(Q_BLOCK)) // CHUNK
        mask = k_chunk[None, :] <= q_chunk[:, None]
        s = jnp.einsum('bhmqd,bhmkd->bhmqk', qb, k).astype(jnp.float32) * scale
        p = jax.nn.softmax(jnp.where(mask, s, -jnp.inf), axis=-1)
        w = p[:, :, 0] - lam * p[:, :, 1]
        return jnp.einsum('bhqk,bhkv->bhqv', w.astype(v.dtype), v)

    o = lax.map(body, (q_blk, jnp.arange(n_blk)))
    return o.transpose(1, 2, 0, 3, 4).reshape(B, H, S, v.shape[-1])


def stick_breaking_attention(q, k, v):
    B, H, S, d = q.shape
    n_blk = S // Q_BLOCK
    k_pos = jnp.arange(S)
    scale = d ** -0.5

    def body(args):
        qb, i = args
        q_pos = i * Q_BLOCK + jnp.arange(Q_BLOCK)
        mask = k_pos[None, :] < q_pos[:, None]
        z = jnp.einsum('bhqd,bhkd->bhqk', qb, k).astype(jnp.float32) * scale
        u = jnp.where(mask, jax.nn.softplus(z), 0.0)
        suffix = lax.cumsum(u, axis=u.ndim - 1, reverse=True) - u
        log_a = -jax.nn.softplus(-z) - suffix
        a = jnp.where(mask, jnp.exp(jnp.where(mask, log_a, -jnp.inf)), 0.0)
        return jnp.einsum('bhqk,bhkd->bhqd', a.astype(v.dtype), v)

    return _from_blocks(lax.map(body, (_q_blocks(q), jnp.arange(n_blk))))


def forgetting_attention(q, k, v, log_f):
    B, H, S, d = q.shape
    n_blk = S // Q_BLOCK
    k_pos = jnp.arange(S)
    scale = d ** -0.5
    F = jnp.cumsum(log_f, axis=-1)
    F_blk = F.reshape(B, H, n_blk, Q_BLOCK).transpose(2, 0, 1, 3)

    def body(args):
        qb, fq, i = args
        q_pos = i * Q_BLOCK + jnp.arange(Q_BLOCK)
        mask = k_pos[None, :] <= q_pos[:, None]
        s = jnp.einsum('bhqd,bhkd->bhqk', qb, k).astype(jnp.float32) * scale
        s = s + fq[..., :, None] - F[:, :, None, :]
        p = jax.nn.softmax(jnp.where(mask, s, -jnp.inf), axis=-1)
        return jnp.einsum('bhqk,bhkd->bhqd', p.astype(v.dtype), v)

    return _from_blocks(lax.map(body, (_q_blocks(q), F_blk, jnp.arange(n_blk))))


def token_mixers(h, layer, w_in, b_forget, lam_q1, lam_k1, lam_q2, lam_k2, diff_norm_g,
                 w_up_diff, w_up_sb, w_up_fox, w_out):
    B, S, D = h.shape
    pos = jnp.arange(S, dtype=jnp.float32)
    proj = h @ w_in
    offsets = [int(c) for c in np.cumsum(SPLIT_SIZES)[:-1]]
    (dq, dk, dv, sq, sk, sv, fq, fk, fv, f_logit, g_logit) = jnp.split(proj, offsets, axis=-1)

    dq = rope(dq.reshape(B, S, N_HEADS_DIFF, 2, HEAD_DIM).transpose(0, 2, 3, 1, 4), pos)
    dk = rope(dk.reshape(B, S, N_HEADS_DIFF, 2, HEAD_DIM).transpose(0, 2, 3, 1, 4), pos)
    lam_init = 0.8 - 0.6 * math.exp(-0.3 * layer)
    f32 = jnp.float32
    lam = (jnp.exp(jnp.sum(lam_q1.astype(f32) * lam_k1.astype(f32)))
           - jnp.exp(jnp.sum(lam_q2.astype(f32) * lam_k2.astype(f32))) + lam_init)
    o_diff = diff_attention(dq, dk, _heads(dv, N_HEADS_DIFF, DIFF_VDIM), lam)
    o_diff = _merge_heads(rms_norm(o_diff, diff_norm_g) * (1.0 - lam_init))

    o_sb = _merge_heads(stick_breaking_attention(
        _heads(sq, N_HEADS_SB, HEAD_DIM), _heads(sk, N_HEADS_SB, HEAD_DIM),
        _heads(sv, N_HEADS_SB, HEAD_DIM)))

    log_f = jax.nn.log_sigmoid(f_logit.astype(f32) + b_forget.astype(f32)).transpose(0, 2, 1)
    o_fox = _merge_heads(forgetting_attention(
        _heads(fq, N_HEADS_FOX, HEAD_DIM), _heads(fk, N_HEADS_FOX, HEAD_DIM),
        _heads(fv, N_HEADS_FOX, HEAD_DIM), log_f))

    gates = jax.nn.sigmoid(g_logit.reshape(B, S, N_BRANCH, D))
    merged = (gates[:, :, 0] * (o_diff @ w_up_diff)
              + gates[:, :, 1] * (o_sb @ w_up_sb)
              + gates[:, :, 2] * (o_fox @ w_up_fox))
    return merged @ w_out


def swiglu(h, w_gate, w_up, w_down):
    return (jax.nn.silu(h @ w_gate) * (h @ w_up)) @ w_down


def moe_ffn(h, w_router, b_router, w_gate, w_up, w_down):
    B, S, D = h.shape
    hf = h.reshape(B * S, D)
    logits = (hf @ w_router).astype(jnp.float32) + b_router.astype(jnp.float32)
    top_v, top_i = lax.top_k(logits, TOP_K)
    top_w = jax.nn.softmax(top_v, axis=-1)
    combine = jnp.sum(jax.nn.one_hot(top_i, N_EXPERTS, dtype=jnp.float32) * top_w[..., None], axis=1)
    combine = combine.astype(h.dtype)
    out = jnp.zeros_like(hf)
    for e in range(N_EXPERTS):
        out = out + combine[:, e:e + 1] * swiglu(hf, w_gate[e], w_up[e], w_down[e])
    return out.reshape(B, S, D)


def setup_inputs(seed: int = 0) -> dict:
    key = jax.random.key(seed)
    ks = jax.random.split(key, 32)
    f32 = jnp.float32
    n = lambda k, shape: jax.random.normal(k, shape, dtype=f32)
    D = D_MODEL
    col_scale = np.concatenate([
        np.full(SPLIT_SIZES[0] + SPLIT_SIZES[1], 1.0, np.float32),
        np.full(SPLIT_SIZES[2], BETA, np.float32),
        np.full(2 * WIDTH_SB, 1.0, np.float32), np.full(WIDTH_SB, BETA, np.float32),
        np.full(2 * WIDTH_FOX, 1.0, np.float32), np.full(WIDTH_FOX, BETA, np.float32),
        np.full(N_HEADS_FOX + N_BRANCH * D, 1.0, np.float32)])
    return {
        "x": n(ks[0], (BATCH, SEQ, D)),
        "ln_in_g": 1.0 + 0.05 * n(ks[1], (D,)),
        "ln_in_b": 0.02 * n(ks[2], (D,)),
        "w_in": n(ks[3], (DEPTH, D, N_IN)) * (D ** -0.5) * jnp.asarray(col_scale),
        "b_forget": 3.0 + 0.5 * n(ks[4], (DEPTH, N_HEADS_FOX)),
        "lam_q1": 0.1 * n(ks[5], (DEPTH, HEAD_DIM)),
        "lam_k1": 0.1 * n(ks[6], (DEPTH, HEAD_DIM)),
        "lam_q2": 0.1 * n(ks[7], (DEPTH, HEAD_DIM)),
        "lam_k2": 0.1 * n(ks[8], (DEPTH, HEAD_DIM)),
        "diff_norm_g": 1.0 + 0.05 * n(ks[9], (DEPTH, DIFF_VDIM)),
        "w_up_diff": n(ks[10], (DEPTH, WIDTH_DIFF, D)) * (WIDTH_DIFF ** -0.5) * BETA,
        "w_up_sb": n(ks[11], (DEPTH, WIDTH_SB, D)) * (WIDTH_SB ** -0.5) * BETA,
        "w_up_fox": n(ks[12], (DEPTH, WIDTH_FOX, D)) * (WIDTH_FOX ** -0.5) * BETA,
        "w_out": n(ks[13], (DEPTH, D, D)) * (D ** -0.5) * BETA,
        "ln_mix_g": 1.0 + 0.05 * n(ks[14], (DEPTH, D)),
        "ln_mix_b": 0.02 * n(ks[15], (DEPTH, D)),
        "ln_ffn_g": 1.0 + 0.05 * n(ks[16], (DEPTH, D)),
        "ln_ffn_b": 0.02 * n(ks[17], (DEPTH, D)),
        "w_gate_dense": n(ks[18], (N_DENSE, D, D_FF)) * (D ** -0.5),
        "w_up_dense": n(ks[19], (N_DENSE, D, D_FF)) * (D ** -0.5) * BETA,
        "w_down_dense": n(ks[20], (N_DENSE, D_FF, D)) * (D_FF ** -0.5) * BETA,
        "w_router": n(ks[21], (N_MOE, D, N_EXPERTS)) * (D ** -0.5),
        "b_router": 0.01 * n(ks[22], (N_MOE, N_EXPERTS)),
        "w_gate_moe": n(ks[23], (N_MOE, N_EXPERTS, D, D_FF)) * (D ** -0.5),
        "w_up_moe": n(ks[24], (N_MOE, N_EXPERTS, D, D_FF)) * (D ** -0.5) * BETA,
        "w_down_moe": n(ks[25], (N_MOE, N_EXPERTS, D_FF, D)) * (D_FF ** -0.5) * BETA,
    }


def reference(x, ln_in_g, ln_in_b, w_in, b_forget, lam_q1, lam_k1, lam_q2, lam_k2,
              diff_norm_g, w_up_diff, w_up_sb, w_up_fox, w_out, ln_mix_g, ln_mix_b,
              ln_ffn_g, ln_ffn_b, w_gate_dense, w_up_dense, w_down_dense, w_router,
              b_router, w_gate_moe, w_up_moe, w_down_moe):
    h = layer_norm(x, ln_in_g, ln_in_b)
    for l in range(DEPTH):
        y = token_mixers(h, l, w_in[l], b_forget[l], lam_q1[l], lam_k1[l], lam_q2[l],
                         lam_k2[l], diff_norm_g[l], w_up_diff[l], w_up_sb[l], w_up_fox[l],
                         w_out[l])
        h = layer_norm(ALPHA * h + y, ln_mix_g[l], ln_mix_b[l])
        if l % 2 == 0:
            j = l // 2
            y = swiglu(h, w_gate_dense[j], w_up_dense[j], w_down_dense[j])
        else:
            j = l // 2
            y = moe_ffn(h, w_router[j], b_router[j], w_gate_moe[j], w_up_moe[j], w_down_moe[j])
        h = layer_norm(ALPHA * h + y, ln_ffn_g[l], ln_ffn_b[l])
    return h
```

```python
import functools
import math

import jax
import jax.numpy as jnp
import numpy as np
from jax import lax
from jax.experimental import pallas as pl
from jax.experimental.pallas import tpu as pltpu

D_MODEL = 1024
DEPTH = 2
CHUNK = 64
HEAD_DIM = 64
N_HEADS_DIFF = 4
N_HEADS_SB = 4
N_HEADS_FOX = 4
WIDTH_DIFF = N_HEADS_DIFF * 2 * HEAD_DIM
WIDTH_SB = N_HEADS_SB * HEAD_DIM
WIDTH_FOX = N_HEADS_FOX * HEAD_DIM
ROPE_THETA = 10000.0
N_EXPERTS = 8
D_FF = 3584
ALPHA = (2 * DEPTH) ** 0.25
Q_SCALE = HEAD_DIM ** -0.5
N_ATT = 3 * WIDTH_DIFF + 3 * WIDTH_SB + 3 * WIDTH_FOX
LANES = 128
NEG = -1e30

F32 = jnp.float32
BF16 = jnp.bfloat16

VMEM_LIMIT = 56 * 1024 * 1024


def _cparams(sem):
    return pltpu.CompilerParams(dimension_semantics=sem, vmem_limit_bytes=VMEM_LIMIT)


def _layer_norm(z, g, b, eps=1e-5):
    mu = jnp.mean(z, axis=-1, keepdims=True)
    zc = z - mu
    var = jnp.mean(zc * zc, axis=-1, keepdims=True)
    return zc * lax.rsqrt(var + eps) * g + b


def _sigmoid(x):
    return 1.0 / (1.0 + jnp.exp(-x))


def _softplus(x):
    return jnp.maximum(x, 0.0) + jnp.log1p(jnp.exp(-jnp.abs(x)))


def _dot(a, b):
    return jnp.dot(a, b, preferred_element_type=F32)


def _dot_nt(a, b):
    return lax.dot_general(a, b, (((1,), (1,)), ((), ())), preferred_element_type=F32)


def _split3(x):
    x1 = x.astype(BF16)
    r = x - x1.astype(F32)
    x2 = r.astype(BF16)
    x3 = (r - x2.astype(F32)).astype(BF16)
    return x1, x2, x3


def _ln_kernel(x_ref, g_ref, b_ref, h_ref, hb_ref):
    h = _layer_norm(x_ref[...], g_ref[...], b_ref[...])
    h_ref[...] = h
    hb_ref[...] = h.astype(BF16)


def _ln_call(x, g, b, tm):
    S, D = x.shape
    row = pl.BlockSpec((tm, D), lambda i: (i, 0))
    vec = pl.BlockSpec((1, D), lambda i: (0, 0))
    return pl.pallas_call(
        _ln_kernel,
        grid=(S // tm,),
        in_specs=[row, vec, vec],
        out_specs=[row, row],
        out_shape=[jax.ShapeDtypeStruct((S, D), F32), jax.ShapeDtypeStruct((S, D), BF16)],
        compiler_params=_cparams(("parallel",)),
        name="ln_in",
    )(x, g.reshape(1, D), b.reshape(1, D))


def _proj_kernel(hb_ref, w_ref, cos_ref, sin_ref, fb_ref,
                 dq_ref, dk_ref, dv_ref, sq_ref, sk_ref, sv_ref, fq_ref, fk_ref, fv_ref, lf_ref):
    x = hb_ref[...]
    cos = cos_ref[...]
    sin = sin_ref[...]
    lane = lax.broadcasted_iota(jnp.int32, (1, LANES), 1)
    lower = (lane & (HEAD_DIM // 2)) == 0

    def mm(c0, c1):
        return _dot(x, w_ref[:, c0:c1])

    def rope_store(y, out_ref, scale):
        for g in range(y.shape[1] // LANES):
            yg = y[:, g * LANES:(g + 1) * LANES]
            partner = jnp.where(lower, pltpu.roll(yg, LANES - HEAD_DIM // 2, 1),
                                pltpu.roll(yg, HEAD_DIM // 2, 1))
            r = yg * cos + partner * sin
            if scale != 1.0:
                r = r * scale
            out_ref[:, g * LANES:(g + 1) * LANES] = r.astype(BF16)

    rope_store(mm(0, 512), dq_ref, Q_SCALE)
    rope_store(mm(512, 1024), dk_ref, 1.0)
    dv_ref[...] = mm(1024, 1536).astype(BF16)
    sq_ref[...] = (mm(1536, 1792) * Q_SCALE).astype(BF16)
    sk_ref[...] = mm(1792, 2048).astype(BF16)
    sv_ref[...] = mm(2048, 2304).astype(BF16)
    fq_ref[...] = (mm(2304, 2560) * Q_SCALE).astype(BF16)
    fk_ref[...] = mm(2560, 2816).astype(BF16)
    fv_ref[...] = mm(2816, 3072).astype(BF16)
    lf_ref[...] = -_softplus(-(mm(N_ATT, N_ATT + LANES) + fb_ref[...]))


def _proj_call(hb, w_att, cos, sin, fbias, tm):
    S, D = hb.shape
    row = lambda n: pl.BlockSpec((tm, n), lambda i: (i, 0))
    full = lambda a: pl.BlockSpec(a.shape, lambda i: (0, 0))
    widths = [512, 512, 512, 256, 256, 256, 256, 256, 256]
    out_shape = [jax.ShapeDtypeStruct((S, n), BF16) for n in widths]
    out_shape.append(jax.ShapeDtypeStruct((S, LANES), F32))
    return pl.pallas_call(
        _proj_kernel,
        grid=(S // tm,),
        in_specs=[row(D), full(w_att), row(LANES), row(LANES), full(fbias)],
        out_specs=[row(n) for n in widths] + [row(LANES)],
        out_shape=out_shape,
        compiler_params=_cparams(("parallel",)),
        name="proj",
    )(hb, w_att, cos, sin, fbias)


def _fcum_kernel(lf_ref, f_ref):
    nh, R, _ = lf_ref.shape
    r0 = lax.broadcasted_iota(jnp.int32, (LANES, LANES), 0)
    c0 = lax.broadcasted_iota(jnp.int32, (LANES, LANES), 1)
    upper = jnp.where(r0 <= c0, 1.0, 0.0).astype(BF16)
    ones = jnp.ones((LANES, LANES), BF16)
    r1 = lax.broadcasted_iota(jnp.int32, (R, R), 0)
    c1 = lax.broadcasted_iota(jnp.int32, (R, R), 1)
    below = jnp.where(c1 < r1, 1.0, 0.0).astype(BF16)
    for h in range(nh):
        parts = _split3(lf_ref[h])
        within = sum(_dot(p, upper) for p in parts)
        totals = sum(_dot(p, ones) for p in parts)
        offs = sum(_dot(below, t) for t in _split3(totals))
        f_ref[h] = within + offs


def _fcum_call(lf):
    return pl.pallas_call(
        _fcum_kernel,
        out_shape=jax.ShapeDtypeStruct(lf.shape, F32),
        name="fcum",
    )(lf)


def _tile_pos(qi, j, tq, tk):
    row = qi * tq + lax.broadcasted_iota(jnp.int32, (tq, tk), 0)
    col = j * tk + lax.broadcasted_iota(jnp.int32, (tq, tk), 1)
    return row, col


def _split_q(q):
    lane = lax.broadcasted_iota(jnp.int32, (1, LANES), 1)
    lo = lane < HEAD_DIM
    zero = jnp.zeros_like(q)
    return jnp.where(lo, q, zero), jnp.where(lo, zero, q)


def _softmax_update(idx, s, vb, m_ref, l_ref, acc_ref):
    m_prev = m_ref[idx]
    m_new = jnp.maximum(m_prev, jnp.max(s, axis=1, keepdims=True))
    alpha = jnp.exp(m_prev - m_new)
    p = jnp.exp(s - m_new)
    l_ref[idx] = alpha * l_ref[idx] + jnp.sum(p, axis=1, keepdims=True)
    acc_ref[idx] = alpha * acc_ref[idx] + _dot(p.astype(BF16), vb)
    m_ref[idx] = m_new


def _init_softmax_state(m_ref, l_ref, acc_ref):
    m_ref[...] = jnp.full(m_ref.shape, NEG, F32)
    l_ref[...] = jnp.zeros(l_ref.shape, F32)
    acc_ref[...] = jnp.zeros(acc_ref.shape, F32)


def _diff_kernel(tq, tk, lam_init, q_ref, k_ref, v_ref, lamv_ref, g_ref, o_ref, m_ref, l_ref, acc_ref):
    qi = pl.program_id(1)
    r = tq // tk
    q1, q2 = _split_q(q_ref[...])
    _init_softmax_state(m_ref, l_ref, acc_ref)

    def tile(j, masked):
        start = pl.multiple_of(j * tk, tk)
        kb = k_ref[pl.ds(start, tk), :]
        vb = v_ref[pl.ds(start, tk), :]
        for idx, q in enumerate((q1, q2)):
            s = _dot_nt(q, kb)
            if masked:
                row, col = _tile_pos(qi, j, tq, tk)
                s = jnp.where((col // CHUNK) <= (row // CHUNK), s, NEG)
            _softmax_update(idx, s, vb, m_ref, l_ref, acc_ref)

    def body(j, c):
        tile(j, False)
        return c

    lax.fori_loop(0, r * qi, body, 0)
    for d in range(r):
        tile(r * qi + d, True)

    lv = lamv_ref[...]
    lam = (jnp.exp(jnp.sum(lv[0:1] * lv[1:2], axis=1, keepdims=True))
           - jnp.exp(jnp.sum(lv[2:3] * lv[3:4], axis=1, keepdims=True)) + lam_init)
    o = acc_ref[0] / l_ref[0] - lam * (acc_ref[1] / l_ref[1])
    o = o * lax.rsqrt(jnp.mean(o * o, axis=-1, keepdims=True) + 1e-6) * g_ref[...]
    o_ref[...] = (o * (1.0 - lam_init)).astype(BF16)


def _diff_call(dq, dk, dv, lamv, gnorm, lam_init, tq, tk):
    S = dq.shape[0]
    nh = dq.shape[1] // LANES
    qspec = pl.BlockSpec((tq, LANES), lambda h, i: (i, h))
    kvspec = pl.BlockSpec((S, LANES), lambda h, i: (0, h))
    full = lambda a: pl.BlockSpec(a.shape, lambda h, i: (0, 0))
    return pl.pallas_call(
        functools.partial(_diff_kernel, tq, tk, lam_init),
        grid=(nh, S // tq),
        in_specs=[qspec, kvspec, kvspec, full(lamv), full(gnorm)],
        out_specs=qspec,
        out_shape=jax.ShapeDtypeStruct(dq.shape, BF16),
        scratch_shapes=[pltpu.VMEM((2, tq, 1), F32), pltpu.VMEM((2, tq, 1), F32),
                        pltpu.VMEM((2, tq, LANES), F32)],
        compiler_params=_cparams(("parallel", "arbitrary")),
        name="diff_attn",
    )(dq, dk, dv, lamv, gnorm)


def _fox_kernel(tq, tk, q_ref, k_ref, v_ref, f_ref, o_ref, m_ref, l_ref, acc_ref):
    qi = pl.program_id(1)
    r = tq // tk
    qa, qb = _split_q(q_ref[...])
    _init_softmax_state(m_ref, l_ref, acc_ref)
    f0 = [f_ref[x, r * qi][:, 0:1] for x in range(2)]

    def tile(j, masked):
        start = pl.multiple_of(j * tk, tk)
        kb = k_ref[pl.ds(start, tk), :]
        vb = v_ref[pl.ds(start, tk), :]
        for idx, q in enumerate((qa, qb)):
            s = _dot_nt(q, kb) + (f0[idx] - f_ref[idx, j])
            if masked:
                row, col = _tile_pos(qi, j, tq, tk)
                s = jnp.where(col <= row, s, NEG)
            _softmax_update(idx, s, vb, m_ref, l_ref, acc_ref)

    def body(j, c):
        tile(j, False)
        return c

    lax.fori_loop(0, r * qi, body, 0)
    for d in range(r):
        tile(r * qi + d, True)

    lane = lax.broadcasted_iota(jnp.int32, (1, LANES), 1)
    o = jnp.where(lane < HEAD_DIM, acc_ref[0] / l_ref[0], acc_ref[1] / l_ref[1])
    o_ref[...] = o.astype(BF16)


def _fox_call(fq, fk, fv, fcum, tq, tk):
    S = fq.shape[0]
    npair = fq.shape[1] // LANES
    qspec = pl.BlockSpec((tq, LANES), lambda p, i: (i, p))
    kvspec = pl.BlockSpec((S, LANES), lambda p, i: (0, p))
    fspec = pl.BlockSpec((2, S // tk, 1, tk), lambda p, i: (p, 0, 0, 0))
    return pl.pallas_call(
        functools.partial(_fox_kernel, tq, tk),
        grid=(npair, S // tq),
        in_specs=[qspec, kvspec, kvspec, fspec],
        out_specs=qspec,
        out_shape=jax.ShapeDtypeStruct(fq.shape, BF16),
        scratch_shapes=[pltpu.VMEM((2, tq, 1), F32), pltpu.VMEM((2, tq, 1), F32),
                        pltpu.VMEM((2, tq, LANES), F32)],
        compiler_params=_cparams(("parallel", "arbitrary")),
        name="fox_attn",
    )(fq, fk, fv, fcum)


def _sb_kernel(tq, tk, q_ref, k_ref, v_ref, tri_ref, o_ref, carry_ref, acc_ref):
    qi = pl.program_id(1)
    r = tq // tk
    qa, qb = _split_q(q_ref[...])
    carry_ref[...] = jnp.zeros(carry_ref.shape, F32)
    acc_ref[...] = jnp.zeros(acc_ref.shape, F32)
    tri = tri_ref[...]

    def tile(j, masked):
        start = pl.multiple_of(j * tk, tk)
        kb = k_ref[pl.ds(start, tk), :]
        vb = v_ref[pl.ds(start, tk), :]
        for idx, q in enumerate((qa, qb)):
            z = _dot_nt(q, kb)
            sp = _softplus(z)
            if masked:
                row, col = _tile_pos(qi, j, tq, tk)
                mask = col < row
                u = jnp.where(mask, sp, 0.0)
            else:
                u = sp
            u_hi = u.astype(BF16)
            u_lo = (u - u_hi.astype(F32)).astype(BF16)
            suffix = _dot(u_hi, tri) + _dot(u_lo, tri) + carry_ref[idx]
            a = jnp.exp(z - sp - suffix)
            if masked:
                a = jnp.where(mask, a, 0.0)
            acc_ref[idx] += _dot(a.astype(BF16), vb)
            carry_ref[idx] += jnp.sum(u, axis=1, keepdims=True)

    for d in range(r):
        tile(r * qi + (r - 1 - d), True)

    def body(i, c):
        tile(r * qi - 1 - i, False)
        return c

    lax.fori_loop(0, r * qi, body, 0)

    lane = lax.broadcasted_iota(jnp.int32, (1, LANES), 1)
    o_ref[...] = jnp.where(lane < HEAD_DIM, acc_ref[0], acc_ref[1]).astype(BF16)


def _sb_call(sq, sk, sv, tq, tk):
    S = sq.shape[0]
    npair = sq.shape[1] // LANES
    qspec = pl.BlockSpec((tq, LANES), lambda p, i: (i, p))
    kvspec = pl.BlockSpec((S, LANES), lambda p, i: (0, p))
    tri = jnp.asarray(np.tril(np.ones((tk, tk), np.float32), -1), BF16)
    return pl.pallas_call(
        functools.partial(_sb_kernel, tq, tk),
        grid=(npair, S // tq),
        in_specs=[qspec, kvspec, kvspec, pl.BlockSpec((tk, tk), lambda p, i: (0, 0))],
        out_specs=qspec,
        out_shape=jax.ShapeDtypeStruct(sq.shape, BF16),
        scratch_shapes=[pltpu.VMEM((2, tq, 1), F32), pltpu.VMEM((2, tq, LANES), F32)],
        compiler_params=_cparams(("parallel", "arbitrary")),
        name="sb_attn",
    )(sq, sk, sv, tri)


def _merge_kernel(hb_ref, h_ref, od_ref, os_ref, of_ref, wg_ref, wud_ref, wus_ref, wuf_ref, wo_ref,
                  g_ref, b_ref, out_ref, outb_ref):
    x = hb_ref[...]
    D = x.shape[1]
    merged = None
    for c, (o_ref, wu_ref) in enumerate(((od_ref, wud_ref), (os_ref, wus_ref), (of_ref, wuf_ref))):
        gate = _sigmoid(_dot(x, wg_ref[:, c * D:(c + 1) * D]))
        term = gate * _dot(o_ref[...], wu_ref[...])
        merged = term if merged is None else merged + term
    y = _dot(merged.astype(BF16), wo_ref[...])
    hn = _layer_norm(ALPHA * h_ref[...] + y, g_ref[...], b_ref[...])
    out_ref[...] = hn
    outb_ref[...] = hn.astype(BF16)


def _merge_call(hb, h, od, osb, of, wg, wud, wus, wuf, wo, g, b, tm):
    S, D = h.shape
    row = lambda a: pl.BlockSpec((tm, a.shape[1]), lambda i: (i, 0))
    full = lambda a: pl.BlockSpec(a.shape, lambda i: (0, 0))
    args = (hb, h, od, osb, of, wg, wud, wus, wuf, wo, g, b)
    in_specs = [row(a) for a in args[:5]] + [full(a) for a in args[5:]]
    return pl.pallas_call(
        _merge_kernel,
        grid=(S // tm,),
        in_specs=in_specs,
        out_specs=[row(h), row(h)],
        out_shape=[jax.ShapeDtypeStruct((S, D), F32), jax.ShapeDtypeStruct((S, D), BF16)],
        compiler_params=_cparams(("parallel",)),
        name="merge",
    )(*args)


def _swiglu_hidden(x, wg, wu):
    gate = _dot(x, wg)
    return (gate * _sigmoid(gate)) * _dot(x, wu)


def _ffn_kernel(hb_ref, h_ref, wg_ref, wu_ref, wd_ref, g_ref, b_ref, out_ref, outb_ref, acc_ref):
    f = pl.program_id(1)

    @pl.when(f == 0)
    def _():
        acc_ref[...] = jnp.zeros(acc_ref.shape, F32)

    hid = _swiglu_hidden(hb_ref[...], wg_ref[...], wu_ref[...])
    acc_ref[...] += _dot(hid.astype(BF16), wd_ref[...])

    @pl.when(f == pl.num_programs(1) - 1)
    def _():
        hn = _layer_norm(ALPHA * h_ref[...] + acc_ref[...], g_ref[...], b_ref[...])
        out_ref[...] = hn
        outb_ref[...] = hn.astype(BF16)


def _ffn_call(hb, h, wg, wu, wd, g, b, tm, tf):
    S, D = h.shape
    F = wg.shape[1]
    row = pl.BlockSpec((tm, D), lambda i, f: (i, 0))
    vec = pl.BlockSpec((1, D), lambda i, f: (0, 0))
    return pl.pallas_call(
        _ffn_kernel,
        grid=(S // tm, F // tf),
        in_specs=[row, row,
                  pl.BlockSpec((D, tf), lambda i, f: (0, f)),
                  pl.BlockSpec((D, tf), lambda i, f: (0, f)),
                  pl.BlockSpec((tf, D), lambda i, f: (f, 0)),
                  vec, vec],
        out_specs=[row, row],
        out_shape=[jax.ShapeDtypeStruct((S, D), F32), jax.ShapeDtypeStruct((S, D), BF16)],
        scratch_shapes=[pltpu.VMEM((tm, D), F32)],
        compiler_params=_cparams(("parallel", "arbitrary")),
        name="ffn",
    )(hb, h, wg, wu, wd, g, b)


def _router_kernel(h_ref, w_ref, b_ref, comb_ref):
    logits = jnp.dot(h_ref[...], w_ref[...], precision=lax.Precision.HIGHEST,
                     preferred_element_type=F32) + b_ref[...]
    lane = lax.broadcasted_iota(jnp.int32, logits.shape, 1)
    logits = jnp.where(lane < N_EXPERTS, logits, NEG)
    v1 = jnp.max(logits, axis=1, keepdims=True)
    i1 = jnp.min(jnp.where(logits == v1, lane, LANES), axis=1, keepdims=True)
    rest = jnp.where(lane == i1, NEG, logits)
    v2 = jnp.max(rest, axis=1, keepdims=True)
    i2 = jnp.min(jnp.where(rest == v2, lane, LANES), axis=1, keepdims=True)
    e2 = jnp.exp(v2 - v1)
    w1 = 1.0 / (1.0 + e2)
    w2 = e2 / (1.0 + e2)
    comb_ref[...] = jnp.where(lane == i1, w1, 0.0) + jnp.where(lane == i2, w2, 0.0)


def _router_call(h, w_router, b_router, tm):
    S, D = h.shape
    wpad = jnp.zeros((D, LANES), F32).at[:, :N_EXPERTS].set(w_router)
    bpad = jnp.zeros((1, LANES), F32).at[0, :N_EXPERTS].set(b_router)
    return pl.pallas_call(
        _router_kernel,
        grid=(S // tm,),
        in_specs=[pl.BlockSpec((tm, D), lambda i: (i, 0)),
                  pl.BlockSpec((D, LANES), lambda i: (0, 0)),
                  pl.BlockSpec((1, LANES), lambda i: (0, 0))],
        out_specs=pl.BlockSpec((tm, LANES), lambda i: (i, 0)),
        out_shape=jax.ShapeDtypeStruct((S, LANES), F32),
        compiler_params=_cparams(("parallel",)),
        name="router",
    )(h, wpad, bpad)


def _moe_kernel(hb_ref, h_ref, comb_ref, wg_ref, wu_ref, wd_ref, g_ref, b_ref,
                out_ref, outb_ref, acc_ref, eacc_ref):
    e = pl.program_id(1)
    f = pl.program_id(2)
    last_f = pl.num_programs(2) - 1

    @pl.when((e == 0) & (f == 0))
    def _():
        acc_ref[...] = jnp.zeros(acc_ref.shape, F32)

    hid = _swiglu_hidden(hb_ref[...], wg_ref[0], wu_ref[0])
    part = _dot(hid.astype(BF16), wd_ref[0])

    @pl.when(f == 0)
    def _():
        eacc_ref[...] = part

    @pl.when(f > 0)
    def _():
        eacc_ref[...] += part

    @pl.when(f == last_f)
    def _():
        lane = lax.broadcasted_iota(jnp.int32, comb_ref.shape, 1)
        ce = jnp.sum(jnp.where(lane == e, comb_ref[...], 0.0), axis=1, keepdims=True)
        acc_ref[...] += ce * eacc_ref[...]

    @pl.when((e == pl.num_programs(1) - 1) & (f == last_f))
    def _():
        hn = _layer_norm(ALPHA * h_ref[...] + acc_ref[...], g_ref[...], b_ref[...])
        out_ref[...] = hn
        outb_ref[...] = hn.astype(BF16)


def _moe_call(hb, h, comb, wg, wu, wd, g, b, tm, tf):
    S, D = h.shape
    E, _, F = wg.shape
    row = pl.BlockSpec((tm, D), lambda i, e, f: (i, 0))
    vec = pl.BlockSpec((1, D), lambda i, e, f: (0, 0))
    return pl.pallas_call(
        _moe_kernel,
        grid=(S // tm, E, F // tf),
        in_specs=[row, row,
                  pl.BlockSpec((tm, LANES), lambda i, e, f: (i, 0)),
                  pl.BlockSpec((1, D, tf), lambda i, e, f: (e, 0, f)),
                  pl.BlockSpec((1, D, tf), lambda i, e, f: (e, 0, f)),
                  pl.BlockSpec((1, tf, D), lambda i, e, f: (e, f, 0)),
                  vec, vec],
        out_specs=[row, row],
        out_shape=[jax.ShapeDtypeStruct((S, D), F32), jax.ShapeDtypeStruct((S, D), BF16)],
        scratch_shapes=[pltpu.VMEM((tm, D), F32), pltpu.VMEM((tm, D), F32)],
        compiler_params=_cparams(("parallel", "arbitrary", "arbitrary")),
        name="moe",
    )(hb, h, comb, wg, wu, wd, g, b)


def _rope_tables(S):
    half = HEAD_DIM // 2
    pos = jnp.arange(S, dtype=F32)
    inv = ROPE_THETA ** (-jnp.arange(half, dtype=F32) / half)
    ang = pos[:, None] * inv[None, :]
    cos, sin = jnp.cos(ang), jnp.sin(ang)
    reps = LANES // HEAD_DIM
    cos_t = jnp.tile(jnp.concatenate([cos, cos], axis=1), (1, reps))
    sin_t = jnp.tile(jnp.concatenate([-sin, sin], axis=1), (1, reps))
    return cos_t, sin_t


def _pick(S, pref):
    t = min(pref, S)
    assert S % t == 0
    return t


def _forward(x, ln_in_g, ln_in_b, w_in, b_forget, lam_q1, lam_k1, lam_q2, lam_k2,
             diff_norm_g, w_up_diff, w_up_sb, w_up_fox, w_out, ln_mix_g, ln_mix_b,
             ln_ffn_g, ln_ffn_b, w_gate_dense, w_up_dense, w_down_dense, w_router,
             b_router, w_gate_moe, w_up_moe, w_down_moe):
    B, S, D = x.shape
    assert B == 1 and D == D_MODEL and S % LANES == 0
    depth = w_in.shape[0]
    tm = _pick(S, 512)
    tq = _pick(S, 512)
    tk = _pick(S, 256)
    tm_ffn = _pick(S, 1024)
    tf = 512
    vec = lambda a: a.reshape(1, -1).astype(F32)

    cos_t, sin_t = _rope_tables(S)
    h, hb = _ln_call(x.reshape(S, D), ln_in_g, ln_in_b, tm)
    for l in range(depth):
        w_l = w_in[l]
        w_att = jnp.concatenate(
            [w_l[:, :N_ATT], jnp.pad(w_l[:, N_ATT:N_ATT + N_HEADS_FOX], ((0, 0), (0, LANES - N_HEADS_FOX)))],
            axis=1).astype(BF16)
        w_gate_logit = w_l[:, N_ATT + N_HEADS_FOX:].astype(BF16)
        fbias = jnp.pad(b_forget[l].astype(F32), (0, LANES - N_HEADS_FOX)).reshape(1, LANES)
        dq, dk, dv, sq, sk, sv, fq, fk, fv, logf = _proj_call(hb, w_att, cos_t, sin_t, fbias, tm)

        lf = logf[:, :N_HEADS_FOX].T.reshape(N_HEADS_FOX, S // LANES, LANES)
        fcum = _fcum_call(lf).reshape(N_HEADS_FOX, S // tk, 1, tk)

        lam_init = 0.8 - 0.6 * math.exp(-0.3 * l)
        lamv = jnp.stack([lam_q1[l], lam_k1[l], lam_q2[l], lam_k2[l]]).astype(F32)
        o_diff = _diff_call(dq, dk, dv, lamv, vec(diff_norm_g[l]), lam_init, tq, tk)
        o_sb = _sb_call(sq, sk, sv, tq, tk)
        o_fox = _fox_call(fq, fk, fv, fcum, tq, tk)

        h, hb = _merge_call(hb, h, o_diff, o_sb, o_fox, w_gate_logit,
                            w_up_diff[l].astype(BF16), w_up_sb[l].astype(BF16),
                            w_up_fox[l].astype(BF16), w_out[l].astype(BF16),
                            vec(ln_mix_g[l]), vec(ln_mix_b[l]), tm)
        j = l // 2
        if l % 2 == 0:
            h, hb = _ffn_call(hb, h, w_gate_dense[j].astype(BF16), w_up_dense[j].astype(BF16),
                              w_down_dense[j].astype(BF16), vec(ln_ffn_g[l]), vec(ln_ffn_b[l]),
                              tm_ffn, tf)
        else:
            comb = _router_call(h, w_router[j], b_router[j], tm)
            h, hb = _moe_call(hb, h, comb, w_gate_moe[j].astype(BF16), w_up_moe[j].astype(BF16),
                              w_down_moe[j].astype(BF16), vec(ln_ffn_g[l]), vec(ln_ffn_b[l]),
                              tm_ffn, tf)
    return h.reshape(B, S, D)


def kernel(x, ln_in_g, ln_in_b, w_in, b_forget, lam_q1, lam_k1, lam_q2, lam_k2, diff_norm_g,
           w_up_diff, w_up_sb, w_up_fox, w_out, ln_mix_g, ln_mix_b, ln_ffn_g, ln_ffn_b,
           w_gate_dense, w_up_dense, w_down_dense, w_router, b_router, w_gate_moe, w_up_moe,
           w_down_moe):
    return _forward(x, ln_in_g, ln_in_b, w_in, b_forget, lam_q1, lam_k1, lam_q2, lam_k2,
                    diff_norm_g, w_up_diff, w_up_sb, w_up_fox, w_out, ln_mix_g, ln_mix_b,
                    ln_ffn_g, ln_ffn_b, w_gate_dense, w_up_dense, w_down_dense, w_router,
                    b_router, w_gate_moe, w_up_moe, w_down_moe)
```

```python
import functools
import math

import jax
import jax.numpy as jnp
import numpy as np
from jax import lax
from jax.experimental import pallas as pl
from jax.experimental.pallas import tpu as pltpu

D_MODEL = 1024
DEPTH = 2
CHUNK = 64
HEAD_DIM = 64
N_HEADS_DIFF = 4
N_HEADS_SB = 4
N_HEADS_FOX = 4
WIDTH_DIFF = N_HEADS_DIFF * 2 * HEAD_DIM
WIDTH_SB = N_HEADS_SB * HEAD_DIM
WIDTH_FOX = N_HEADS_FOX * HEAD_DIM
ROPE_THETA = 10000.0
N_EXPERTS = 8
D_FF = 3584
ALPHA = (2 * DEPTH) ** 0.25
Q_SCALE = HEAD_DIM ** -0.5
LOG2E = 1.4426950408889634
N_ATT = 3 * WIDTH_DIFF + 3 * WIDTH_SB + 3 * WIDTH_FOX
LANES = 128
NEG = -1e30

F32 = jnp.float32
BF16 = jnp.bfloat16

VMEM_LIMIT = 56 * 1024 * 1024


def _cparams(sem):
    return pltpu.CompilerParams(dimension_semantics=sem, vmem_limit_bytes=VMEM_LIMIT)


def _layer_norm(z, g, b, eps=1e-5):
    mu = jnp.mean(z, axis=-1, keepdims=True)
    zc = z - mu
    var = jnp.mean(zc * zc, axis=-1, keepdims=True)
    return zc * lax.rsqrt(var + eps) * g + b


def _sigmoid(x):
    return 1.0 / (1.0 + jnp.exp(-x))


def _softplus(x):
    return jnp.maximum(x, 0.0) + jnp.log1p(jnp.exp(-jnp.abs(x)))


def _dot(a, b):
    return jnp.dot(a, b, preferred_element_type=F32)


def _dot_nt(a, b):
    return lax.dot_general(a, b, (((1,), (1,)), ((), ())), preferred_element_type=F32)


def _split3(x):
    x1 = x.astype(BF16)
    r = x - x1.astype(F32)
    x2 = r.astype(BF16)
    x3 = (r - x2.astype(F32)).astype(BF16)
    return x1, x2, x3


def _ln_kernel(x_ref, g_ref, b_ref, h_ref, hb_ref):
    h = _layer_norm(x_ref[...], g_ref[...], b_ref[...])
    h_ref[...] = h
    hb_ref[...] = h.astype(BF16)


def _ln_call(x, g, b, tm):
    S, D = x.shape
    row = pl.BlockSpec((tm, D), lambda i: (i, 0))
    vec = pl.BlockSpec((1, D), lambda i: (0, 0))
    return pl.pallas_call(
        _ln_kernel,
        grid=(S // tm,),
        in_specs=[row, vec, vec],
        out_specs=[row, row],
        out_shape=[jax.ShapeDtypeStruct((S, D), F32), jax.ShapeDtypeStruct((S, D), BF16)],
        compiler_params=_cparams(("parallel",)),
        name="ln_in",
    )(x, g.reshape(1, D), b.reshape(1, D))


def _proj_kernel(hb_ref, w_ref, cos_ref, sin_ref, fb_ref,
                 dq_ref, dk_ref, dv_ref, sq_ref, sk_ref, sv_ref, fq_ref, fk_ref, fv_ref, lf_ref):
    x = hb_ref[...]
    cos = cos_ref[...]
    sin = sin_ref[...]
    lane = lax.broadcasted_iota(jnp.int32, (1, LANES), 1)
    lower = (lane & (HEAD_DIM // 2)) == 0

    def mm(c0, c1):
        return _dot(x, w_ref[:, c0:c1])

    def rope_store(y, out_ref, scale):
        for g in range(y.shape[1] // LANES):
            yg = y[:, g * LANES:(g + 1) * LANES]
            partner = jnp.where(lower, pltpu.roll(yg, LANES - HEAD_DIM // 2, 1),
                                pltpu.roll(yg, HEAD_DIM // 2, 1))
            r = yg * cos + partner * sin
            if scale != 1.0:
                r = r * scale
            out_ref[:, g * LANES:(g + 1) * LANES] = r.astype(BF16)

    rope_store(mm(0, 512), dq_ref, Q_SCALE * LOG2E)
    rope_store(mm(512, 1024), dk_ref, 1.0)
    dv_ref[...] = mm(1024, 1536).astype(BF16)
    sq_ref[...] = (mm(1536, 1792) * (Q_SCALE * LOG2E)).astype(BF16)
    sk_ref[...] = mm(1792, 2048).astype(BF16)
    sv_ref[...] = mm(2048, 2304).astype(BF16)
    fq_ref[...] = (mm(2304, 2560) * (Q_SCALE * LOG2E)).astype(BF16)
    fk_ref[...] = mm(2560, 2816).astype(BF16)
    fv_ref[...] = mm(2816, 3072).astype(BF16)
    lf_ref[...] = -_softplus(-(mm(N_ATT, N_ATT + LANES) + fb_ref[...]))


def _proj_call(hb, w_att, cos, sin, fbias, tm):
    S, D = hb.shape
    row = lambda n: pl.BlockSpec((tm, n), lambda i: (i, 0))
    full = lambda a: pl.BlockSpec(a.shape, lambda i: (0, 0))
    widths = [512, 512, 512, 256, 256, 256, 256, 256, 256]
    out_shape = [jax.ShapeDtypeStruct((S, n), BF16) for n in widths]
    out_shape.append(jax.ShapeDtypeStruct((S, LANES), F32))
    return pl.pallas_call(
        _proj_kernel,
        grid=(S // tm,),
        in_specs=[row(D), full(w_att), row(LANES), row(LANES), full(fbias)],
        out_specs=[row(n) for n in widths] + [row(LANES)],
        out_shape=out_shape,
        compiler_params=_cparams(("parallel",)),
        name="proj",
    )(hb, w_att, cos, sin, fbias)


def _fcum_kernel(lf_ref, f_ref):
    nh, R, _ = lf_ref.shape
    r0 = lax.broadcasted_iota(jnp.int32, (LANES, LANES), 0)
    c0 = lax.broadcasted_iota(jnp.int32, (LANES, LANES), 1)
    upper = jnp.where(r0 <= c0, 1.0, 0.0).astype(BF16)
    ones = jnp.ones((LANES, LANES), BF16)
    r1 = lax.broadcasted_iota(jnp.int32, (R, R), 0)
    c1 = lax.broadcasted_iota(jnp.int32, (R, R), 1)
    below = jnp.where(c1 < r1, 1.0, 0.0).astype(BF16)
    for h in range(nh):
        parts = _split3(lf_ref[h])
        within = sum(_dot(p, upper) for p in parts)
        totals = sum(_dot(p, ones) for p in parts)
        offs = sum(_dot(below, t) for t in _split3(totals))
        f_ref[h] = within + offs


def _fcum_call(lf):
    return pl.pallas_call(
        _fcum_kernel,
        out_shape=jax.ShapeDtypeStruct(lf.shape, F32),
        name="fcum",
    )(lf)


def _tile_pos(qi, j, tq, tk):
    row = qi * tq + lax.broadcasted_iota(jnp.int32, (tq, tk), 0)
    col = j * tk + lax.broadcasted_iota(jnp.int32, (tq, tk), 1)
    return row, col


def _split_q(q):
    lane = lax.broadcasted_iota(jnp.int32, (1, LANES), 1)
    lo = lane < HEAD_DIM
    zero = jnp.zeros_like(q)
    return jnp.where(lo, q, zero), jnp.where(lo, zero, q)


STRIP = 16
TRI = 256


def _chunk(ref, r0, c):
    return ref[r0:r0 + STRIP, c * LANES:(c + 1) * LANES]


def _softmax_tile(s_ref, p_ref, m_ref, l_ref, al_ref, chunk_state):
    tq, tk = s_ref.shape
    for r0 in range(0, tq, STRIP):
        states = [chunk_state(r0, c) for c in range(tk // LANES)]
        sc = []
        for c, st in enumerate(states):
            if st is False:
                sc.append(None)
            elif st is None:
                sc.append(_chunk(s_ref, r0, c))
            else:
                sc.append(jnp.where(st, _chunk(s_ref, r0, c), NEG))
        live = [s for s in sc if s is not None]
        m_prev = m_ref[r0:r0 + STRIP, :]
        if live:
            mx = functools.reduce(jnp.maximum, live)
            m_new = jnp.maximum(m_prev, jnp.max(mx, axis=1, keepdims=True))
        else:
            m_new = m_prev
        alpha = jnp.exp2(m_prev - m_new)
        psum = jnp.zeros((STRIP, LANES), F32)
        for c, s in enumerate(sc):
            if s is None:
                p_ref[r0:r0 + STRIP, c * LANES:(c + 1) * LANES] = jnp.zeros((STRIP, LANES), BF16)
            else:
                p = jnp.exp2(s - m_new)
                psum = psum + p
                p_ref[r0:r0 + STRIP, c * LANES:(c + 1) * LANES] = p.astype(BF16)
        m_ref[r0:r0 + STRIP, :] = m_new
        al_ref[r0:r0 + STRIP, :] = alpha
        l_ref[r0:r0 + STRIP, :] = alpha * l_ref[r0:r0 + STRIP, :] + psum


def _all_visible(r0, c):
    return None


def _init_softmax_state(m_refs, l_refs, acc_refs):
    for m_ref, l_ref, acc_ref in zip(m_refs, l_refs, acc_refs):
        m_ref[...] = jnp.full(m_ref.shape, NEG, F32)
        l_ref[...] = jnp.zeros(l_ref.shape, F32)
        acc_ref[...] = jnp.zeros(acc_ref.shape, F32)


def _softmax_scratch(tq, tk):
    per_stream = [pltpu.VMEM((tq, tk), F32), pltpu.VMEM((tq, tk), BF16), pltpu.VMEM((tq, LANES), F32),
                  pltpu.VMEM((tq, LANES), F32), pltpu.VMEM((tq, LANES), F32), pltpu.VMEM((tq, LANES), F32)]
    return per_stream + per_stream


def _row_total(l_ref):
    return jnp.sum(l_ref[...], axis=1, keepdims=True)


def _sweep(qi, tq, scores, v_ref, scratch, diag_chunk_state):
    s_refs, p_refs, m_refs, l_refs, al_refs, acc_refs = (scratch[i::6] for i in range(6))

    def softmax(chunk_state):
        for x in range(2):
            _softmax_tile(s_refs[x], p_refs[x], m_refs[x], l_refs[x], al_refs[x], chunk_state)

    def accumulate(j):
        vb = v_ref[pl.ds(pl.multiple_of(j * tq, tq), tq), :]
        for x in range(2):
            acc_refs[x][...] = al_refs[x][...] * acc_refs[x][...] + _dot(p_refs[x][...], vb)

    scores(0)

    def body(j, c):
        softmax(_all_visible)
        scores(j + 1)
        accumulate(j)
        return c

    lax.fori_loop(0, qi, body, 0)
    softmax(diag_chunk_state)
    accumulate(qi)


def _diff_chunk_state(r0, c):
    q_chunk = r0 // CHUNK
    k_lo, k_hi = (c * LANES) // CHUNK, (c * LANES + LANES - 1) // CHUNK
    if k_hi <= q_chunk:
        return None
    if k_lo > q_chunk:
        return False
    return lax.broadcasted_iota(jnp.int32, (STRIP, LANES), 1) < CHUNK


def _diff_kernel(tq, lam_init, q_ref, k_ref, v_ref, lamv_ref, g_ref, o_ref, *scratch):
    s_refs, p_refs, m_refs, l_refs, al_refs, acc_refs = (scratch[i::6] for i in range(6))
    qi = pl.program_id(1)
    qs = _split_q(q_ref[...])
    _init_softmax_state(m_refs, l_refs, acc_refs)

    def scores(j):
        kb = k_ref[pl.ds(pl.multiple_of(j * tq, tq), tq), :]
        for x in range(2):
            s_refs[x][...] = _dot_nt(qs[x], kb)

    _sweep(qi, tq, scores, v_ref, scratch, _diff_chunk_state)

    lv = lamv_ref[...]
    lam = (jnp.exp(jnp.sum(lv[0:1] * lv[1:2], axis=1, keepdims=True))
           - jnp.exp(jnp.sum(lv[2:3] * lv[3:4], axis=1, keepdims=True)) + lam_init)
    o = (acc_refs[0][...] / _row_total(l_refs[0])
         - lam * (acc_refs[1][...] / _row_total(l_refs[1])))
    o = o * lax.rsqrt(jnp.mean(o * o, axis=-1, keepdims=True) + 1e-6) * g_ref[...]
    o_ref[...] = (o * (1.0 - lam_init)).astype(BF16)


def _diff_call(dq, dk, dv, lamv, gnorm, lam_init, tq, tk):
    S = dq.shape[0]
    nh = dq.shape[1] // LANES
    qspec = pl.BlockSpec((tq, LANES), lambda h, i: (i, h))
    kvspec = pl.BlockSpec((S, LANES), lambda h, i: (0, h))
    full = lambda a: pl.BlockSpec(a.shape, lambda h, i: (0, 0))
    return pl.pallas_call(
        functools.partial(_diff_kernel, tq, lam_init),
        grid=(nh, S // tq),
        in_specs=[qspec, kvspec, kvspec, full(lamv), full(gnorm)],
        out_specs=qspec,
        out_shape=jax.ShapeDtypeStruct(dq.shape, BF16),
        scratch_shapes=_softmax_scratch(tq, tq),
        compiler_params=_cparams(("parallel", "arbitrary")),
        name="diff_attn",
    )(dq, dk, dv, lamv, gnorm)


def _causal_chunk_state(strict, col0=0):
    def state(r0, c):
        k_lo, k_hi = col0 + c * LANES, col0 + c * LANES + LANES - 1
        q_lo, q_hi = r0, r0 + STRIP - 1
        if k_hi < q_lo or (not strict and k_hi <= q_lo):
            return None
        if k_lo > q_hi or (strict and k_lo >= q_hi):
            return False
        row = r0 + lax.broadcasted_iota(jnp.int32, (STRIP, LANES), 0)
        col = k_lo + lax.broadcasted_iota(jnp.int32, (STRIP, LANES), 1)
        return col < row if strict else col <= row
    return state


def _fox_kernel(tq, q_ref, k_ref, v_ref, f_ref, o_ref, *scratch):
    s_refs, p_refs, m_refs, l_refs, al_refs, acc_refs = (scratch[i::6] for i in range(6))
    qi = pl.program_id(1)
    qs = _split_q(q_ref[...])
    _init_softmax_state(m_refs, l_refs, acc_refs)
    f0 = [f_ref[x, qi][:, 0:1] for x in range(2)]

    def scores(j):
        kb = k_ref[pl.ds(pl.multiple_of(j * tq, tq), tq), :]
        for x in range(2):
            s_refs[x][...] = _dot_nt(qs[x], kb) + (f0[x] - f_ref[x, j]) * LOG2E

    _sweep(qi, tq, scores, v_ref, scratch, _causal_chunk_state(strict=False))

    lane = lax.broadcasted_iota(jnp.int32, (1, LANES), 1)
    o = jnp.where(lane < HEAD_DIM, acc_refs[0][...] / _row_total(l_refs[0]),
                  acc_refs[1][...] / _row_total(l_refs[1]))
    o_ref[...] = o.astype(BF16)


def _fox_call(fq, fk, fv, fcum, tq, tk):
    S = fq.shape[0]
    npair = fq.shape[1] // LANES
    qspec = pl.BlockSpec((tq, LANES), lambda p, i: (i, p))
    kvspec = pl.BlockSpec((S, LANES), lambda p, i: (0, p))
    fspec = pl.BlockSpec((2, S // tk, 1, tk), lambda p, i: (p, 0, 0, 0))
    assert tk == tq
    return pl.pallas_call(
        functools.partial(_fox_kernel, tq),
        grid=(npair, S // tq),
        in_specs=[qspec, kvspec, kvspec, fspec],
        out_specs=qspec,
        out_shape=jax.ShapeDtypeStruct(fq.shape, BF16),
        scratch_shapes=_softmax_scratch(tq, tq),
        compiler_params=_cparams(("parallel", "arbitrary")),
        name="fox_attn",
    )(fq, fk, fv, fcum)


def _sb_kernel(tq, tk, q_ref, k_ref, v_ref, tri_ref, o_ref, *scratch):
    z_refs, u_refs, lb_refs, a_refs, tot_refs, carry_refs, acc_refs = (scratch[i::7] for i in range(7))
    qi = pl.program_id(1)
    qs = _split_q(q_ref[...])
    for x in range(2):
        carry_refs[x][...] = jnp.zeros((tq, LANES), F32)
        acc_refs[x][...] = jnp.zeros((tq, LANES), F32)
    assert tk == tq and tk == 2 * TRI
    nc = tk // LANES

    def scores(j):
        kb = k_ref[pl.ds(pl.multiple_of(j * tk, tk), tk), :]
        for x in range(2):
            z_refs[x][...] = _dot_nt(qs[x], kb)

    def gates(chunk_state):
        for x in range(2):
            z_ref, u_ref, lb_ref = z_refs[x], u_refs[x], lb_refs[x]
            for r0 in range(0, tq, STRIP):
                usum = jnp.zeros((STRIP, LANES), F32)
                for c in range(nc):
                    st = chunk_state(r0, c)
                    if st is False:
                        u_ref[r0:r0 + STRIP, c * LANES:(c + 1) * LANES] = jnp.zeros((STRIP, LANES), BF16)
                        lb_ref[r0:r0 + STRIP, c * LANES:(c + 1) * LANES] = jnp.zeros((STRIP, LANES), F32)
                        continue
                    z = _chunk(z_ref, r0, c)
                    sp = jnp.maximum(z, 0.0) + LOG2E * jnp.log(1.0 + jnp.exp2(-jnp.abs(z)))
                    u = sp if st is None else jnp.where(st, sp, 0.0)
                    u_ref[r0:r0 + STRIP, c * LANES:(c + 1) * LANES] = u.astype(BF16)
                    lb_ref[r0:r0 + STRIP, c * LANES:(c + 1) * LANES] = z - sp
                    usum = usum + u
                tot_refs[x][r0:r0 + STRIP, :] = (jnp.sum(usum, axis=1, keepdims=True)
                                                 + jnp.zeros((STRIP, LANES), F32))

    def weights_and_accumulate(j, masked):
        vb = v_ref[pl.ds(pl.multiple_of(j * tk, tk), tk), :]
        tri_stack = tri_ref[...]
        for x in range(2):
            u_ref, lb_ref, a_ref = u_refs[x], lb_refs[x], a_refs[x]
            suffix = (_dot(u_ref[...], tri_stack), _dot(u_ref[:, TRI:], tri_stack[:TRI]))
            carry = carry_refs[x][...]
            for c in range(nc):
                cols = slice(c * LANES, (c + 1) * LANES)
                blk, off = divmod(c * LANES, TRI)
                suf = suffix[blk][:, off:off + LANES]
                a = jnp.exp2(lb_ref[:, cols] - suf - carry)
                if masked:
                    row = lax.broadcasted_iota(jnp.int32, (tq, LANES), 0)
                    col = c * LANES + lax.broadcasted_iota(jnp.int32, (tq, LANES), 1)
                    a = jnp.where(col < row, a, 0.0)
                a_ref[:, cols] = a.astype(BF16)
            acc_refs[x][...] += _dot(a_ref[...], vb)
            carry_refs[x][...] = carry + tot_refs[x][...]

    scores(qi)
    gates(_causal_chunk_state(True))
    scores(jnp.maximum(qi - 1, 0))
    weights_and_accumulate(qi, True)

    def body(i, c):
        j = qi - 1 - i
        gates(_all_visible)
        scores(jnp.maximum(j - 1, 0))
        weights_and_accumulate(j, False)
        return c

    lax.fori_loop(0, qi, body, 0)

    lane = lax.broadcasted_iota(jnp.int32, (1, LANES), 1)
    o_ref[...] = jnp.where(lane < HEAD_DIM, acc_refs[0][...], acc_refs[1][...]).astype(BF16)


def _sb_call(sq, sk, sv, tq, tk):
    S = sq.shape[0]
    npair = sq.shape[1] // LANES
    qspec = pl.BlockSpec((tq, LANES), lambda p, i: (i, p))
    kvspec = pl.BlockSpec((S, LANES), lambda p, i: (0, p))
    tri = jnp.asarray(np.concatenate([np.tril(np.ones((TRI, TRI), np.float32), -1),
                                      np.ones((TRI, TRI), np.float32)]), BF16)
    return pl.pallas_call(
        functools.partial(_sb_kernel, tq, tk),
        grid=(npair, S // tq),
        in_specs=[qspec, kvspec, kvspec, pl.BlockSpec((2 * TRI, TRI), lambda p, i: (0, 0))],
        out_specs=qspec,
        out_shape=jax.ShapeDtypeStruct(sq.shape, BF16),
        scratch_shapes=2 * [pltpu.VMEM((tq, tk), F32), pltpu.VMEM((tq, tk), BF16),
                            pltpu.VMEM((tq, tk), F32), pltpu.VMEM((tq, tk), BF16),
                            pltpu.VMEM((tq, LANES), F32), pltpu.VMEM((tq, LANES), F32),
                            pltpu.VMEM((tq, LANES), F32)],
        compiler_params=_cparams(("parallel", "arbitrary")),
        name="sb_attn",
    )(sq, sk, sv, tri)


def _merge_kernel(hb_ref, h_ref, od_ref, os_ref, of_ref, wg_ref, wud_ref, wus_ref, wuf_ref, wo_ref,
                  g_ref, b_ref, out_ref, outb_ref):
    x = hb_ref[...]
    D = x.shape[1]
    merged = None
    for c, (o_ref, wu_ref) in enumerate(((od_ref, wud_ref), (os_ref, wus_ref), (of_ref, wuf_ref))):
        gate = _sigmoid(_dot(x, wg_ref[:, c * D:(c + 1) * D]))
        term = gate * _dot(o_ref[...], wu_ref[...])
        merged = term if merged is None else merged + term
    y = _dot(merged.astype(BF16), wo_ref[...])
    hn = _layer_norm(ALPHA * h_ref[...] + y, g_ref[...], b_ref[...])
    out_ref[...] = hn
    outb_ref[...] = hn.astype(BF16)


def _merge_call(hb, h, od, osb, of, wg, wud, wus, wuf, wo, g, b, tm):
    S, D = h.shape
    row = lambda a: pl.BlockSpec((tm, a.shape[1]), lambda i: (i, 0))
    full = lambda a: pl.BlockSpec(a.shape, lambda i: (0, 0))
    args = (hb, h, od, osb, of, wg, wud, wus, wuf, wo, g, b)
    in_specs = [row(a) for a in args[:5]] + [full(a) for a in args[5:]]
    return pl.pallas_call(
        _merge_kernel,
        grid=(S // tm,),
        in_specs=in_specs,
        out_specs=[row(h), row(h)],
        out_shape=[jax.ShapeDtypeStruct((S, D), F32), jax.ShapeDtypeStruct((S, D), BF16)],
        compiler_params=_cparams(("parallel",)),
        name="merge",
    )(*args)


def _swiglu_hidden(x, wg, wu):
    gate = _dot(x, wg)
    return (gate * _sigmoid(gate)) * _dot(x, wu)


def _ffn_kernel(hb_ref, h_ref, wg_ref, wu_ref, wd_ref, g_ref, b_ref, out_ref, outb_ref, acc_ref):
    f = pl.program_id(1)

    @pl.when(f == 0)
    def _():
        acc_ref[...] = jnp.zeros(acc_ref.shape, F32)

    hid = _swiglu_hidden(hb_ref[...], wg_ref[...], wu_ref[...])
    acc_ref[...] += _dot(hid.astype(BF16), wd_ref[...])

    @pl.when(f == pl.num_programs(1) - 1)
    def _():
        hn = _layer_norm(ALPHA * h_ref[...] + acc_ref[...], g_ref[...], b_ref[...])
        out_ref[...] = hn
        outb_ref[...] = hn.astype(BF16)


def _ffn_call(hb, h, wg, wu, wd, g, b, tm, tf):
    S, D = h.shape
    F = wg.shape[1]
    row = pl.BlockSpec((tm, D), lambda i, f: (i, 0))
    vec = pl.BlockSpec((1, D), lambda i, f: (0, 0))
    return pl.pallas_call(
        _ffn_kernel,
        grid=(S // tm, F // tf),
        in_specs=[row, row,
                  pl.BlockSpec((D, tf), lambda i, f: (0, f)),
                  pl.BlockSpec((D, tf), lambda i, f: (0, f)),
                  pl.BlockSpec((tf, D), lambda i, f: (f, 0)),
                  vec, vec],
        out_specs=[row, row],
        out_shape=[jax.ShapeDtypeStruct((S, D), F32), jax.ShapeDtypeStruct((S, D), BF16)],
        scratch_shapes=[pltpu.VMEM((tm, D), F32)],
        compiler_params=_cparams(("parallel", "arbitrary")),
        name="ffn",
    )(hb, h, wg, wu, wd, g, b)


def _router_kernel(h_ref, w_ref, b_ref, comb_ref):
    logits = jnp.dot(h_ref[...], w_ref[...], precision=lax.Precision.HIGHEST,
                     preferred_element_type=F32) + b_ref[...]
    lane = lax.broadcasted_iota(jnp.int32, logits.shape, 1)
    logits = jnp.where(lane < N_EXPERTS, logits, NEG)
    v1 = jnp.max(logits, axis=1, keepdims=True)
    i1 = jnp.min(jnp.where(logits == v1, lane, LANES), axis=1, keepdims=True)
    rest = jnp.where(lane == i1, NEG, logits)
    v2 = jnp.max(rest, axis=1, keepdims=True)
    i2 = jnp.min(jnp.where(rest == v2, lane, LANES), axis=1, keepdims=True)
    e2 = jnp.exp(v2 - v1)
    w1 = 1.0 / (1.0 + e2)
    w2 = e2 / (1.0 + e2)
    comb_ref[...] = jnp.where(lane == i1, w1, 0.0) + jnp.where(lane == i2, w2, 0.0)


def _router_call(h, w_router, b_router, tm):
    S, D = h.shape
    wpad = jnp.zeros((D, LANES), F32).at[:, :N_EXPERTS].set(w_router)
    bpad = jnp.zeros((1, LANES), F32).at[0, :N_EXPERTS].set(b_router)
    return pl.pallas_call(
        _router_kernel,
        grid=(S // tm,),
        in_specs=[pl.BlockSpec((tm, D), lambda i: (i, 0)),
                  pl.BlockSpec((D, LANES), lambda i: (0, 0)),
                  pl.BlockSpec((1, LANES), lambda i: (0, 0))],
        out_specs=pl.BlockSpec((tm, LANES), lambda i: (i, 0)),
        out_shape=jax.ShapeDtypeStruct((S, LANES), F32),
        compiler_params=_cparams(("parallel",)),
        name="router",
    )(h, wpad, bpad)


def _moe_kernel(hb_ref, h_ref, comb_ref, wg_ref, wu_ref, wd_ref, g_ref, b_ref,
                out_ref, outb_ref, acc_ref, eacc_ref):
    e = pl.program_id(1)
    f = pl.program_id(2)
    last_f = pl.num_programs(2) - 1

    @pl.when((e == 0) & (f == 0))
    def _():
        acc_ref[...] = jnp.zeros(acc_ref.shape, F32)

    hid = _swiglu_hidden(hb_ref[...], wg_ref[0], wu_ref[0])
    part = _dot(hid.astype(BF16), wd_ref[0])

    @pl.when(f == 0)
    def _():
        eacc_ref[...] = part

    @pl.when(f > 0)
    def _():
        eacc_ref[...] += part

    @pl.when(f == last_f)
    def _():
        lane = lax.broadcasted_iota(jnp.int32, comb_ref.shape, 1)
        ce = jnp.sum(jnp.where(lane == e, comb_ref[...], 0.0), axis=1, keepdims=True)
        acc_ref[...] += ce * eacc_ref[...]

    @pl.when((e == pl.num_programs(1) - 1) & (f == last_f))
    def _():
        hn = _layer_norm(ALPHA * h_ref[...] + acc_ref[...], g_ref[...], b_ref[...])
        out_ref[...] = hn
        outb_ref[...] = hn.astype(BF16)


def _moe_call(hb, h, comb, wg, wu, wd, g, b, tm, tf):
    S, D = h.shape
    E, _, F = wg.shape
    row = pl.BlockSpec((tm, D), lambda i, e, f: (i, 0))
    vec = pl.BlockSpec((1, D), lambda i, e, f: (0, 0))
    return pl.pallas_call(
        _moe_kernel,
        grid=(S // tm, E, F // tf),
        in_specs=[row, row,
                  pl.BlockSpec((tm, LANES), lambda i, e, f: (i, 0)),
                  pl.BlockSpec((1, D, tf), lambda i, e, f: (e, 0, f)),
                  pl.BlockSpec((1, D, tf), lambda i, e, f: (e, 0, f)),
                  pl.BlockSpec((1, tf, D), lambda i, e, f: (e, f, 0)),
                  vec, vec],
        out_specs=[row, row],
        out_shape=[jax.ShapeDtypeStruct((S, D), F32), jax.ShapeDtypeStruct((S, D), BF16)],
        scratch_shapes=[pltpu.VMEM((tm, D), F32), pltpu.VMEM((tm, D), F32)],
        compiler_params=_cparams(("parallel", "arbitrary", "arbitrary")),
        name="moe",
    )(hb, h, comb, wg, wu, wd, g, b)


def _rope_tables(S):
    half = HEAD_DIM // 2
    pos = jnp.arange(S, dtype=F32)
    inv = ROPE_THETA ** (-jnp.arange(half, dtype=F32) / half)
    ang = pos[:, None] * inv[None, :]
    cos, sin = jnp.cos(ang), jnp.sin(ang)
    reps = LANES // HEAD_DIM
    cos_t = jnp.tile(jnp.concatenate([cos, cos], axis=1), (1, reps))
    sin_t = jnp.tile(jnp.concatenate([-sin, sin], axis=1), (1, reps))
    return cos_t, sin_t


def _pick(S, pref):
    t = min(pref, S)
    assert S % t == 0
    return t


def _forward(x, ln_in_g, ln_in_b, w_in, b_forget, lam_q1, lam_k1, lam_q2, lam_k2,
             diff_norm_g, w_up_diff, w_up_sb, w_up_fox, w_out, ln_mix_g, ln_mix_b,
             ln_ffn_g, ln_ffn_b, w_gate_dense, w_up_dense, w_down_dense, w_router,
             b_router, w_gate_moe, w_up_moe, w_down_moe):
    B, S, D = x.shape
    assert B == 1 and D == D_MODEL and S % LANES == 0
    depth = w_in.shape[0]
    tm = _pick(S, 512)
    tq = _pick(S, 512)
    tk = _pick(S, 256)
    tm_ffn = _pick(S, 1024)
    tf = 512
    vec = lambda a: a.reshape(1, -1).astype(F32)

    cos_t, sin_t = _rope_tables(S)
    h, hb = _ln_call(x.reshape(S, D), ln_in_g, ln_in_b, tm)
    for l in range(depth):
        w_l = w_in[l]
        w_att = jnp.concatenate(
            [w_l[:, :N_ATT], jnp.pad(w_l[:, N_ATT:N_ATT + N_HEADS_FOX], ((0, 0), (0, LANES - N_HEADS_FOX)))],
            axis=1).astype(BF16)
        w_gate_logit = w_l[:, N_ATT + N_HEADS_FOX:].astype(BF16)
        fbias = jnp.pad(b_forget[l].astype(F32), (0, LANES - N_HEADS_FOX)).reshape(1, LANES)
        dq, dk, dv, sq, sk, sv, fq, fk, fv, logf = _proj_call(hb, w_att, cos_t, sin_t, fbias, tm)

        lf = logf[:, :N_HEADS_FOX].T.reshape(N_HEADS_FOX, S // LANES, LANES)
        fcum = _fcum_call(lf).reshape(N_HEADS_FOX, S // tq, 1, tq)

        lam_init = 0.8 - 0.6 * math.exp(-0.3 * l)
        lamv = jnp.stack([lam_q1[l], lam_k1[l], lam_q2[l], lam_k2[l]]).astype(F32)
        o_diff = _diff_call(dq, dk, dv, lamv, vec(diff_norm_g[l]), lam_init, tq, tk)
        o_sb = _sb_call(sq, sk, sv, tq, tq)
        o_fox = _fox_call(fq, fk, fv, fcum, tq, tq)

        h, hb = _merge_call(hb, h, o_diff, o_sb, o_fox, w_gate_logit,
                            w_up_diff[l].astype(BF16), w_up_sb[l].astype(BF16),
                            w_up_fox[l].astype(BF16), w_out[l].astype(BF16),
                            vec(ln_mix_g[l]), vec(ln_mix_b[l]), tm)
        j = l // 2
        if l % 2 == 0:
            h, hb = _ffn_call(hb, h, w_gate_dense[j].astype(BF16), w_up_dense[j].astype(BF16),
                              w_down_dense[j].astype(BF16), vec(ln_ffn_g[l]), vec(ln_ffn_b[l]),
                              tm_ffn, tf)
        else:
            comb = _router_call(h, w_router[j], b_router[j], tm)
            h, hb = _moe_call(hb, h, comb, w_gate_moe[j].astype(BF16), w_up_moe[j].astype(BF16),
                              w_down_moe[j].astype(BF16), vec(ln_ffn_g[l]), vec(ln_ffn_b[l]),
                              tm_ffn, tf)
    return h.reshape(B, S, D)


def kernel(x, ln_in_g, ln_in_b, w_in, b_forget, lam_q1, lam_k1, lam_q2, lam_k2, diff_norm_g,
           w_up_diff, w_up_sb, w_up_fox, w_out, ln_mix_g, ln_mix_b, ln_ffn_g, ln_ffn_b,
           w_gate_dense, w_up_dense, w_down_dense, w_router, b_router, w_gate_moe, w_up_moe,
           w_down_moe):
    return _forward(x, ln_in_g, ln_in_b, w_in, b_forget, lam_q1, lam_k1, lam_q2, lam_k2,
                    diff_norm_g, w_up_diff, w_up_sb, w_up_fox, w_out, ln_mix_g, ln_mix_b,
                    ln_ffn_g, ln_ffn_b, w_gate_dense, w_up_dense, w_down_dense, w_router,
                    b_router, w_gate_moe, w_up_moe, w_down_moe)
```

```python
import functools
import math

import jax
import jax.numpy as jnp
import numpy as np
from jax import lax
from jax.experimental import pallas as pl
from jax.experimental.pallas import tpu as pltpu

D_MODEL = 1024
DEPTH = 2
CHUNK = 64
HEAD_DIM = 64
N_HEADS_DIFF = 4
N_HEADS_SB = 4
N_HEADS_FOX = 4
WIDTH_DIFF = N_HEADS_DIFF * 2 * HEAD_DIM
WIDTH_SB = N_HEADS_SB * HEAD_DIM
WIDTH_FOX = N_HEADS_FOX * HEAD_DIM
ROPE_THETA = 10000.0
N_EXPERTS = 8
D_FF = 3584
ALPHA = (2 * DEPTH) ** 0.25
Q_SCALE = HEAD_DIM ** -0.5
LOG2E = 1.4426950408889634
N_ATT = 3 * WIDTH_DIFF + 3 * WIDTH_SB + 3 * WIDTH_FOX
LANES = 128
NEG = -1e30

F32 = jnp.float32
BF16 = jnp.bfloat16

VMEM_LIMIT = 56 * 1024 * 1024


def _cparams(sem):
    return pltpu.CompilerParams(dimension_semantics=sem, vmem_limit_bytes=VMEM_LIMIT)


def _layer_norm(z, g, b, eps=1e-5):
    mu = jnp.mean(z, axis=-1, keepdims=True)
    zc = z - mu
    var = jnp.mean(zc * zc, axis=-1, keepdims=True)
    return zc * lax.rsqrt(var + eps) * g + b


def _sigmoid(x):
    return 1.0 / (1.0 + jnp.exp(-x))


def _softplus(x):
    return jnp.maximum(x, 0.0) + jnp.log1p(jnp.exp(-jnp.abs(x)))


def _dot(a, b):
    return jnp.dot(a, b, preferred_element_type=F32)


def _dot_nt(a, b):
    return lax.dot_general(a, b, (((1,), (1,)), ((), ())), preferred_element_type=F32)


def _split3(x):
    x1 = x.astype(BF16)
    r = x - x1.astype(F32)
    x2 = r.astype(BF16)
    x3 = (r - x2.astype(F32)).astype(BF16)
    return x1, x2, x3


def _ln_kernel(x_ref, g_ref, b_ref, h_ref, hb_ref):
    h = _layer_norm(x_ref[...], g_ref[...], b_ref[...])
    h_ref[...] = h
    hb_ref[...] = h.astype(BF16)


def _ln_call(x, g, b, tm):
    S, D = x.shape
    row = pl.BlockSpec((tm, D), lambda i: (i, 0))
    vec = pl.BlockSpec((1, D), lambda i: (0, 0))
    return pl.pallas_call(
        _ln_kernel,
        grid=(S // tm,),
        in_specs=[row, vec, vec],
        out_specs=[row, row],
        out_shape=[jax.ShapeDtypeStruct((S, D), F32), jax.ShapeDtypeStruct((S, D), BF16)],
        compiler_params=_cparams(("parallel",)),
        name="ln_in",
    )(x, g.reshape(1, D), b.reshape(1, D))


def _wsplit_kernel(w_ref, att_ref, fgt_ref, gate_ref):
    att_ref[0] = w_ref[0, :, :N_ATT].astype(BF16)
    lane = lax.broadcasted_iota(jnp.int32, (1, LANES), 1)
    fgt_ref[0] = jnp.where(lane < N_HEADS_FOX, w_ref[0, :, N_ATT:N_ATT + LANES], 0.0).astype(BF16)
    gate_ref[0] = w_ref[0, :, N_ATT + N_HEADS_FOX:].astype(BF16)


def _wsplit_call(w_in, tr):
    depth, D, n_in = w_in.shape
    n_gate = n_in - N_ATT - N_HEADS_FOX
    spec = lambda n: pl.BlockSpec((1, tr, n), lambda l, i: (l, i, 0))
    return pl.pallas_call(
        _wsplit_kernel,
        grid=(depth, D // tr),
        in_specs=[spec(n_in)],
        out_specs=[spec(N_ATT), spec(LANES), spec(n_gate)],
        out_shape=[jax.ShapeDtypeStruct((depth, D, N_ATT), BF16),
                   jax.ShapeDtypeStruct((depth, D, LANES), BF16),
                   jax.ShapeDtypeStruct((depth, D, n_gate), BF16)],
        compiler_params=_cparams(("parallel", "parallel")),
        name="wsplit",
    )(w_in)


def _proj_kernel(hb_ref, w_ref, wf_ref, cos_ref, sin_ref, fb_ref,
                 dq_ref, dk_ref, dv_ref, sq_ref, sk_ref, sv_ref, fq_ref, fk_ref, fv_ref, lf_ref):
    x = hb_ref[...]
    cos = cos_ref[...]
    sin = sin_ref[...]
    lane = lax.broadcasted_iota(jnp.int32, (1, LANES), 1)
    lower = (lane & (HEAD_DIM // 2)) == 0

    def mm(c0, c1):
        return _dot(x, w_ref[:, c0:c1])

    def rope_store(y, out_ref, scale):
        for g in range(y.shape[1] // LANES):
            yg = y[:, g * LANES:(g + 1) * LANES]
            partner = jnp.where(lower, pltpu.roll(yg, LANES - HEAD_DIM // 2, 1),
                                pltpu.roll(yg, HEAD_DIM // 2, 1))
            r = yg * cos + partner * sin
            if scale != 1.0:
                r = r * scale
            out_ref[:, g * LANES:(g + 1) * LANES] = r.astype(BF16)

    rope_store(mm(0, 512), dq_ref, Q_SCALE * LOG2E)
    rope_store(mm(512, 1024), dk_ref, 1.0)
    dv_ref[...] = mm(1024, 1536).astype(BF16)
    sq_ref[...] = (mm(1536, 1792) * (Q_SCALE * LOG2E)).astype(BF16)
    sk_ref[...] = mm(1792, 2048).astype(BF16)
    sv_ref[...] = mm(2048, 2304).astype(BF16)
    fq_ref[...] = (mm(2304, 2560) * (Q_SCALE * LOG2E)).astype(BF16)
    fk_ref[...] = mm(2560, 2816).astype(BF16)
    fv_ref[...] = mm(2816, 3072).astype(BF16)
    lf_ref[...] = -_softplus(-(_dot(x, wf_ref[...]) + fb_ref[...]))


def _proj_call(hb, w_att, w_forget, cos, sin, fbias, tm):
    S, D = hb.shape
    row = lambda n: pl.BlockSpec((tm, n), lambda i: (i, 0))
    full = lambda a: pl.BlockSpec(a.shape, lambda i: (0, 0))
    widths = [512, 512, 512, 256, 256, 256, 256, 256, 256]
    out_shape = [jax.ShapeDtypeStruct((S, n), BF16) for n in widths]
    out_shape.append(jax.ShapeDtypeStruct((S, LANES), F32))
    return pl.pallas_call(
        _proj_kernel,
        grid=(S // tm,),
        in_specs=[row(D), full(w_att), full(w_forget), row(LANES), row(LANES), full(fbias)],
        out_specs=[row(n) for n in widths] + [row(LANES)],
        out_shape=out_shape,
        compiler_params=_cparams(("parallel",)),
        name="proj",
    )(hb, w_att, w_forget, cos, sin, fbias)


def _fcum_kernel(lf_ref, f_ref):
    nh, R, _ = lf_ref.shape
    r0 = lax.broadcasted_iota(jnp.int32, (LANES, LANES), 0)
    c0 = lax.broadcasted_iota(jnp.int32, (LANES, LANES), 1)
    upper = jnp.where(r0 <= c0, 1.0, 0.0).astype(BF16)
    ones = jnp.ones((LANES, LANES), BF16)
    r1 = lax.broadcasted_iota(jnp.int32, (R, R), 0)
    c1 = lax.broadcasted_iota(jnp.int32, (R, R), 1)
    below = jnp.where(c1 < r1, 1.0, 0.0).astype(BF16)
    for h in range(nh):
        parts = _split3(lf_ref[h])
        within = sum(_dot(p, upper) for p in parts)
        totals = sum(_dot(p, ones) for p in parts)
        offs = sum(_dot(below, t) for t in _split3(totals))
        f_ref[h] = within + offs


def _fcum_call(lf):
    return pl.pallas_call(
        _fcum_kernel,
        out_shape=jax.ShapeDtypeStruct(lf.shape, F32),
        name="fcum",
    )(lf)


def _tile_pos(qi, j, tq, tk):
    row = qi * tq + lax.broadcasted_iota(jnp.int32, (tq, tk), 0)
    col = j * tk + lax.broadcasted_iota(jnp.int32, (tq, tk), 1)
    return row, col


def _split_q(q):
    lane = lax.broadcasted_iota(jnp.int32, (1, LANES), 1)
    lo = lane < HEAD_DIM
    zero = jnp.zeros_like(q)
    return jnp.where(lo, q, zero), jnp.where(lo, zero, q)


STRIP = 16
TRI = 256


def _chunk(ref, r0, c):
    return ref[r0:r0 + STRIP, c * LANES:(c + 1) * LANES]


def _softmax_tile(s_ref, p_ref, m_ref, l_ref, al_ref, chunk_state):
    tq, tk = s_ref.shape
    for r0 in range(0, tq, STRIP):
        states = [chunk_state(r0, c) for c in range(tk // LANES)]
        sc = []
        for c, st in enumerate(states):
            if st is False:
                sc.append(None)
            elif st is None:
                sc.append(_chunk(s_ref, r0, c))
            else:
                sc.append(jnp.where(st, _chunk(s_ref, r0, c), NEG))
        live = [s for s in sc if s is not None]
        m_prev = m_ref[r0:r0 + STRIP, :]
        if live:
            mx = functools.reduce(jnp.maximum, live)
            m_new = jnp.maximum(m_prev, jnp.max(mx, axis=1, keepdims=True))
        else:
            m_new = m_prev
        alpha = jnp.exp2(m_prev - m_new)
        psum = jnp.zeros((STRIP, LANES), F32)
        for c, s in enumerate(sc):
            if s is None:
                p_ref[r0:r0 + STRIP, c * LANES:(c + 1) * LANES] = jnp.zeros((STRIP, LANES), BF16)
            else:
                p = jnp.exp2(s - m_new)
                psum = psum + p
                p_ref[r0:r0 + STRIP, c * LANES:(c + 1) * LANES] = p.astype(BF16)
        m_ref[r0:r0 + STRIP, :] = m_new
        al_ref[r0:r0 + STRIP, :] = alpha
        l_ref[r0:r0 + STRIP, :] = alpha * l_ref[r0:r0 + STRIP, :] + psum


def _all_visible(r0, c):
    return None


def _init_softmax_state(m_refs, l_refs, acc_refs):
    for m_ref, l_ref, acc_ref in zip(m_refs, l_refs, acc_refs):
        m_ref[...] = jnp.full(m_ref.shape, NEG, F32)
        l_ref[...] = jnp.zeros(l_ref.shape, F32)
        acc_ref[...] = jnp.zeros(acc_ref.shape, F32)


def _softmax_scratch(tq, tk):
    per_stream = [pltpu.VMEM((tq, tk), F32), pltpu.VMEM((tq, tk), BF16), pltpu.VMEM((tq, LANES), F32),
                  pltpu.VMEM((tq, LANES), F32), pltpu.VMEM((tq, LANES), F32), pltpu.VMEM((tq, LANES), F32)]
    return per_stream + per_stream


def _row_total(l_ref):
    return jnp.sum(l_ref[...], axis=1, keepdims=True)


def _sweep(qi, tq, scores, v_ref, scratch, diag_chunk_state):
    s_refs, p_refs, m_refs, l_refs, al_refs, acc_refs = (scratch[i::6] for i in range(6))

    def softmax(chunk_state):
        for x in range(2):
            _softmax_tile(s_refs[x], p_refs[x], m_refs[x], l_refs[x], al_refs[x], chunk_state)

    def accumulate(j):
        vb = v_ref[pl.ds(pl.multiple_of(j * tq, tq), tq), :]
        for x in range(2):
            acc_refs[x][...] = al_refs[x][...] * acc_refs[x][...] + _dot(p_refs[x][...], vb)

    scores(0)

    def body(j, c):
        softmax(_all_visible)
        scores(j + 1)
        accumulate(j)
        return c

    lax.fori_loop(0, qi, body, 0)
    softmax(diag_chunk_state)
    accumulate(qi)


def _diff_chunk_state(r0, c):
    q_chunk = r0 // CHUNK
    k_lo, k_hi = (c * LANES) // CHUNK, (c * LANES + LANES - 1) // CHUNK
    if k_hi <= q_chunk:
        return None
    if k_lo > q_chunk:
        return False
    return lax.broadcasted_iota(jnp.int32, (STRIP, LANES), 1) < CHUNK


def _diff_kernel(tq, lam_init, q_ref, k_ref, v_ref, lamv_ref, g_ref, o_ref, *scratch):
    s_refs, p_refs, m_refs, l_refs, al_refs, acc_refs = (scratch[i::6] for i in range(6))
    qi = pl.program_id(1)
    qs = _split_q(q_ref[...])
    _init_softmax_state(m_refs, l_refs, acc_refs)

    def scores(j):
        kb = k_ref[pl.ds(pl.multiple_of(j * tq, tq), tq), :]
        for x in range(2):
            s_refs[x][...] = _dot_nt(qs[x], kb)

    _sweep(qi, tq, scores, v_ref, scratch, _diff_chunk_state)

    lv = lamv_ref[...]
    lam = (jnp.exp(jnp.sum(lv[0:1] * lv[1:2], axis=1, keepdims=True))
           - jnp.exp(jnp.sum(lv[2:3] * lv[3:4], axis=1, keepdims=True)) + lam_init)
    o = (acc_refs[0][...] / _row_total(l_refs[0])
         - lam * (acc_refs[1][...] / _row_total(l_refs[1])))
    o = o * lax.rsqrt(jnp.mean(o * o, axis=-1, keepdims=True) + 1e-6) * g_ref[...]
    o_ref[...] = (o * (1.0 - lam_init)).astype(BF16)


def _diff_call(dq, dk, dv, lamv, gnorm, lam_init, tq, tk):
    S = dq.shape[0]
    nh = dq.shape[1] // LANES
    qspec = pl.BlockSpec((tq, LANES), lambda h, i: (i, h))
    kvspec = pl.BlockSpec((S, LANES), lambda h, i: (0, h))
    full = lambda a: pl.BlockSpec(a.shape, lambda h, i: (0, 0))
    return pl.pallas_call(
        functools.partial(_diff_kernel, tq, lam_init),
        grid=(nh, S // tq),
        in_specs=[qspec, kvspec, kvspec, full(lamv), full(gnorm)],
        out_specs=qspec,
        out_shape=jax.ShapeDtypeStruct(dq.shape, BF16),
        scratch_shapes=_softmax_scratch(tq, tq),
        compiler_params=_cparams(("parallel", "arbitrary")),
        name="diff_attn",
    )(dq, dk, dv, lamv, gnorm)


def _causal_chunk_state(strict, col0=0):
    def state(r0, c):
        k_lo, k_hi = col0 + c * LANES, col0 + c * LANES + LANES - 1
        q_lo, q_hi = r0, r0 + STRIP - 1
        if k_hi < q_lo or (not strict and k_hi <= q_lo):
            return None
        if k_lo > q_hi or (strict and k_lo >= q_hi):
            return False
        row = r0 + lax.broadcasted_iota(jnp.int32, (STRIP, LANES), 0)
        col = k_lo + lax.broadcasted_iota(jnp.int32, (STRIP, LANES), 1)
        return col < row if strict else col <= row
    return state


def _fox_kernel(tq, q_ref, k_ref, v_ref, f_ref, o_ref, *scratch):
    s_refs, p_refs, m_refs, l_refs, al_refs, acc_refs = (scratch[i::6] for i in range(6))
    qi = pl.program_id(1)
    qs = _split_q(q_ref[...])
    _init_softmax_state(m_refs, l_refs, acc_refs)
    f0 = [f_ref[x, qi][:, 0:1] for x in range(2)]

    def scores(j):
        kb = k_ref[pl.ds(pl.multiple_of(j * tq, tq), tq), :]
        for x in range(2):
            s_refs[x][...] = _dot_nt(qs[x], kb) + (f0[x] - f_ref[x, j]) * LOG2E

    _sweep(qi, tq, scores, v_ref, scratch, _causal_chunk_state(strict=False))

    lane = lax.broadcasted_iota(jnp.int32, (1, LANES), 1)
    o = jnp.where(lane < HEAD_DIM, acc_refs[0][...] / _row_total(l_refs[0]),
                  acc_refs[1][...] / _row_total(l_refs[1]))
    o_ref[...] = o.astype(BF16)


def _fox_call(fq, fk, fv, fcum, tq, tk):
    S = fq.shape[0]
    npair = fq.shape[1] // LANES
    qspec = pl.BlockSpec((tq, LANES), lambda p, i: (i, p))
    kvspec = pl.BlockSpec((S, LANES), lambda p, i: (0, p))
    fspec = pl.BlockSpec((2, S // tk, 1, tk), lambda p, i: (p, 0, 0, 0))
    assert tk == tq
    return pl.pallas_call(
        functools.partial(_fox_kernel, tq),
        grid=(npair, S // tq),
        in_specs=[qspec, kvspec, kvspec, fspec],
        out_specs=qspec,
        out_shape=jax.ShapeDtypeStruct(fq.shape, BF16),
        scratch_shapes=_softmax_scratch(tq, tq),
        compiler_params=_cparams(("parallel", "arbitrary")),
        name="fox_attn",
    )(fq, fk, fv, fcum)


def _sb_kernel(tq, tk, q_ref, k_ref, v_ref, tri_ref, o_ref, *scratch):
    z_refs, u_refs, lb_refs, a_refs, tot_refs, carry_refs, acc_refs = (scratch[i::7] for i in range(7))
    qi = pl.program_id(1)
    qs = _split_q(q_ref[...])
    for x in range(2):
        carry_refs[x][...] = jnp.zeros((tq, LANES), F32)
        acc_refs[x][...] = jnp.zeros((tq, LANES), F32)
    assert tk == tq and tk == 2 * TRI
    nc = tk // LANES

    def scores(j):
        kb = k_ref[pl.ds(pl.multiple_of(j * tk, tk), tk), :]
        for x in range(2):
            z_refs[x][...] = _dot_nt(qs[x], kb)

    def gates(chunk_state):
        for x in range(2):
            z_ref, u_ref, lb_ref = z_refs[x], u_refs[x], lb_refs[x]
            for r0 in range(0, tq, STRIP):
                usum = jnp.zeros((STRIP, LANES), F32)
                for c in range(nc):
                    st = chunk_state(r0, c)
                    if st is False:
                        u_ref[r0:r0 + STRIP, c * LANES:(c + 1) * LANES] = jnp.zeros((STRIP, LANES), BF16)
                        lb_ref[r0:r0 + STRIP, c * LANES:(c + 1) * LANES] = jnp.zeros((STRIP, LANES), F32)
                        continue
                    z = _chunk(z_ref, r0, c)
                    sp = jnp.maximum(z, 0.0) + LOG2E * jnp.log(1.0 + jnp.exp2(-jnp.abs(z)))
                    u = sp if st is None else jnp.where(st, sp, 0.0)
                    u_ref[r0:r0 + STRIP, c * LANES:(c + 1) * LANES] = u.astype(BF16)
                    lb_ref[r0:r0 + STRIP, c * LANES:(c + 1) * LANES] = z - sp
                    usum = usum + u
                tot_refs[x][r0:r0 + STRIP, :] = (jnp.sum(usum, axis=1, keepdims=True)
                                                 + jnp.zeros((STRIP, LANES), F32))

    def weights_and_accumulate(j, masked):
        vb = v_ref[pl.ds(pl.multiple_of(j * tk, tk), tk), :]
        tri_stack = tri_ref[...]
        for x in range(2):
            u_ref, lb_ref, a_ref = u_refs[x], lb_refs[x], a_refs[x]
            suffix = (_dot(u_ref[...], tri_stack), _dot(u_ref[:, TRI:], tri_stack[:TRI]))
            carry = carry_refs[x][...]
            for c in range(nc):
                cols = slice(c * LANES, (c + 1) * LANES)
                blk, off = divmod(c * LANES, TRI)
                suf = suffix[blk][:, off:off + LANES]
                a = jnp.exp2(lb_ref[:, cols] - suf - carry)
                if masked:
                    row = lax.broadcasted_iota(jnp.int32, (tq, LANES), 0)
                    col = c * LANES + lax.broadcasted_iota(jnp.int32, (tq, LANES), 1)
                    a = jnp.where(col < row, a, 0.0)
                a_ref[:, cols] = a.astype(BF16)
            acc_refs[x][...] += _dot(a_ref[...], vb)
            carry_refs[x][...] = carry + tot_refs[x][...]

    scores(qi)
    gates(_causal_chunk_state(True))
    scores(jnp.maximum(qi - 1, 0))
    weights_and_accumulate(qi, True)

    def body(i, c):
        j = qi - 1 - i
        gates(_all_visible)
        scores(jnp.maximum(j - 1, 0))
        weights_and_accumulate(j, False)
        return c

    lax.fori_loop(0, qi, body, 0)

    lane = lax.broadcasted_iota(jnp.int32, (1, LANES), 1)
    o_ref[...] = jnp.where(lane < HEAD_DIM, acc_refs[0][...], acc_refs[1][...]).astype(BF16)


def _sb_call(sq, sk, sv, tq, tk):
    S = sq.shape[0]
    npair = sq.shape[1] // LANES
    qspec = pl.BlockSpec((tq, LANES), lambda p, i: (i, p))
    kvspec = pl.BlockSpec((S, LANES), lambda p, i: (0, p))
    tri = jnp.asarray(np.concatenate([np.tril(np.ones((TRI, TRI), np.float32), -1),
                                      np.ones((TRI, TRI), np.float32)]), BF16)
    return pl.pallas_call(
        functools.partial(_sb_kernel, tq, tk),
        grid=(npair, S // tq),
        in_specs=[qspec, kvspec, kvspec, pl.BlockSpec((2 * TRI, TRI), lambda p, i: (0, 0))],
        out_specs=qspec,
        out_shape=jax.ShapeDtypeStruct(sq.shape, BF16),
        scratch_shapes=2 * [pltpu.VMEM((tq, tk), F32), pltpu.VMEM((tq, tk), BF16),
                            pltpu.VMEM((tq, tk), F32), pltpu.VMEM((tq, tk), BF16),
                            pltpu.VMEM((tq, LANES), F32), pltpu.VMEM((tq, LANES), F32),
                            pltpu.VMEM((tq, LANES), F32)],
        compiler_params=_cparams(("parallel", "arbitrary")),
        name="sb_attn",
    )(sq, sk, sv, tri)


def _merge_kernel(hb_ref, h_ref, od_ref, os_ref, of_ref, wg_ref, wud_ref, wus_ref, wuf_ref, wo_ref,
                  g_ref, b_ref, out_ref, outb_ref):
    x = hb_ref[...]
    D = x.shape[1]
    merged = None
    for c, (o_ref, wu_ref) in enumerate(((od_ref, wud_ref), (os_ref, wus_ref), (of_ref, wuf_ref))):
        gate = _sigmoid(_dot(x, wg_ref[:, c * D:(c + 1) * D]))
        term = gate * _dot(o_ref[...], wu_ref[...])
        merged = term if merged is None else merged + term
    y = _dot(merged.astype(BF16), wo_ref[...])
    hn = _layer_norm(ALPHA * h_ref[...] + y, g_ref[...], b_ref[...])
    out_ref[...] = hn
    outb_ref[...] = hn.astype(BF16)


def _merge_call(hb, h, od, osb, of, wg, wud, wus, wuf, wo, g, b, tm):
    S, D = h.shape
    row = lambda a: pl.BlockSpec((tm, a.shape[1]), lambda i: (i, 0))
    full = lambda a: pl.BlockSpec(a.shape, lambda i: (0, 0))
    args = (hb, h, od, osb, of, wg, wud, wus, wuf, wo, g, b)
    in_specs = [row(a) for a in args[:5]] + [full(a) for a in args[5:]]
    return pl.pallas_call(
        _merge_kernel,
        grid=(S // tm,),
        in_specs=in_specs,
        out_specs=[row(h), row(h)],
        out_shape=[jax.ShapeDtypeStruct((S, D), F32), jax.ShapeDtypeStruct((S, D), BF16)],
        compiler_params=_cparams(("parallel",)),
        name="merge",
    )(*args)


def _swiglu_hidden(x, wg, wu):
    gate = _dot(x, wg)
    return (gate * _sigmoid(gate)) * _dot(x, wu)


def _ffn_kernel(hb_ref, h_ref, wg_ref, wu_ref, wd_ref, g_ref, b_ref, out_ref, outb_ref, acc_ref):
    f = pl.program_id(1)

    @pl.when(f == 0)
    def _():
        acc_ref[...] = jnp.zeros(acc_ref.shape, F32)

    hid = _swiglu_hidden(hb_ref[...], wg_ref[...], wu_ref[...])
    acc_ref[...] += _dot(hid.astype(BF16), wd_ref[...])

    @pl.when(f == pl.num_programs(1) - 1)
    def _():
        hn = _layer_norm(ALPHA * h_ref[...] + acc_ref[...], g_ref[...], b_ref[...])
        out_ref[...] = hn
        outb_ref[...] = hn.astype(BF16)


def _ffn_call(hb, h, wg, wu, wd, g, b, tm, tf):
    S, D = h.shape
    F = wg.shape[1]
    row = pl.BlockSpec((tm, D), lambda i, f: (i, 0))
    vec = pl.BlockSpec((1, D), lambda i, f: (0, 0))
    return pl.pallas_call(
        _ffn_kernel,
        grid=(S // tm, F // tf),
        in_specs=[row, row,
                  pl.BlockSpec((D, tf), lambda i, f: (0, f)),
                  pl.BlockSpec((D, tf), lambda i, f: (0, f)),
                  pl.BlockSpec((tf, D), lambda i, f: (f, 0)),
                  vec, vec],
        out_specs=[row, row],
        out_shape=[jax.ShapeDtypeStruct((S, D), F32), jax.ShapeDtypeStruct((S, D), BF16)],
        scratch_shapes=[pltpu.VMEM((tm, D), F32)],
        compiler_params=_cparams(("parallel", "arbitrary")),
        name="ffn",
    )(hb, h, wg, wu, wd, g, b)


def _router_kernel(h_ref, w_ref, b_ref, comb_ref):
    logits = jnp.dot(h_ref[...], w_ref[...], precision=lax.Precision.HIGHEST,
                     preferred_element_type=F32) + b_ref[...]
    lane = lax.broadcasted_iota(jnp.int32, logits.shape, 1)
    logits = jnp.where(lane < N_EXPERTS, logits, NEG)
    v1 = jnp.max(logits, axis=1, keepdims=True)
    i1 = jnp.min(jnp.where(logits == v1, lane, LANES), axis=1, keepdims=True)
    rest = jnp.where(lane == i1, NEG, logits)
    v2 = jnp.max(rest, axis=1, keepdims=True)
    i2 = jnp.min(jnp.where(rest == v2, lane, LANES), axis=1, keepdims=True)
    e2 = jnp.exp(v2 - v1)
    w1 = 1.0 / (1.0 + e2)
    w2 = e2 / (1.0 + e2)
    comb_ref[...] = jnp.where(lane == i1, w1, 0.0) + jnp.where(lane == i2, w2, 0.0)


def _router_call(h, w_router, b_router, tm):
    S, D = h.shape
    wpad = jnp.zeros((D, LANES), F32).at[:, :N_EXPERTS].set(w_router)
    bpad = jnp.zeros((1, LANES), F32).at[0, :N_EXPERTS].set(b_router)
    return pl.pallas_call(
        _router_kernel,
        grid=(S // tm,),
        in_specs=[pl.BlockSpec((tm, D), lambda i: (i, 0)),
                  pl.BlockSpec((D, LANES), lambda i: (0, 0)),
                  pl.BlockSpec((1, LANES), lambda i: (0, 0))],
        out_specs=pl.BlockSpec((tm, LANES), lambda i: (i, 0)),
        out_shape=jax.ShapeDtypeStruct((S, LANES), F32),
        compiler_params=_cparams(("parallel",)),
        name="router",
    )(h, wpad, bpad)


def _moe_kernel(hb_ref, h_ref, comb_ref, wg_ref, wu_ref, wd_ref, g_ref, b_ref,
                out_ref, outb_ref, acc_ref, eacc_ref):
    e = pl.program_id(1)
    f = pl.program_id(2)
    last_f = pl.num_programs(2) - 1

    @pl.when((e == 0) & (f == 0))
    def _():
        acc_ref[...] = jnp.zeros(acc_ref.shape, F32)

    hid = _swiglu_hidden(hb_ref[...], wg_ref[0], wu_ref[0])
    part = _dot(hid.astype(BF16), wd_ref[0])

    @pl.when(f == 0)
    def _():
        eacc_ref[...] = part

    @pl.when(f > 0)
    def _():
        eacc_ref[...] += part

    @pl.when(f == last_f)
    def _():
        lane = lax.broadcasted_iota(jnp.int32, comb_ref.shape, 1)
        ce = jnp.sum(jnp.where(lane == e, comb_ref[...], 0.0), axis=1, keepdims=True)
        acc_ref[...] += ce * eacc_ref[...]

    @pl.when((e == pl.num_programs(1) - 1) & (f == last_f))
    def _():
        hn = _layer_norm(ALPHA * h_ref[...] + acc_ref[...], g_ref[...], b_ref[...])
        out_ref[...] = hn
        outb_ref[...] = hn.astype(BF16)


def _moe_call(hb, h, comb, wg, wu, wd, g, b, tm, tf):
    S, D = h.shape
    E, _, F = wg.shape
    row = pl.BlockSpec((tm, D), lambda i, e, f: (i, 0))
    vec = pl.BlockSpec((1, D), lambda i, e, f: (0, 0))
    return pl.pallas_call(
        _moe_kernel,
        grid=(S // tm, E, F // tf),
        in_specs=[row, row,
                  pl.BlockSpec((tm, LANES), lambda i, e, f: (i, 0)),
                  pl.BlockSpec((1, D, tf), lambda i, e, f: (e, 0, f)),
                  pl.BlockSpec((1, D, tf), lambda i, e, f: (e, 0, f)),
                  pl.BlockSpec((1, tf, D), lambda i, e, f: (e, f, 0)),
                  vec, vec],
        out_specs=[row, row],
        out_shape=[jax.ShapeDtypeStruct((S, D), F32), jax.ShapeDtypeStruct((S, D), BF16)],
        scratch_shapes=[pltpu.VMEM((tm, D), F32), pltpu.VMEM((tm, D), F32)],
        compiler_params=_cparams(("parallel", "arbitrary", "arbitrary")),
        name="moe",
    )(hb, h, comb, wg, wu, wd, g, b)


def _rope_tables(S):
    half = HEAD_DIM // 2
    pos = jnp.arange(S, dtype=F32)
    inv = ROPE_THETA ** (-jnp.arange(half, dtype=F32) / half)
    ang = pos[:, None] * inv[None, :]
    cos, sin = jnp.cos(ang), jnp.sin(ang)
    reps = LANES // HEAD_DIM
    cos_t = jnp.tile(jnp.concatenate([cos, cos], axis=1), (1, reps))
    sin_t = jnp.tile(jnp.concatenate([-sin, sin], axis=1), (1, reps))
    return cos_t, sin_t


def _pick(S, pref):
    t = min(pref, S)
    assert S % t == 0
    return t


def _forward(x, ln_in_g, ln_in_b, w_in, b_forget, lam_q1, lam_k1, lam_q2, lam_k2,
             diff_norm_g, w_up_diff, w_up_sb, w_up_fox, w_out, ln_mix_g, ln_mix_b,
             ln_ffn_g, ln_ffn_b, w_gate_dense, w_up_dense, w_down_dense, w_router,
             b_router, w_gate_moe, w_up_moe, w_down_moe):
    B, S, D = x.shape
    assert B == 1 and D == D_MODEL and S % LANES == 0
    depth = w_in.shape[0]
    tm = _pick(S, 512)
    tq = _pick(S, 512)
    tk = _pick(S, 256)
    tm_ffn = _pick(S, 1024)
    tf = 512
    vec = lambda a: a.reshape(1, -1).astype(F32)

    cos_t, sin_t = _rope_tables(S)
    h, hb = _ln_call(x.reshape(S, D), ln_in_g, ln_in_b, tm)
    w_atts, w_forgets, w_gate_logits = _wsplit_call(w_in, 128)
    for l in range(depth):
        w_att, w_forget, w_gate_logit = w_atts[l], w_forgets[l], w_gate_logits[l]
        fbias = jnp.pad(b_forget[l].astype(F32), (0, LANES - N_HEADS_FOX)).reshape(1, LANES)
        dq, dk, dv, sq, sk, sv, fq, fk, fv, logf = _proj_call(hb, w_att, w_forget, cos_t, sin_t,
                                                              fbias, tm)

        lf = logf[:, :N_HEADS_FOX].T.reshape(N_HEADS_FOX, S // LANES, LANES)
        fcum = _fcum_call(lf).reshape(N_HEADS_FOX, S // tq, 1, tq)

        lam_init = 0.8 - 0.6 * math.exp(-0.3 * l)
        lamv = jnp.stack([lam_q1[l], lam_k1[l], lam_q2[l], lam_k2[l]]).astype(F32)
        o_diff = _diff_call(dq, dk, dv, lamv, vec(diff_norm_g[l]), lam_init, tq, tk)
        o_sb = _sb_call(sq, sk, sv, tq, tq)
        o_fox = _fox_call(fq, fk, fv, fcum, tq, tq)

        h, hb = _merge_call(hb, h, o_diff, o_sb, o_fox, w_gate_logit,
                            w_up_diff[l].astype(BF16), w_up_sb[l].astype(BF16),
                            w_up_fox[l].astype(BF16), w_out[l].astype(BF16),
                            vec(ln_mix_g[l]), vec(ln_mix_b[l]), tm)
        j = l // 2
        if l % 2 == 0:
            h, hb = _ffn_call(hb, h, w_gate_dense[j].astype(BF16), w_up_dense[j].astype(BF16),
                              w_down_dense[j].astype(BF16), vec(ln_ffn_g[l]), vec(ln_ffn_b[l]),
                              tm_ffn, tf)
        else:
            comb = _router_call(h, w_router[j], b_router[j], tm)
            h, hb = _moe_call(hb, h, comb, w_gate_moe[j].astype(BF16), w_up_moe[j].astype(BF16),
                              w_down_moe[j].astype(BF16), vec(ln_ffn_g[l]), vec(ln_ffn_b[l]),
                              tm_ffn, tf)
    return h.reshape(B, S, D)


def kernel(x, ln_in_g, ln_in_b, w_in, b_forget, lam_q1, lam_k1, lam_q2, lam_k2, diff_norm_g,
           w_up_diff, w_up_sb, w_up_fox, w_out, ln_mix_g, ln_mix_b, ln_ffn_g, ln_ffn_b,
           w_gate_dense, w_up_dense, w_down_dense, w_router, b_router, w_gate_moe, w_up_moe,
           w_down_moe):
    return _forward(x, ln_in_g, ln_in_b, w_in, b_forget, lam_q1, lam_k1, lam_q2, lam_k2,
                    diff_norm_g, w_up_diff, w_up_sb, w_up_fox, w_out, ln_mix_g, ln_mix_b,
                    ln_ffn_g, ln_ffn_b, w_gate_dense, w_up_dense, w_down_dense, w_router,
                    b_router, w_gate_moe, w_up_moe, w_down_moe)
```

```python
import functools
import math

import jax
import jax.numpy as jnp
import numpy as np
from jax import lax
from jax.experimental import pallas as pl
from jax.experimental.pallas import tpu as pltpu

D_MODEL = 1024
DEPTH = 2
CHUNK = 64
HEAD_DIM = 64
N_HEADS_DIFF = 4
N_HEADS_SB = 4
N_HEADS_FOX = 4
WIDTH_DIFF = N_HEADS_DIFF * 2 * HEAD_DIM
WIDTH_SB = N_HEADS_SB * HEAD_DIM
WIDTH_FOX = N_HEADS_FOX * HEAD_DIM
ROPE_THETA = 10000.0
N_EXPERTS = 8
D_FF = 3584
ALPHA = (2 * DEPTH) ** 0.25
Q_SCALE = HEAD_DIM ** -0.5
LOG2E = 1.4426950408889634
N_ATT = 3 * WIDTH_DIFF + 3 * WIDTH_SB + 3 * WIDTH_FOX
LANES = 128
NEG = -1e30

F32 = jnp.float32
BF16 = jnp.bfloat16

VMEM_LIMIT = 56 * 1024 * 1024


def _cparams(sem):
    return pltpu.CompilerParams(dimension_semantics=sem, vmem_limit_bytes=VMEM_LIMIT)


def _layer_norm(z, g, b, eps=1e-5):
    mu = jnp.mean(z, axis=-1, keepdims=True)
    zc = z - mu
    var = jnp.mean(zc * zc, axis=-1, keepdims=True)
    return zc * lax.rsqrt(var + eps) * g + b


def _sigmoid(x):
    return 1.0 / (1.0 + jnp.exp(-x))


def _softplus(x):
    return jnp.maximum(x, 0.0) + jnp.log1p(jnp.exp(-jnp.abs(x)))


def _dot(a, b):
    return jnp.dot(a, b, preferred_element_type=F32)


def _dot_nt(a, b):
    return lax.dot_general(a, b, (((1,), (1,)), ((), ())), preferred_element_type=F32)


def _split3(x):
    x1 = x.astype(BF16)
    r = x - x1.astype(F32)
    x2 = r.astype(BF16)
    x3 = (r - x2.astype(F32)).astype(BF16)
    return x1, x2, x3


def _ln_kernel(x_ref, g_ref, b_ref, h_ref, hb_ref):
    h = _layer_norm(x_ref[...], g_ref[...], b_ref[...])
    h_ref[...] = h
    hb_ref[...] = h.astype(BF16)


def _ln_call(x, g, b, tm):
    S, D = x.shape
    row = pl.BlockSpec((tm, D), lambda i: (i, 0))
    vec = pl.BlockSpec((1, D), lambda i: (0, 0))
    return pl.pallas_call(
        _ln_kernel,
        grid=(S // tm,),
        in_specs=[row, vec, vec],
        out_specs=[row, row],
        out_shape=[jax.ShapeDtypeStruct((S, D), F32), jax.ShapeDtypeStruct((S, D), BF16)],
        compiler_params=_cparams(("parallel",)),
        name="ln_in",
    )(x, g.reshape(1, D), b.reshape(1, D))


def _wsplit_kernel(w_ref, att_ref, fgt_ref, gate_ref):
    att_ref[0] = w_ref[0, :, :N_ATT].astype(BF16)
    lane = lax.broadcasted_iota(jnp.int32, (1, LANES), 1)
    fgt_ref[0] = jnp.where(lane < N_HEADS_FOX, w_ref[0, :, N_ATT:N_ATT + LANES], 0.0).astype(BF16)
    gate_ref[0] = w_ref[0, :, N_ATT + N_HEADS_FOX:].astype(BF16)


def _wsplit_call(w_in, tr):
    depth, D, n_in = w_in.shape
    n_gate = n_in - N_ATT - N_HEADS_FOX
    spec = lambda n: pl.BlockSpec((1, tr, n), lambda l, i: (l, i, 0))
    return pl.pallas_call(
        _wsplit_kernel,
        grid=(depth, D // tr),
        in_specs=[spec(n_in)],
        out_specs=[spec(N_ATT), spec(LANES), spec(n_gate)],
        out_shape=[jax.ShapeDtypeStruct((depth, D, N_ATT), BF16),
                   jax.ShapeDtypeStruct((depth, D, LANES), BF16),
                   jax.ShapeDtypeStruct((depth, D, n_gate), BF16)],
        compiler_params=_cparams(("parallel", "parallel")),
        name="wsplit",
    )(w_in)


def _proj_kernel(hb_ref, w_ref, wf_ref, cos_ref, sin_ref, fb_ref,
                 dq_ref, dk_ref, dv_ref, sq_ref, sk_ref, sv_ref, fq_ref, fk_ref, fv_ref, lf_ref):
    x = hb_ref[...]
    cos = cos_ref[...]
    sin = sin_ref[...]
    lane = lax.broadcasted_iota(jnp.int32, (1, LANES), 1)
    lower = (lane & (HEAD_DIM // 2)) == 0

    def mm(c0, c1):
        return _dot(x, w_ref[:, c0:c1])

    def rope_store(y, out_ref, scale):
        for g in range(y.shape[1] // LANES):
            yg = y[:, g * LANES:(g + 1) * LANES]
            partner = jnp.where(lower, pltpu.roll(yg, LANES - HEAD_DIM // 2, 1),
                                pltpu.roll(yg, HEAD_DIM // 2, 1))
            r = yg * cos + partner * sin
            if scale != 1.0:
                r = r * scale
            out_ref[:, g * LANES:(g + 1) * LANES] = r.astype(BF16)

    rope_store(mm(0, 512), dq_ref, Q_SCALE * LOG2E)
    rope_store(mm(512, 1024), dk_ref, 1.0)
    dv_ref[...] = mm(1024, 1536).astype(BF16)
    sq_ref[...] = (mm(1536, 1792) * (Q_SCALE * LOG2E)).astype(BF16)
    sk_ref[...] = mm(1792, 2048).astype(BF16)
    sv_ref[...] = mm(2048, 2304).astype(BF16)
    fq_ref[...] = (mm(2304, 2560) * (Q_SCALE * LOG2E)).astype(BF16)
    fk_ref[...] = mm(2560, 2816).astype(BF16)
    fv_ref[...] = mm(2816, 3072).astype(BF16)
    lf_ref[...] = -_softplus(-(_dot(x, wf_ref[...]) + fb_ref[...]))


def _proj_call(hb, w_att, w_forget, cos, sin, fbias, tm):
    S, D = hb.shape
    row = lambda n: pl.BlockSpec((tm, n), lambda i: (i, 0))
    full = lambda a: pl.BlockSpec(a.shape, lambda i: (0, 0))
    widths = [512, 512, 512, 256, 256, 256, 256, 256, 256]
    out_shape = [jax.ShapeDtypeStruct((S, n), BF16) for n in widths]
    out_shape.append(jax.ShapeDtypeStruct((S, LANES), F32))
    return pl.pallas_call(
        _proj_kernel,
        grid=(S // tm,),
        in_specs=[row(D), full(w_att), full(w_forget), row(LANES), row(LANES), full(fbias)],
        out_specs=[row(n) for n in widths] + [row(LANES)],
        out_shape=out_shape,
        compiler_params=_cparams(("parallel",)),
        name="proj",
    )(hb, w_att, w_forget, cos, sin, fbias)


def _fcum_kernel(lf_ref, f_ref):
    nh, R, _ = lf_ref.shape
    r0 = lax.broadcasted_iota(jnp.int32, (LANES, LANES), 0)
    c0 = lax.broadcasted_iota(jnp.int32, (LANES, LANES), 1)
    upper = jnp.where(r0 <= c0, 1.0, 0.0).astype(BF16)
    ones = jnp.ones((LANES, LANES), BF16)
    r1 = lax.broadcasted_iota(jnp.int32, (R, R), 0)
    c1 = lax.broadcasted_iota(jnp.int32, (R, R), 1)
    below = jnp.where(c1 < r1, 1.0, 0.0).astype(BF16)
    for h in range(nh):
        parts = _split3(lf_ref[h])
        within = sum(_dot(p, upper) for p in parts)
        totals = sum(_dot(p, ones) for p in parts)
        offs = sum(_dot(below, t) for t in _split3(totals))
        f_ref[h] = within + offs


def _fcum_call(lf):
    return pl.pallas_call(
        _fcum_kernel,
        out_shape=jax.ShapeDtypeStruct(lf.shape, F32),
        name="fcum",
    )(lf)


def _tile_pos(qi, j, tq, tk):
    row = qi * tq + lax.broadcasted_iota(jnp.int32, (tq, tk), 0)
    col = j * tk + lax.broadcasted_iota(jnp.int32, (tq, tk), 1)
    return row, col


def _split_q(q):
    lane = lax.broadcasted_iota(jnp.int32, (1, LANES), 1)
    lo = lane < HEAD_DIM
    zero = jnp.zeros_like(q)
    return jnp.where(lo, q, zero), jnp.where(lo, zero, q)


STRIP = 16
TRI = 256


def _chunk(ref, r0, c):
    return ref[r0:r0 + STRIP, c * LANES:(c + 1) * LANES]


def _softmax_tile(s_ref, p_ref, m_ref, l_ref, al_ref, chunk_state):
    tq, tk = s_ref.shape
    for r0 in range(0, tq, STRIP):
        states = [chunk_state(r0, c) for c in range(tk // LANES)]
        sc = []
        for c, st in enumerate(states):
            if st is False:
                sc.append(None)
            elif st is None:
                sc.append(_chunk(s_ref, r0, c))
            else:
                sc.append(jnp.where(st, _chunk(s_ref, r0, c), NEG))
        live = [s for s in sc if s is not None]
        m_prev = m_ref[r0:r0 + STRIP, :]
        if live:
            mx = functools.reduce(jnp.maximum, live)
            m_new = jnp.maximum(m_prev, jnp.max(mx, axis=1, keepdims=True))
        else:
            m_new = m_prev
        alpha = jnp.exp2(m_prev - m_new)
        psum = jnp.zeros((STRIP, LANES), F32)
        for c, s in enumerate(sc):
            if s is None:
                p_ref[r0:r0 + STRIP, c * LANES:(c + 1) * LANES] = jnp.zeros((STRIP, LANES), BF16)
            else:
                p = jnp.exp2(s - m_new)
                psum = psum + p
                p_ref[r0:r0 + STRIP, c * LANES:(c + 1) * LANES] = p.astype(BF16)
        m_ref[r0:r0 + STRIP, :] = m_new
        al_ref[r0:r0 + STRIP, :] = alpha
        l_ref[r0:r0 + STRIP, :] = alpha * l_ref[r0:r0 + STRIP, :] + psum


def _all_visible(r0, c):
    return None


def _init_softmax_state(m_refs, l_refs, acc_refs):
    for m_ref, l_ref, acc_ref in zip(m_refs, l_refs, acc_refs):
        m_ref[...] = jnp.full(m_ref.shape, NEG, F32)
        l_ref[...] = jnp.zeros(l_ref.shape, F32)
        acc_ref[...] = jnp.zeros(acc_ref.shape, F32)


def _softmax_scratch(tq, tk):
    per_stream = [pltpu.VMEM((tq, tk), F32), pltpu.VMEM((tq, tk), BF16), pltpu.VMEM((tq, LANES), F32),
                  pltpu.VMEM((tq, LANES), F32), pltpu.VMEM((tq, LANES), F32), pltpu.VMEM((tq, LANES), F32)]
    return per_stream + per_stream


def _row_total(l_ref):
    return jnp.sum(l_ref[...], axis=1, keepdims=True)


def _sweep(qi, tq, scores, v_ref, scratch, diag_chunk_state):
    s_refs, p_refs, m_refs, l_refs, al_refs, acc_refs = (scratch[i::6] for i in range(6))

    def softmax(chunk_state):
        for x in range(2):
            _softmax_tile(s_refs[x], p_refs[x], m_refs[x], l_refs[x], al_refs[x], chunk_state)

    def accumulate(j):
        vb = v_ref[pl.ds(pl.multiple_of(j * tq, tq), tq), :]
        for x in range(2):
            acc_refs[x][...] = al_refs[x][...] * acc_refs[x][...] + _dot(p_refs[x][...], vb)

    scores(0)

    def body(j, c):
        softmax(_all_visible)
        scores(j + 1)
        accumulate(j)
        return c

    lax.fori_loop(0, qi, body, 0)
    softmax(diag_chunk_state)
    accumulate(qi)


def _diff_chunk_state(r0, c):
    q_chunk = r0 // CHUNK
    k_lo, k_hi = (c * LANES) // CHUNK, (c * LANES + LANES - 1) // CHUNK
    if k_hi <= q_chunk:
        return None
    if k_lo > q_chunk:
        return False
    return lax.broadcasted_iota(jnp.int32, (STRIP, LANES), 1) < CHUNK


def _diff_kernel(tq, lam_init, q_ref, k_ref, v_ref, lamv_ref, g_ref, o_ref, *scratch):
    s_refs, p_refs, m_refs, l_refs, al_refs, acc_refs = (scratch[i::6] for i in range(6))
    qi = pl.program_id(1)
    qs = _split_q(q_ref[...])
    _init_softmax_state(m_refs, l_refs, acc_refs)

    def scores(j):
        kb = k_ref[pl.ds(pl.multiple_of(j * tq, tq), tq), :]
        for x in range(2):
            s_refs[x][...] = _dot_nt(qs[x], kb)

    _sweep(qi, tq, scores, v_ref, scratch, _diff_chunk_state)

    lv = lamv_ref[...]
    lam = (jnp.exp(jnp.sum(lv[0:1] * lv[1:2], axis=1, keepdims=True))
           - jnp.exp(jnp.sum(lv[2:3] * lv[3:4], axis=1, keepdims=True)) + lam_init)
    o = (acc_refs[0][...] / _row_total(l_refs[0])
         - lam * (acc_refs[1][...] / _row_total(l_refs[1])))
    o = o * lax.rsqrt(jnp.mean(o * o, axis=-1, keepdims=True) + 1e-6) * g_ref[...]
    o_ref[...] = (o * (1.0 - lam_init)).astype(BF16)


def _diff_call(dq, dk, dv, lamv, gnorm, lam_init, tq, tk):
    S = dq.shape[0]
    nh = dq.shape[1] // LANES
    qspec = pl.BlockSpec((tq, LANES), lambda h, i: (i, h))
    kvspec = pl.BlockSpec((S, LANES), lambda h, i: (0, h))
    full = lambda a: pl.BlockSpec(a.shape, lambda h, i: (0, 0))
    return pl.pallas_call(
        functools.partial(_diff_kernel, tq, lam_init),
        grid=(nh, S // tq),
        in_specs=[qspec, kvspec, kvspec, full(lamv), full(gnorm)],
        out_specs=qspec,
        out_shape=jax.ShapeDtypeStruct(dq.shape, BF16),
        scratch_shapes=_softmax_scratch(tq, tq),
        compiler_params=_cparams(("parallel", "arbitrary")),
        name="diff_attn",
    )(dq, dk, dv, lamv, gnorm)


def _causal_chunk_state(strict, col0=0):
    def state(r0, c):
        k_lo, k_hi = col0 + c * LANES, col0 + c * LANES + LANES - 1
        q_lo, q_hi = r0, r0 + STRIP - 1
        if k_hi < q_lo or (not strict and k_hi <= q_lo):
            return None
        if k_lo > q_hi or (strict and k_lo >= q_hi):
            return False
        row = r0 + lax.broadcasted_iota(jnp.int32, (STRIP, LANES), 0)
        col = k_lo + lax.broadcasted_iota(jnp.int32, (STRIP, LANES), 1)
        return col < row if strict else col <= row
    return state


def _fox_kernel(tq, q_ref, k_ref, v_ref, f_ref, o_ref, *scratch):
    s_refs, p_refs, m_refs, l_refs, al_refs, acc_refs = (scratch[i::6] for i in range(6))
    qi = pl.program_id(1)
    qs = _split_q(q_ref[...])
    _init_softmax_state(m_refs, l_refs, acc_refs)
    f0 = [f_ref[x, qi][:, 0:1] for x in range(2)]

    def scores(j):
        kb = k_ref[pl.ds(pl.multiple_of(j * tq, tq), tq), :]
        for x in range(2):
            s_refs[x][...] = _dot_nt(qs[x], kb) + (f0[x] - f_ref[x, j]) * LOG2E

    _sweep(qi, tq, scores, v_ref, scratch, _causal_chunk_state(strict=False))

    lane = lax.broadcasted_iota(jnp.int32, (1, LANES), 1)
    o = jnp.where(lane < HEAD_DIM, acc_refs[0][...] / _row_total(l_refs[0]),
                  acc_refs[1][...] / _row_total(l_refs[1]))
    o_ref[...] = o.astype(BF16)


def _fox_call(fq, fk, fv, fcum, tq, tk):
    S = fq.shape[0]
    npair = fq.shape[1] // LANES
    qspec = pl.BlockSpec((tq, LANES), lambda p, i: (i, p))
    kvspec = pl.BlockSpec((S, LANES), lambda p, i: (0, p))
    fspec = pl.BlockSpec((2, S // tk, 1, tk), lambda p, i: (p, 0, 0, 0))
    assert tk == tq
    return pl.pallas_call(
        functools.partial(_fox_kernel, tq),
        grid=(npair, S // tq),
        in_specs=[qspec, kvspec, kvspec, fspec],
        out_specs=qspec,
        out_shape=jax.ShapeDtypeStruct(fq.shape, BF16),
        scratch_shapes=_softmax_scratch(tq, tq),
        compiler_params=_cparams(("parallel", "arbitrary")),
        name="fox_attn",
    )(fq, fk, fv, fcum)


def _sb_kernel(tq, tk, q_ref, k_ref, v_ref, tri_ref, o_ref, *scratch):
    z_refs, u_refs, lb_refs, a_refs, tot_refs, carry_refs, acc_refs = (scratch[i::7] for i in range(7))
    qi = pl.program_id(1)
    qs = _split_q(q_ref[...])
    for x in range(2):
        carry_refs[x][...] = jnp.zeros((tq, LANES), F32)
        acc_refs[x][...] = jnp.zeros((tq, LANES), F32)
    assert tk == tq and tk == 2 * TRI
    nc = tk // LANES

    def scores(j):
        kb = k_ref[pl.ds(pl.multiple_of(j * tk, tk), tk), :]
        for x in range(2):
            z_refs[x][...] = _dot_nt(qs[x], kb)

    def gates(chunk_state):
        for x in range(2):
            z_ref, u_ref, lb_ref = z_refs[x], u_refs[x], lb_refs[x]
            for r0 in range(0, tq, STRIP):
                usum = jnp.zeros((STRIP, LANES), F32)
                for c in range(nc):
                    st = chunk_state(r0, c)
                    if st is False:
                        u_ref[r0:r0 + STRIP, c * LANES:(c + 1) * LANES] = jnp.zeros((STRIP, LANES), BF16)
                        lb_ref[r0:r0 + STRIP, c * LANES:(c + 1) * LANES] = jnp.zeros((STRIP, LANES), F32)
                        continue
                    z = _chunk(z_ref, r0, c)
                    sp = jnp.maximum(z, 0.0) + LOG2E * jnp.log(1.0 + jnp.exp2(-jnp.abs(z)))
                    u = sp if st is None else jnp.where(st, sp, 0.0)
                    u_ref[r0:r0 + STRIP, c * LANES:(c + 1) * LANES] = u.astype(BF16)
                    lb_ref[r0:r0 + STRIP, c * LANES:(c + 1) * LANES] = z - sp
                    usum = usum + u
                tot_refs[x][r0:r0 + STRIP, :] = (jnp.sum(usum, axis=1, keepdims=True)
                                                 + jnp.zeros((STRIP, LANES), F32))

    def weights_and_accumulate(j, masked):
        vb = v_ref[pl.ds(pl.multiple_of(j * tk, tk), tk), :]
        tri_stack = tri_ref[...]
        for x in range(2):
            u_ref, lb_ref, a_ref = u_refs[x], lb_refs[x], a_refs[x]
            suffix = (_dot(u_ref[...], tri_stack), _dot(u_ref[:, TRI:], tri_stack[:TRI]))
            carry = carry_refs[x][...]
            for c in range(nc):
                cols = slice(c * LANES, (c + 1) * LANES)
                blk, off = divmod(c * LANES, TRI)
                suf = suffix[blk][:, off:off + LANES]
                a = jnp.exp2(lb_ref[:, cols] - suf - carry)
                if masked:
                    row = lax.broadcasted_iota(jnp.int32, (tq, LANES), 0)
                    col = c * LANES + lax.broadcasted_iota(jnp.int32, (tq, LANES), 1)
                    a = jnp.where(col < row, a, 0.0)
                a_ref[:, cols] = a.astype(BF16)
            acc_refs[x][...] += _dot(a_ref[...], vb)
            carry_refs[x][...] = carry + tot_refs[x][...]

    scores(qi)
    gates(_causal_chunk_state(True))
    scores(jnp.maximum(qi - 1, 0))
    weights_and_accumulate(qi, True)

    def body(i, c):
        j = qi - 1 - i
        gates(_all_visible)
        scores(jnp.maximum(j - 1, 0))
        weights_and_accumulate(j, False)
        return c

    lax.fori_loop(0, qi, body, 0)

    lane = lax.broadcasted_iota(jnp.int32, (1, LANES), 1)
    o_ref[...] = jnp.where(lane < HEAD_DIM, acc_refs[0][...], acc_refs[1][...]).astype(BF16)


def _sb_call(sq, sk, sv, tq, tk):
    S = sq.shape[0]
    npair = sq.shape[1] // LANES
    qspec = pl.BlockSpec((tq, LANES), lambda p, i: (i, p))
    kvspec = pl.BlockSpec((S, LANES), lambda p, i: (0, p))
    tri = jnp.asarray(np.concatenate([np.tril(np.ones((TRI, TRI), np.float32), -1),
                                      np.ones((TRI, TRI), np.float32)]), BF16)
    return pl.pallas_call(
        functools.partial(_sb_kernel, tq, tk),
        grid=(npair, S // tq),
        in_specs=[qspec, kvspec, kvspec, pl.BlockSpec((2 * TRI, TRI), lambda p, i: (0, 0))],
        out_specs=qspec,
        out_shape=jax.ShapeDtypeStruct(sq.shape, BF16),
        scratch_shapes=2 * [pltpu.VMEM((tq, tk), F32), pltpu.VMEM((tq, tk), BF16),
                            pltpu.VMEM((tq, tk), F32), pltpu.VMEM((tq, tk), BF16),
                            pltpu.VMEM((tq, LANES), F32), pltpu.VMEM((tq, LANES), F32),
                            pltpu.VMEM((tq, LANES), F32)],
        compiler_params=_cparams(("parallel", "arbitrary")),
        name="sb_attn",
    )(sq, sk, sv, tri)


def _merge_kernel(hb_ref, h_ref, od_ref, os_ref, of_ref, wg_ref, wud_ref, wus_ref, wuf_ref, wo_ref,
                  g_ref, b_ref, out_ref, outb_ref):
    x = hb_ref[...]
    D = x.shape[1]
    merged = None
    for c, (o_ref, wu_ref) in enumerate(((od_ref, wud_ref), (os_ref, wus_ref), (of_ref, wuf_ref))):
        gate = _sigmoid(_dot(x, wg_ref[:, c * D:(c + 1) * D]))
        term = gate * _dot(o_ref[...], wu_ref[...])
        merged = term if merged is None else merged + term
    y = _dot(merged.astype(BF16), wo_ref[...])
    hn = _layer_norm(ALPHA * h_ref[...] + y, g_ref[...], b_ref[...])
    out_ref[...] = hn
    outb_ref[...] = hn.astype(BF16)


def _merge_call(hb, h, od, osb, of, wg, wud, wus, wuf, wo, g, b, tm):
    S, D = h.shape
    row = lambda a: pl.BlockSpec((tm, a.shape[1]), lambda i: (i, 0))
    full = lambda a: pl.BlockSpec(a.shape, lambda i: (0, 0))
    args = (hb, h, od, osb, of, wg, wud, wus, wuf, wo, g, b)
    in_specs = [row(a) for a in args[:5]] + [full(a) for a in args[5:]]
    return pl.pallas_call(
        _merge_kernel,
        grid=(S // tm,),
        in_specs=in_specs,
        out_specs=[row(h), row(h)],
        out_shape=[jax.ShapeDtypeStruct((S, D), F32), jax.ShapeDtypeStruct((S, D), BF16)],
        compiler_params=_cparams(("parallel",)),
        name="merge",
    )(*args)


def _swiglu_hidden(x, wg, wu):
    gate = _dot(x, wg)
    return (gate * _sigmoid(gate)) * _dot(x, wu)


def _ffn_kernel(hb_ref, h_ref, wg_ref, wu_ref, wd_ref, g_ref, b_ref, out_ref, outb_ref, acc_ref):
    f = pl.program_id(1)

    @pl.when(f == 0)
    def _():
        acc_ref[...] = jnp.zeros(acc_ref.shape, F32)

    hid = _swiglu_hidden(hb_ref[...], wg_ref[...], wu_ref[...])
    acc_ref[...] += _dot(hid.astype(BF16), wd_ref[...])

    @pl.when(f == pl.num_programs(1) - 1)
    def _():
        hn = _layer_norm(ALPHA * h_ref[...] + acc_ref[...], g_ref[...], b_ref[...])
        out_ref[...] = hn
        outb_ref[...] = hn.astype(BF16)


def _ffn_call(hb, h, wg, wu, wd, g, b, tm, tf):
    S, D = h.shape
    F = wg.shape[1]
    row = pl.BlockSpec((tm, D), lambda i, f: (i, 0))
    vec = pl.BlockSpec((1, D), lambda i, f: (0, 0))
    return pl.pallas_call(
        _ffn_kernel,
        grid=(S // tm, F // tf),
        in_specs=[row, row,
                  pl.BlockSpec((D, tf), lambda i, f: (0, f)),
                  pl.BlockSpec((D, tf), lambda i, f: (0, f)),
                  pl.BlockSpec((tf, D), lambda i, f: (f, 0)),
                  vec, vec],
        out_specs=[row, row],
        out_shape=[jax.ShapeDtypeStruct((S, D), F32), jax.ShapeDtypeStruct((S, D), BF16)],
        scratch_shapes=[pltpu.VMEM((tm, D), F32)],
        compiler_params=_cparams(("parallel", "arbitrary")),
        name="ffn",
    )(hb, h, wg, wu, wd, g, b)


MOE_TILE = 512


def _route_kernel(h_ref, w_ref, b_ref, sel_ref, i1_ref, i2_ref, w1_ref, w2_ref):
    logits = jnp.dot(h_ref[...], w_ref[...], precision=lax.Precision.HIGHEST,
                     preferred_element_type=F32) + b_ref[...]
    lane = lax.broadcasted_iota(jnp.int32, logits.shape, 1)
    logits = jnp.where(lane < N_EXPERTS, logits, NEG)
    v1 = jnp.max(logits, axis=1, keepdims=True)
    i1 = jnp.min(jnp.where(logits == v1, lane, LANES), axis=1, keepdims=True)
    rest = jnp.where(lane == i1, NEG, logits)
    v2 = jnp.max(rest, axis=1, keepdims=True)
    i2 = jnp.min(jnp.where(rest == v2, lane, LANES), axis=1, keepdims=True)
    e2 = jnp.exp(v2 - v1)
    zero = jnp.zeros(logits.shape, F32)
    sel_ref[...] = jnp.where((lane == i1) | (lane == i2), 1.0, 0.0).astype(BF16)
    i1_ref[...] = i1 + jnp.zeros(logits.shape, jnp.int32)
    i2_ref[...] = i2 + jnp.zeros(logits.shape, jnp.int32)
    w1_ref[...] = 1.0 / (1.0 + e2) + zero
    w2_ref[...] = e2 / (1.0 + e2) + zero


def _route_call(h, w_router, b_router, tm):
    S, D = h.shape
    wpad = jnp.zeros((D, LANES), F32).at[:, :N_EXPERTS].set(w_router)
    bpad = jnp.zeros((1, LANES), F32).at[0, :N_EXPERTS].set(b_router)
    row = pl.BlockSpec((tm, LANES), lambda i: (i, 0))
    return pl.pallas_call(
        _route_kernel,
        grid=(S // tm,),
        in_specs=[pl.BlockSpec((tm, D), lambda i: (i, 0)),
                  pl.BlockSpec((D, LANES), lambda i: (0, 0)),
                  pl.BlockSpec((1, LANES), lambda i: (0, 0))],
        out_specs=[row] * 5,
        out_shape=[jax.ShapeDtypeStruct((S, LANES), BF16),
                   jax.ShapeDtypeStruct((S, LANES), jnp.int32),
                   jax.ShapeDtypeStruct((S, LANES), jnp.int32),
                   jax.ShapeDtypeStruct((S, LANES), F32),
                   jax.ShapeDtypeStruct((S, LANES), F32)],
        compiler_params=_cparams(("parallel",)),
        name="route",
    )(h, wpad, bpad)


def _rank_kernel(sel_ref, i1_ref, i2_ref, tril_ref, rank_ref, cnt_ref):
    @pl.when(pl.program_id(0) == 0)
    def _():
        cnt_ref[...] = jnp.zeros(cnt_ref.shape, F32)

    sel = sel_ref[...]
    rank = _dot(tril_ref[...], sel) + cnt_ref[...]
    lane = lax.broadcasted_iota(jnp.int32, rank.shape, 1)
    r1 = jnp.sum(jnp.where(lane == i1_ref[...], rank, 0.0), axis=1, keepdims=True)
    r2 = jnp.sum(jnp.where(lane == i2_ref[...], rank, 0.0), axis=1, keepdims=True)
    rank_ref[...] = jnp.where(lane == 0, r1, jnp.where(lane == 1, r2, 0.0))
    cnt_ref[...] += jnp.sum(sel.astype(F32), axis=0, keepdims=True)


def _rank_call(sel, i1b, i2b, tb):
    S = sel.shape[0]
    tril = jnp.asarray(np.tril(np.ones((tb, tb), np.float32), -1), BF16)
    row = pl.BlockSpec((tb, LANES), lambda i: (i, 0))
    return pl.pallas_call(
        _rank_kernel,
        grid=(S // tb,),
        in_specs=[row, row, row, pl.BlockSpec((tb, tb), lambda i: (0, 0))],
        out_specs=[row, pl.BlockSpec((1, LANES), lambda i: (0, 0))],
        out_shape=[jax.ShapeDtypeStruct((S, LANES), F32), jax.ShapeDtypeStruct((1, LANES), F32)],
        compiler_params=_cparams(("arbitrary",)),
        name="rank",
    )(sel, i1b, i2b, tril)


def _row_copy(src_hbm, s, dst_hbm, d, sem):
    return pltpu.make_async_copy(src_hbm.at[pl.ds(s, 1)], dst_hbm.at[pl.ds(d, 1)], sem)


def _dispatch_kernel(tb, n_tok, dest_ref, h_hbm, xs_in_hbm, xs_hbm, sem):
    del xs_in_hbm
    base = pl.program_id(0) * tb

    def issue(t, c):
        _row_copy(h_hbm, base + t, xs_hbm, dest_ref[base + t], sem).start()
        _row_copy(h_hbm, base + t, xs_hbm, dest_ref[n_tok + base + t], sem).start()
        return c

    lax.fori_loop(0, tb, issue, 0)

    def drain(t, c):
        _row_copy(h_hbm, 0, xs_hbm, 0, sem).wait()
        _row_copy(h_hbm, 0, xs_hbm, 0, sem).wait()
        return c

    lax.fori_loop(0, tb, drain, 0)


def _dispatch_call(dest, h, n_rows, tb):
    S, D = h.shape
    xs0 = jnp.zeros((n_rows, D), F32)
    return pl.pallas_call(
        functools.partial(_dispatch_kernel, tb, S),
        grid_spec=pltpu.PrefetchScalarGridSpec(
            num_scalar_prefetch=1,
            grid=(S // tb,),
            in_specs=[pl.BlockSpec(memory_space=pl.ANY), pl.BlockSpec(memory_space=pl.ANY)],
            out_specs=pl.BlockSpec(memory_space=pl.ANY),
            scratch_shapes=[pltpu.SemaphoreType.DMA],
        ),
        out_shape=jax.ShapeDtypeStruct((n_rows, D), F32),
        input_output_aliases={2: 0},
        compiler_params=_cparams(("arbitrary",)),
        name="dispatch",
    )(dest, h, xs0)


def _experts_kernel(te_ref, nu_ref, xs_ref, wg_ref, wu_ref, wd_ref, ys_ref, xb_ref, acc_ref):
    i = pl.program_id(0)
    f = pl.program_id(1)
    last_f = pl.num_programs(1) - 1
    used = i < nu_ref[0]

    @pl.when(used & (f == 0))
    def _():
        xb_ref[...] = xs_ref[...].astype(BF16)

    @pl.when(used)
    def _():
        hid = _swiglu_hidden(xb_ref[...], wg_ref[0], wu_ref[0])
        part = _dot(hid.astype(BF16), wd_ref[0])

        @pl.when(f == 0)
        def _():
            acc_ref[...] = part

        @pl.when(f > 0)
        def _():
            acc_ref[...] += part

    @pl.when(used & (f == last_f))
    def _():
        ys_ref[...] = acc_ref[...]

    @pl.when(jnp.logical_not(used) & (f == last_f))
    def _():
        ys_ref[...] = jnp.zeros(ys_ref.shape, F32)


def _experts_call(tile_expert, n_used, xs, wg, wu, wd, tmx, tf):
    P, D = xs.shape
    F = wg.shape[2]
    nf = F // tf

    def fsel(i, f, nu):
        return jnp.where(i < nu[0], f, nf - 1)

    return pl.pallas_call(
        _experts_kernel,
        grid_spec=pltpu.PrefetchScalarGridSpec(
            num_scalar_prefetch=2,
            grid=(P // tmx, nf),
            in_specs=[pl.BlockSpec((tmx, D), lambda i, f, te, nu: (i, 0)),
                      pl.BlockSpec((1, D, tf), lambda i, f, te, nu: (te[i], 0, fsel(i, f, nu))),
                      pl.BlockSpec((1, D, tf), lambda i, f, te, nu: (te[i], 0, fsel(i, f, nu))),
                      pl.BlockSpec((1, tf, D), lambda i, f, te, nu: (te[i], fsel(i, f, nu), 0))],
            out_specs=pl.BlockSpec((tmx, D), lambda i, f, te, nu: (i, 0)),
            scratch_shapes=[pltpu.VMEM((tmx, D), BF16), pltpu.VMEM((tmx, D), F32)],
        ),
        out_shape=jax.ShapeDtypeStruct((P, D), F32),
        compiler_params=_cparams(("arbitrary", "arbitrary")),
        name="experts",
    )(tile_expert, n_used, xs, wg, wu, wd)


def _combine_kernel(tm, n_tok, dest_ref, ys_hbm, h_ref, w1_ref, w2_ref, g_ref, b_ref,
                    out_ref, outb_ref, buf_ref, sem):
    i = pl.program_id(0)
    slot = i % 2

    def gather(tile, slot_, start):
        base = tile * tm

        def one(t, c):
            for k in range(2):
                src_row = dest_ref[k * n_tok + base + t] if start else 0
                cp = pltpu.make_async_copy(ys_hbm.at[pl.ds(src_row, 1)],
                                           buf_ref.at[slot_, k, pl.ds(t, 1)], sem.at[slot_])
                if start:
                    cp.start()
                else:
                    cp.wait()
            return c

        lax.fori_loop(0, tm, one, 0)

    @pl.when(i == 0)
    def _():
        gather(0, 0, True)

    @pl.when(i + 1 < pl.num_programs(0))
    def _():
        gather(i + 1, 1 - slot, True)

    gather(i, slot, False)

    w1 = w1_ref[...]
    w2 = w2_ref[...]
    y = jnp.concatenate(
        [w1 * buf_ref[slot, 0, :, c * LANES:(c + 1) * LANES]
         + w2 * buf_ref[slot, 1, :, c * LANES:(c + 1) * LANES]
         for c in range(h_ref.shape[1] // LANES)], axis=1)
    hn = _layer_norm(ALPHA * h_ref[...] + y, g_ref[...], b_ref[...])
    out_ref[...] = hn
    outb_ref[...] = hn.astype(BF16)


def _combine_call(dest, ys, h, w1b, w2b, g, b, tm):
    S, D = h.shape
    row = pl.BlockSpec((tm, D), lambda i, d: (i, 0))
    lrow = pl.BlockSpec((tm, LANES), lambda i, d: (i, 0))
    vec = pl.BlockSpec((1, D), lambda i, d: (0, 0))
    return pl.pallas_call(
        functools.partial(_combine_kernel, tm, S),
        grid_spec=pltpu.PrefetchScalarGridSpec(
            num_scalar_prefetch=1,
            grid=(S // tm,),
            in_specs=[pl.BlockSpec(memory_space=pl.ANY), row, lrow, lrow, vec, vec],
            out_specs=[row, row],
            scratch_shapes=[pltpu.VMEM((2, 2, tm, D), F32), pltpu.SemaphoreType.DMA((2,))],
        ),
        out_shape=[jax.ShapeDtypeStruct((S, D), F32), jax.ShapeDtypeStruct((S, D), BF16)],
        compiler_params=_cparams(("arbitrary",)),
        name="combine",
    )(dest, ys, h, w1b, w2b, g, b)


def _moe_call(h, w_router, b_router, wg, wu, wd, g, b, tm, tf):
    S, D = h.shape
    tmx = MOE_TILE
    sel, i1b, i2b, w1b, w2b = _route_call(h, w_router, b_router, tm)
    ranks, counts = _rank_call(sel, i1b, i2b, min(256, S))
    cnt = counts[0, :N_EXPERTS].astype(jnp.int32)
    padded = ((cnt + tmx - 1) // tmx) * tmx
    ends = jnp.cumsum(padded)
    off = ends - padded
    dest = jnp.concatenate([off[i1b[:, 0]] + ranks[:, 0].astype(jnp.int32),
                            off[i2b[:, 0]] + ranks[:, 1].astype(jnp.int32)])
    n_tiles = (2 * S) // tmx + N_EXPERTS
    tile_start = jnp.arange(n_tiles, dtype=jnp.int32) * tmx
    tile_expert = jnp.minimum(jnp.sum(tile_start[:, None] >= ends[None, :], axis=1),
                              N_EXPERTS - 1).astype(jnp.int32)
    n_used = (ends[-1] // tmx).astype(jnp.int32).reshape(1)
    xs = _dispatch_call(dest, h, n_tiles * tmx, tm)
    ys = _experts_call(tile_expert, n_used, xs, wg, wu, wd, tmx, tf)
    return _combine_call(dest, ys, h, w1b, w2b, g, b, min(256, S))


def _rope_tables(S):
    half = HEAD_DIM // 2
    pos = jnp.arange(S, dtype=F32)
    inv = ROPE_THETA ** (-jnp.arange(half, dtype=F32) / half)
    ang = pos[:, None] * inv[None, :]
    cos, sin = jnp.cos(ang), jnp.sin(ang)
    reps = LANES // HEAD_DIM
    cos_t = jnp.tile(jnp.concatenate([cos, cos], axis=1), (1, reps))
    sin_t = jnp.tile(jnp.concatenate([-sin, sin], axis=1), (1, reps))
    return cos_t, sin_t


def _pick(S, pref):
    t = min(pref, S)
    assert S % t == 0
    return t


def _forward(x, ln_in_g, ln_in_b, w_in, b_forget, lam_q1, lam_k1, lam_q2, lam_k2,
             diff_norm_g, w_up_diff, w_up_sb, w_up_fox, w_out, ln_mix_g, ln_mix_b,
             ln_ffn_g, ln_ffn_b, w_gate_dense, w_up_dense, w_down_dense, w_router,
             b_router, w_gate_moe, w_up_moe, w_down_moe):
    B, S, D = x.shape
    assert B == 1 and D == D_MODEL and S % LANES == 0
    depth = w_in.shape[0]
    tm = _pick(S, 512)
    tq = _pick(S, 512)
    tk = _pick(S, 256)
    tm_ffn = _pick(S, 1024)
    tf = 512
    vec = lambda a: a.reshape(1, -1).astype(F32)

    cos_t, sin_t = _rope_tables(S)
    h, hb = _ln_call(x.reshape(S, D), ln_in_g, ln_in_b, tm)
    w_atts, w_forgets, w_gate_logits = _wsplit_call(w_in, 128)
    for l in range(depth):
        w_att, w_forget, w_gate_logit = w_atts[l], w_forgets[l], w_gate_logits[l]
        fbias = jnp.pad(b_forget[l].astype(F32), (0, LANES - N_HEADS_FOX)).reshape(1, LANES)
        dq, dk, dv, sq, sk, sv, fq, fk, fv, logf = _proj_call(hb, w_att, w_forget, cos_t, sin_t,
                                                              fbias, tm)

        lf = logf[:, :N_HEADS_FOX].T.reshape(N_HEADS_FOX, S // LANES, LANES)
        fcum = _fcum_call(lf).reshape(N_HEADS_FOX, S // tq, 1, tq)

        lam_init = 0.8 - 0.6 * math.exp(-0.3 * l)
        lamv = jnp.stack([lam_q1[l], lam_k1[l], lam_q2[l], lam_k2[l]]).astype(F32)
        o_diff = _diff_call(dq, dk, dv, lamv, vec(diff_norm_g[l]), lam_init, tq, tk)
        o_sb = _sb_call(sq, sk, sv, tq, tq)
        o_fox = _fox_call(fq, fk, fv, fcum, tq, tq)

        h, hb = _merge_call(hb, h, o_diff, o_sb, o_fox, w_gate_logit,
                            w_up_diff[l].astype(BF16), w_up_sb[l].astype(BF16),
                            w_up_fox[l].astype(BF16), w_out[l].astype(BF16),
                            vec(ln_mix_g[l]), vec(ln_mix_b[l]), tm)
        j = l // 2
        if l % 2 == 0:
            h, hb = _ffn_call(hb, h, w_gate_dense[j].astype(BF16), w_up_dense[j].astype(BF16),
                              w_down_dense[j].astype(BF16), vec(ln_ffn_g[l]), vec(ln_ffn_b[l]),
                              tm_ffn, tf)
        else:
            h, hb = _moe_call(h, w_router[j], b_router[j], w_gate_moe[j].astype(BF16),
                              w_up_moe[j].astype(BF16), w_down_moe[j].astype(BF16),
                              vec(ln_ffn_g[l]), vec(ln_ffn_b[l]), tm, tf)
    return h.reshape(B, S, D)


def kernel(x, ln_in_g, ln_in_b, w_in, b_forget, lam_q1, lam_k1, lam_q2, lam_k2, diff_norm_g,
           w_up_diff, w_up_sb, w_up_fox, w_out, ln_mix_g, ln_mix_b, ln_ffn_g, ln_ffn_b,
           w_gate_dense, w_up_dense, w_down_dense, w_router, b_router, w_gate_moe, w_up_moe,
           w_down_moe):
    return _forward(x, ln_in_g, ln_in_b, w_in, b_forget, lam_q1, lam_k1, lam_q2, lam_k2,
                    diff_norm_g, w_up_diff, w_up_sb, w_up_fox, w_out, ln_mix_g, ln_mix_b,
                    ln_ffn_g, ln_ffn_b, w_gate_dense, w_up_dense, w_down_dense, w_router,
                    b_router, w_gate_moe, w_up_moe, w_down_moe)
```

```python
import functools
import math

import jax
import jax.numpy as jnp
import numpy as np
from jax import lax
from jax.experimental import pallas as pl
from jax.experimental.pallas import tpu as pltpu

D_MODEL = 1024
DEPTH = 2
CHUNK = 64
HEAD_DIM = 64
N_HEADS_DIFF = 4
N_HEADS_SB = 4
N_HEADS_FOX = 4
WIDTH_DIFF = N_HEADS_DIFF * 2 * HEAD_DIM
WIDTH_SB = N_HEADS_SB * HEAD_DIM
WIDTH_FOX = N_HEADS_FOX * HEAD_DIM
ROPE_THETA = 10000.0
N_EXPERTS = 8
D_FF = 3584
ALPHA = (2 * DEPTH) ** 0.25
Q_SCALE = HEAD_DIM ** -0.5
LOG2E = 1.4426950408889634
N_ATT = 3 * WIDTH_DIFF + 3 * WIDTH_SB + 3 * WIDTH_FOX
LANES = 128
NEG = -1e30

F32 = jnp.float32
BF16 = jnp.bfloat16

VMEM_LIMIT = 56 * 1024 * 1024


def _cparams(sem):
    return pltpu.CompilerParams(dimension_semantics=sem, vmem_limit_bytes=VMEM_LIMIT)


def _layer_norm(z, g, b, eps=1e-5):
    mu = jnp.mean(z, axis=-1, keepdims=True)
    zc = z - mu
    var = jnp.mean(zc * zc, axis=-1, keepdims=True)
    return zc * lax.rsqrt(var + eps) * g + b


def _sigmoid(x):
    return 1.0 / (1.0 + jnp.exp(-x))


def _softplus(x):
    return jnp.maximum(x, 0.0) + jnp.log1p(jnp.exp(-jnp.abs(x)))


def _dot(a, b):
    return jnp.dot(a, b, preferred_element_type=F32)


def _dot_nt(a, b):
    return lax.dot_general(a, b, (((1,), (1,)), ((), ())), preferred_element_type=F32)


def _split3(x):
    x1 = x.astype(BF16)
    r = x - x1.astype(F32)
    x2 = r.astype(BF16)
    x3 = (r - x2.astype(F32)).astype(BF16)
    return x1, x2, x3


def _ln_kernel(x_ref, g_ref, b_ref, h_ref, hb_ref):
    h = _layer_norm(x_ref[...], g_ref[...], b_ref[...])
    h_ref[...] = h
    hb_ref[...] = h.astype(BF16)


def _ln_call(x, g, b, tm):
    S, D = x.shape
    row = pl.BlockSpec((tm, D), lambda i: (i, 0))
    vec = pl.BlockSpec((1, D), lambda i: (0, 0))
    return pl.pallas_call(
        _ln_kernel,
        grid=(S // tm,),
        in_specs=[row, vec, vec],
        out_specs=[row, row],
        out_shape=[jax.ShapeDtypeStruct((S, D), F32), jax.ShapeDtypeStruct((S, D), BF16)],
        compiler_params=_cparams(("parallel",)),
        name="ln_in",
    )(x, g.reshape(1, D), b.reshape(1, D))


def _wsplit_kernel(w_ref, att_ref, fgt_ref, gate_ref):
    att_ref[0] = w_ref[0, :, :N_ATT].astype(BF16)
    lane = lax.broadcasted_iota(jnp.int32, (1, LANES), 1)
    fgt_ref[0] = jnp.where(lane < N_HEADS_FOX, w_ref[0, :, N_ATT:N_ATT + LANES], 0.0).astype(BF16)
    gate_ref[0] = w_ref[0, :, N_ATT + N_HEADS_FOX:].astype(BF16)


def _wsplit_call(w_in, tr):
    depth, D, n_in = w_in.shape
    n_gate = n_in - N_ATT - N_HEADS_FOX
    spec = lambda n: pl.BlockSpec((1, tr, n), lambda l, i: (l, i, 0))
    return pl.pallas_call(
        _wsplit_kernel,
        grid=(depth, D // tr),
        in_specs=[spec(n_in)],
        out_specs=[spec(N_ATT), spec(LANES), spec(n_gate)],
        out_shape=[jax.ShapeDtypeStruct((depth, D, N_ATT), BF16),
                   jax.ShapeDtypeStruct((depth, D, LANES), BF16),
                   jax.ShapeDtypeStruct((depth, D, n_gate), BF16)],
        compiler_params=_cparams(("parallel", "parallel")),
        name="wsplit",
    )(w_in)


def _proj_kernel(hb_ref, w_ref, wf_ref, cos_ref, sin_ref, fb_ref,
                 dq_ref, dk_ref, dv_ref, sq_ref, sk_ref, sv_ref, fq_ref, fk_ref, fv_ref, lf_ref):
    x = hb_ref[...]
    cos = cos_ref[...]
    sin = sin_ref[...]
    lane = lax.broadcasted_iota(jnp.int32, (1, LANES), 1)
    lower = (lane & (HEAD_DIM // 2)) == 0

    def mm(c0, c1):
        return _dot(x, w_ref[:, c0:c1])

    def rope_store(y, out_ref, scale):
        for g in range(y.shape[1] // LANES):
            yg = y[:, g * LANES:(g + 1) * LANES]
            partner = jnp.where(lower, pltpu.roll(yg, LANES - HEAD_DIM // 2, 1),
                                pltpu.roll(yg, HEAD_DIM // 2, 1))
            r = yg * cos + partner * sin
            if scale != 1.0:
                r = r * scale
            out_ref[:, g * LANES:(g + 1) * LANES] = r.astype(BF16)

    rope_store(mm(0, 512), dq_ref, Q_SCALE * LOG2E)
    rope_store(mm(512, 1024), dk_ref, 1.0)
    dv_ref[...] = mm(1024, 1536).astype(BF16)
    sq_ref[...] = (mm(1536, 1792) * (Q_SCALE * LOG2E)).astype(BF16)
    sk_ref[...] = mm(1792, 2048).astype(BF16)
    sv_ref[...] = mm(2048, 2304).astype(BF16)
    fq_ref[...] = (mm(2304, 2560) * (Q_SCALE * LOG2E)).astype(BF16)
    fk_ref[...] = mm(2560, 2816).astype(BF16)
    fv_ref[...] = mm(2816, 3072).astype(BF16)
    lf_ref[...] = -_softplus(-(_dot(x, wf_ref[...]) + fb_ref[...]))


def _proj_call(hb, w_att, w_forget, cos, sin, fbias, tm):
    S, D = hb.shape
    row = lambda n: pl.BlockSpec((tm, n), lambda i: (i, 0))
    full = lambda a: pl.BlockSpec(a.shape, lambda i: (0, 0))
    widths = [512, 512, 512, 256, 256, 256, 256, 256, 256]
    out_shape = [jax.ShapeDtypeStruct((S, n), BF16) for n in widths]
    out_shape.append(jax.ShapeDtypeStruct((S, LANES), F32))
    return pl.pallas_call(
        _proj_kernel,
        grid=(S // tm,),
        in_specs=[row(D), full(w_att), full(w_forget), row(LANES), row(LANES), full(fbias)],
        out_specs=[row(n) for n in widths] + [row(LANES)],
        out_shape=out_shape,
        compiler_params=_cparams(("parallel",)),
        name="proj",
    )(hb, w_att, w_forget, cos, sin, fbias)


def _fcum_kernel(lf_ref, f_ref):
    nh, R, _ = lf_ref.shape
    r0 = lax.broadcasted_iota(jnp.int32, (LANES, LANES), 0)
    c0 = lax.broadcasted_iota(jnp.int32, (LANES, LANES), 1)
    upper = jnp.where(r0 <= c0, 1.0, 0.0).astype(BF16)
    ones = jnp.ones((LANES, LANES), BF16)
    r1 = lax.broadcasted_iota(jnp.int32, (R, R), 0)
    c1 = lax.broadcasted_iota(jnp.int32, (R, R), 1)
    below = jnp.where(c1 < r1, 1.0, 0.0).astype(BF16)
    for h in range(nh):
        parts = _split3(lf_ref[h])
        within = sum(_dot(p, upper) for p in parts)
        totals = sum(_dot(p, ones) for p in parts)
        offs = sum(_dot(below, t) for t in _split3(totals))
        f_ref[h] = within + offs


def _fcum_call(lf):
    return pl.pallas_call(
        _fcum_kernel,
        out_shape=jax.ShapeDtypeStruct(lf.shape, F32),
        name="fcum",
    )(lf)


def _tile_pos(qi, j, tq, tk):
    row = qi * tq + lax.broadcasted_iota(jnp.int32, (tq, tk), 0)
    col = j * tk + lax.broadcasted_iota(jnp.int32, (tq, tk), 1)
    return row, col


def _split_q(q):
    lane = lax.broadcasted_iota(jnp.int32, (1, LANES), 1)
    lo = lane < HEAD_DIM
    zero = jnp.zeros_like(q)
    return jnp.where(lo, q, zero), jnp.where(lo, zero, q)


STRIP = 16
TRI = 256


def _chunk(ref, r0, c):
    return ref[r0:r0 + STRIP, c * LANES:(c + 1) * LANES]


def _softmax_tile(s_ref, p_ref, m_ref, l_ref, al_ref, chunk_state):
    tq, tk = s_ref.shape
    for r0 in range(0, tq, STRIP):
        states = [chunk_state(r0, c) for c in range(tk // LANES)]
        sc = []
        for c, st in enumerate(states):
            if st is False:
                sc.append(None)
            elif st is None:
                sc.append(_chunk(s_ref, r0, c))
            else:
                sc.append(jnp.where(st, _chunk(s_ref, r0, c), NEG))
        live = [s for s in sc if s is not None]
        m_prev = m_ref[r0:r0 + STRIP, :]
        if live:
            mx = functools.reduce(jnp.maximum, live)
            m_new = jnp.maximum(m_prev, jnp.max(mx, axis=1, keepdims=True))
        else:
            m_new = m_prev
        alpha = jnp.exp2(m_prev - m_new)
        psum = jnp.zeros((STRIP, LANES), F32)
        for c, s in enumerate(sc):
            if s is None:
                p_ref[r0:r0 + STRIP, c * LANES:(c + 1) * LANES] = jnp.zeros((STRIP, LANES), BF16)
            else:
                p = jnp.exp2(s - m_new)
                psum = psum + p
                p_ref[r0:r0 + STRIP, c * LANES:(c + 1) * LANES] = p.astype(BF16)
        m_ref[r0:r0 + STRIP, :] = m_new
        al_ref[r0:r0 + STRIP, :] = alpha
        l_ref[r0:r0 + STRIP, :] = alpha * l_ref[r0:r0 + STRIP, :] + psum


def _all_visible(r0, c):
    return None


def _init_softmax_state(m_refs, l_refs, acc_refs):
    for m_ref, l_ref, acc_ref in zip(m_refs, l_refs, acc_refs):
        m_ref[...] = jnp.full(m_ref.shape, NEG, F32)
        l_ref[...] = jnp.zeros(l_ref.shape, F32)
        acc_ref[...] = jnp.zeros(acc_ref.shape, F32)


def _softmax_scratch(tq, tk):
    per_stream = [pltpu.VMEM((tq, tk), F32), pltpu.VMEM((tq, tk), BF16), pltpu.VMEM((tq, LANES), F32),
                  pltpu.VMEM((tq, LANES), F32), pltpu.VMEM((tq, LANES), F32), pltpu.VMEM((tq, LANES), F32)]
    return per_stream + per_stream


def _row_total(l_ref):
    return jnp.sum(l_ref[...], axis=1, keepdims=True)


def _sweep(qi, tq, scores, v_ref, scratch, diag_chunk_state):
    s_refs, p_refs, m_refs, l_refs, al_refs, acc_refs = (scratch[i::6] for i in range(6))

    def softmax(chunk_state):
        for x in range(2):
            _softmax_tile(s_refs[x], p_refs[x], m_refs[x], l_refs[x], al_refs[x], chunk_state)

    def accumulate(j):
        vb = v_ref[pl.ds(pl.multiple_of(j * tq, tq), tq), :]
        for x in range(2):
            acc_refs[x][...] = al_refs[x][...] * acc_refs[x][...] + _dot(p_refs[x][...], vb)

    scores(0)

    def body(j, c):
        softmax(_all_visible)
        scores(j + 1)
        accumulate(j)
        return c

    lax.fori_loop(0, qi, body, 0)
    softmax(diag_chunk_state)
    accumulate(qi)


def _diff_chunk_state(r0, c):
    q_chunk = r0 // CHUNK
    k_lo, k_hi = (c * LANES) // CHUNK, (c * LANES + LANES - 1) // CHUNK
    if k_hi <= q_chunk:
        return None
    if k_lo > q_chunk:
        return False
    return lax.broadcasted_iota(jnp.int32, (STRIP, LANES), 1) < CHUNK


def _diff_kernel(tq, lam_init, q_ref, k_ref, v_ref, lamv_ref, g_ref, o_ref, *scratch):
    s_refs, p_refs, m_refs, l_refs, al_refs, acc_refs = (scratch[i::6] for i in range(6))
    qi = pl.program_id(1)
    qs = _split_q(q_ref[...])
    _init_softmax_state(m_refs, l_refs, acc_refs)

    def scores(j):
        kb = k_ref[pl.ds(pl.multiple_of(j * tq, tq), tq), :]
        for x in range(2):
            s_refs[x][...] = _dot_nt(qs[x], kb)

    _sweep(qi, tq, scores, v_ref, scratch, _diff_chunk_state)

    lv = lamv_ref[...]
    lam = (jnp.exp(jnp.sum(lv[0:1] * lv[1:2], axis=1, keepdims=True))
           - jnp.exp(jnp.sum(lv[2:3] * lv[3:4], axis=1, keepdims=True)) + lam_init)
    o = (acc_refs[0][...] / _row_total(l_refs[0])
         - lam * (acc_refs[1][...] / _row_total(l_refs[1])))
    o = o * lax.rsqrt(jnp.mean(o * o, axis=-1, keepdims=True) + 1e-6) * g_ref[...]
    o_ref[...] = (o * (1.0 - lam_init)).astype(BF16)


def _diff_call(dq, dk, dv, lamv, gnorm, lam_init, tq, tk):
    S = dq.shape[0]
    nh = dq.shape[1] // LANES
    qspec = pl.BlockSpec((tq, LANES), lambda h, i: (i, h))
    kvspec = pl.BlockSpec((S, LANES), lambda h, i: (0, h))
    full = lambda a: pl.BlockSpec(a.shape, lambda h, i: (0, 0))
    return pl.pallas_call(
        functools.partial(_diff_kernel, tq, lam_init),
        grid=(nh, S // tq),
        in_specs=[qspec, kvspec, kvspec, full(lamv), full(gnorm)],
        out_specs=qspec,
        out_shape=jax.ShapeDtypeStruct(dq.shape, BF16),
        scratch_shapes=_softmax_scratch(tq, tq),
        compiler_params=_cparams(("parallel", "arbitrary")),
        name="diff_attn",
    )(dq, dk, dv, lamv, gnorm)


def _causal_chunk_state(strict, col0=0):
    def state(r0, c):
        k_lo, k_hi = col0 + c * LANES, col0 + c * LANES + LANES - 1
        q_lo, q_hi = r0, r0 + STRIP - 1
        if k_hi < q_lo or (not strict and k_hi <= q_lo):
            return None
        if k_lo > q_hi or (strict and k_lo >= q_hi):
            return False
        row = r0 + lax.broadcasted_iota(jnp.int32, (STRIP, LANES), 0)
        col = k_lo + lax.broadcasted_iota(jnp.int32, (STRIP, LANES), 1)
        return col < row if strict else col <= row
    return state


def _fox_kernel(tq, q_ref, k_ref, v_ref, f_ref, o_ref, *scratch):
    s_refs, p_refs, m_refs, l_refs, al_refs, acc_refs = (scratch[i::6] for i in range(6))
    qi = pl.program_id(1)
    qs = _split_q(q_ref[...])
    _init_softmax_state(m_refs, l_refs, acc_refs)
    f0 = [f_ref[x, qi][:, 0:1] for x in range(2)]

    def scores(j):
        kb = k_ref[pl.ds(pl.multiple_of(j * tq, tq), tq), :]
        for x in range(2):
            s_refs[x][...] = _dot_nt(qs[x], kb) + (f0[x] - f_ref[x, j]) * LOG2E

    _sweep(qi, tq, scores, v_ref, scratch, _causal_chunk_state(strict=False))

    lane = lax.broadcasted_iota(jnp.int32, (1, LANES), 1)
    o = jnp.where(lane < HEAD_DIM, acc_refs[0][...] / _row_total(l_refs[0]),
                  acc_refs[1][...] / _row_total(l_refs[1]))
    o_ref[...] = o.astype(BF16)


def _fox_call(fq, fk, fv, fcum, tq, tk):
    S = fq.shape[0]
    npair = fq.shape[1] // LANES
    qspec = pl.BlockSpec((tq, LANES), lambda p, i: (i, p))
    kvspec = pl.BlockSpec((S, LANES), lambda p, i: (0, p))
    fspec = pl.BlockSpec((2, S // tk, 1, tk), lambda p, i: (p, 0, 0, 0))
    assert tk == tq
    return pl.pallas_call(
        functools.partial(_fox_kernel, tq),
        grid=(npair, S // tq),
        in_specs=[qspec, kvspec, kvspec, fspec],
        out_specs=qspec,
        out_shape=jax.ShapeDtypeStruct(fq.shape, BF16),
        scratch_shapes=_softmax_scratch(tq, tq),
        compiler_params=_cparams(("parallel", "arbitrary")),
        name="fox_attn",
    )(fq, fk, fv, fcum)


def _sb_kernel(tq, tk, q_ref, k_ref, v_ref, tri_ref, o_ref, *scratch):
    z_refs, u_refs, lb_refs, a_refs, tot_refs, carry_refs, acc_refs = (scratch[i::7] for i in range(7))
    qi = pl.program_id(1)
    qs = _split_q(q_ref[...])
    for x in range(2):
        carry_refs[x][...] = jnp.zeros((tq, LANES), F32)
        acc_refs[x][...] = jnp.zeros((tq, LANES), F32)
    assert tk == tq and tk == 2 * TRI
    nc = tk // LANES

    def scores(j):
        kb = k_ref[pl.ds(pl.multiple_of(j * tk, tk), tk), :]
        for x in range(2):
            z_refs[x][...] = _dot_nt(qs[x], kb)

    def gates(chunk_state):
        for x in range(2):
            z_ref, u_ref, lb_ref = z_refs[x], u_refs[x], lb_refs[x]
            for r0 in range(0, tq, STRIP):
                usum = jnp.zeros((STRIP, LANES), F32)
                for c in range(nc):
                    st = chunk_state(r0, c)
                    if st is False:
                        u_ref[r0:r0 + STRIP, c * LANES:(c + 1) * LANES] = jnp.zeros((STRIP, LANES), BF16)
                        lb_ref[r0:r0 + STRIP, c * LANES:(c + 1) * LANES] = jnp.zeros((STRIP, LANES), F32)
                        continue
                    z = _chunk(z_ref, r0, c)
                    sp = jnp.maximum(z, 0.0) + LOG2E * jnp.log(1.0 + jnp.exp2(-jnp.abs(z)))
                    u = sp if st is None else jnp.where(st, sp, 0.0)
                    u_ref[r0:r0 + STRIP, c * LANES:(c + 1) * LANES] = u.astype(BF16)
                    lb_ref[r0:r0 + STRIP, c * LANES:(c + 1) * LANES] = z - sp
                    usum = usum + u
                tot_refs[x][r0:r0 + STRIP, :] = (jnp.sum(usum, axis=1, keepdims=True)
                                                 + jnp.zeros((STRIP, LANES), F32))

    def weights_and_accumulate(j, masked):
        vb = v_ref[pl.ds(pl.multiple_of(j * tk, tk), tk), :]
        tri_stack = tri_ref[...]
        for x in range(2):
            u_ref, lb_ref, a_ref = u_refs[x], lb_refs[x], a_refs[x]
            suffix = (_dot(u_ref[...], tri_stack), _dot(u_ref[:, TRI:], tri_stack[:TRI]))
            carry = carry_refs[x][...]
            for c in range(nc):
                cols = slice(c * LANES, (c + 1) * LANES)
                blk, off = divmod(c * LANES, TRI)
                suf = suffix[blk][:, off:off + LANES]
                a = jnp.exp2(lb_ref[:, cols] - suf - carry)
                if masked:
                    row = lax.broadcasted_iota(jnp.int32, (tq, LANES), 0)
                    col = c * LANES + lax.broadcasted_iota(jnp.int32, (tq, LANES), 1)
                    a = jnp.where(col < row, a, 0.0)
                a_ref[:, cols] = a.astype(BF16)
            acc_refs[x][...] += _dot(a_ref[...], vb)
            carry_refs[x][...] = carry + tot_refs[x][...]

    scores(qi)
    gates(_causal_chunk_state(True))
    scores(jnp.maximum(qi - 1, 0))
    weights_and_accumulate(qi, True)

    def body(i, c):
        j = qi - 1 - i
        gates(_all_visible)
        scores(jnp.maximum(j - 1, 0))
        weights_and_accumulate(j, False)
        return c

    lax.fori_loop(0, qi, body, 0)

    lane = lax.broadcasted_iota(jnp.int32, (1, LANES), 1)
    o_ref[...] = jnp.where(lane < HEAD_DIM, acc_refs[0][...], acc_refs[1][...]).astype(BF16)


def _sb_call(sq, sk, sv, tq, tk):
    S = sq.shape[0]
    npair = sq.shape[1] // LANES
    qspec = pl.BlockSpec((tq, LANES), lambda p, i: (i, p))
    kvspec = pl.BlockSpec((S, LANES), lambda p, i: (0, p))
    tri = jnp.asarray(np.concatenate([np.tril(np.ones((TRI, TRI), np.float32), -1),
                                      np.ones((TRI, TRI), np.float32)]), BF16)
    return pl.pallas_call(
        functools.partial(_sb_kernel, tq, tk),
        grid=(npair, S // tq),
        in_specs=[qspec, kvspec, kvspec, pl.BlockSpec((2 * TRI, TRI), lambda p, i: (0, 0))],
        out_specs=qspec,
        out_shape=jax.ShapeDtypeStruct(sq.shape, BF16),
        scratch_shapes=2 * [pltpu.VMEM((tq, tk), F32), pltpu.VMEM((tq, tk), BF16),
                            pltpu.VMEM((tq, tk), F32), pltpu.VMEM((tq, tk), BF16),
                            pltpu.VMEM((tq, LANES), F32), pltpu.VMEM((tq, LANES), F32),
                            pltpu.VMEM((tq, LANES), F32)],
        compiler_params=_cparams(("parallel", "arbitrary")),
        name="sb_attn",
    )(sq, sk, sv, tri)


def _merge_kernel(hb_ref, h_ref, od_ref, os_ref, of_ref, wg_ref, wud_ref, wus_ref, wuf_ref, wo_ref,
                  g_ref, b_ref, out_ref, outb_ref):
    x = hb_ref[...]
    D = x.shape[1]
    merged = None
    for c, (o_ref, wu_ref) in enumerate(((od_ref, wud_ref), (os_ref, wus_ref), (of_ref, wuf_ref))):
        gate = _sigmoid(_dot(x, wg_ref[:, c * D:(c + 1) * D]))
        term = gate * _dot(o_ref[...], wu_ref[...])
        merged = term if merged is None else merged + term
    y = _dot(merged.astype(BF16), wo_ref[...])
    hn = _layer_norm(ALPHA * h_ref[...] + y, g_ref[...], b_ref[...])
    out_ref[...] = hn
    outb_ref[...] = hn.astype(BF16)


def _merge_call(hb, h, od, osb, of, wg, wud, wus, wuf, wo, g, b, tm):
    S, D = h.shape
    row = lambda a: pl.BlockSpec((tm, a.shape[1]), lambda i: (i, 0))
    full = lambda a: pl.BlockSpec(a.shape, lambda i: (0, 0))
    args = (hb, h, od, osb, of, wg, wud, wus, wuf, wo, g, b)
    in_specs = [row(a) for a in args[:5]] + [full(a) for a in args[5:]]
    return pl.pallas_call(
        _merge_kernel,
        grid=(S // tm,),
        in_specs=in_specs,
        out_specs=[row(h), row(h)],
        out_shape=[jax.ShapeDtypeStruct((S, D), F32), jax.ShapeDtypeStruct((S, D), BF16)],
        compiler_params=_cparams(("parallel",)),
        name="merge",
    )(*args)


def _swiglu_hidden(x, wg, wu):
    gate = _dot(x, wg)
    return (gate * _sigmoid(gate)) * _dot(x, wu)


def _ffn_kernel(hb_ref, h_ref, wg_ref, wu_ref, wd_ref, g_ref, b_ref, out_ref, outb_ref, acc_ref):
    f = pl.program_id(1)

    @pl.when(f == 0)
    def _():
        acc_ref[...] = jnp.zeros(acc_ref.shape, F32)

    hid = _swiglu_hidden(hb_ref[...], wg_ref[...], wu_ref[...])
    acc_ref[...] += _dot(hid.astype(BF16), wd_ref[...])

    @pl.when(f == pl.num_programs(1) - 1)
    def _():
        hn = _layer_norm(ALPHA * h_ref[...] + acc_ref[...], g_ref[...], b_ref[...])
        out_ref[...] = hn
        outb_ref[...] = hn.astype(BF16)


def _ffn_call(hb, h, wg, wu, wd, g, b, tm, tf):
    S, D = h.shape
    F = wg.shape[1]
    row = pl.BlockSpec((tm, D), lambda i, f: (i, 0))
    vec = pl.BlockSpec((1, D), lambda i, f: (0, 0))
    return pl.pallas_call(
        _ffn_kernel,
        grid=(S // tm, F // tf),
        in_specs=[row, row,
                  pl.BlockSpec((D, tf), lambda i, f: (0, f)),
                  pl.BlockSpec((D, tf), lambda i, f: (0, f)),
                  pl.BlockSpec((tf, D), lambda i, f: (f, 0)),
                  vec, vec],
        out_specs=[row, row],
        out_shape=[jax.ShapeDtypeStruct((S, D), F32), jax.ShapeDtypeStruct((S, D), BF16)],
        scratch_shapes=[pltpu.VMEM((tm, D), F32)],
        compiler_params=_cparams(("parallel", "arbitrary")),
        name="ffn",
    )(hb, h, wg, wu, wd, g, b)


MOE_TILE = 512
DMA_UNROLL = 8


def _route_kernel(h_ref, w_ref, b_ref, sel_ref, i1_ref, i2_ref, w1_ref, w2_ref):
    logits = jnp.dot(h_ref[...], w_ref[...], precision=lax.Precision.HIGHEST,
                     preferred_element_type=F32) + b_ref[...]
    lane = lax.broadcasted_iota(jnp.int32, logits.shape, 1)
    logits = jnp.where(lane < N_EXPERTS, logits, NEG)
    v1 = jnp.max(logits, axis=1, keepdims=True)
    i1 = jnp.min(jnp.where(logits == v1, lane, LANES), axis=1, keepdims=True)
    rest = jnp.where(lane == i1, NEG, logits)
    v2 = jnp.max(rest, axis=1, keepdims=True)
    i2 = jnp.min(jnp.where(rest == v2, lane, LANES), axis=1, keepdims=True)
    e2 = jnp.exp(v2 - v1)
    zero = jnp.zeros(logits.shape, F32)
    sel_ref[...] = jnp.where((lane == i1) | (lane == i2), 1.0, 0.0).astype(BF16)
    i1_ref[...] = i1 + jnp.zeros(logits.shape, jnp.int32)
    i2_ref[...] = i2 + jnp.zeros(logits.shape, jnp.int32)
    w1_ref[...] = 1.0 / (1.0 + e2) + zero
    w2_ref[...] = e2 / (1.0 + e2) + zero


def _route_call(h, w_router, b_router, tm):
    S, D = h.shape
    wpad = jnp.zeros((D, LANES), F32).at[:, :N_EXPERTS].set(w_router)
    bpad = jnp.zeros((1, LANES), F32).at[0, :N_EXPERTS].set(b_router)
    row = pl.BlockSpec((tm, LANES), lambda i: (i, 0))
    return pl.pallas_call(
        _route_kernel,
        grid=(S // tm,),
        in_specs=[pl.BlockSpec((tm, D), lambda i: (i, 0)),
                  pl.BlockSpec((D, LANES), lambda i: (0, 0)),
                  pl.BlockSpec((1, LANES), lambda i: (0, 0))],
        out_specs=[row] * 5,
        out_shape=[jax.ShapeDtypeStruct((S, LANES), BF16),
                   jax.ShapeDtypeStruct((S, LANES), jnp.int32),
                   jax.ShapeDtypeStruct((S, LANES), jnp.int32),
                   jax.ShapeDtypeStruct((S, LANES), F32),
                   jax.ShapeDtypeStruct((S, LANES), F32)],
        compiler_params=_cparams(("parallel",)),
        name="route",
    )(h, wpad, bpad)


def _rank_kernel(sel_ref, i1_ref, i2_ref, tril_ref, rank_ref, cnt_ref):
    @pl.when(pl.program_id(0) == 0)
    def _():
        cnt_ref[...] = jnp.zeros(cnt_ref.shape, F32)

    sel = sel_ref[...]
    rank = _dot(tril_ref[...], sel) + cnt_ref[...]
    lane = lax.broadcasted_iota(jnp.int32, rank.shape, 1)
    r1 = jnp.sum(jnp.where(lane == i1_ref[...], rank, 0.0), axis=1, keepdims=True)
    r2 = jnp.sum(jnp.where(lane == i2_ref[...], rank, 0.0), axis=1, keepdims=True)
    rank_ref[...] = jnp.where(lane == 0, r1, jnp.where(lane == 1, r2, 0.0))
    cnt_ref[...] += jnp.sum(sel.astype(F32), axis=0, keepdims=True)


def _rank_call(sel, i1b, i2b, tb):
    S = sel.shape[0]
    tril = jnp.asarray(np.tril(np.ones((tb, tb), np.float32), -1), BF16)
    row = pl.BlockSpec((tb, LANES), lambda i: (i, 0))
    return pl.pallas_call(
        _rank_kernel,
        grid=(S // tb,),
        in_specs=[row, row, row, pl.BlockSpec((tb, tb), lambda i: (0, 0))],
        out_specs=[row, pl.BlockSpec((1, LANES), lambda i: (0, 0))],
        out_shape=[jax.ShapeDtypeStruct((S, LANES), F32), jax.ShapeDtypeStruct((1, LANES), F32)],
        compiler_params=_cparams(("arbitrary",)),
        name="rank",
    )(sel, i1b, i2b, tril)


def _row_copy(src_ref, s, dst_ref, d, sem):
    return pltpu.make_async_copy(src_ref.at[pl.ds(s, 1)], dst_ref.at[pl.ds(d, 1)], sem)


def _dispatch_kernel(tb, n_tok, dest_ref, h_ref, xs_in_hbm, xs_hbm, sem):
    del xs_in_hbm
    base = pl.program_id(0) * tb

    def issue(t, c):
        _row_copy(h_ref, t, xs_hbm, dest_ref[base + t], sem).start(priority=0)
        _row_copy(h_ref, t, xs_hbm, dest_ref[n_tok + base + t], sem).start(priority=1)
        return c

    lax.fori_loop(0, tb, issue, 0, unroll=DMA_UNROLL)

    def drain(t, c):
        _row_copy(h_ref, 0, xs_hbm, 0, sem).wait()
        _row_copy(h_ref, 0, xs_hbm, 0, sem).wait()
        return c

    lax.fori_loop(0, tb, drain, 0, unroll=DMA_UNROLL)


def _dispatch_call(dest, h, n_rows, tb):
    S, D = h.shape
    xs0 = jnp.zeros((n_rows, D), F32)
    return pl.pallas_call(
        functools.partial(_dispatch_kernel, tb, S),
        grid_spec=pltpu.PrefetchScalarGridSpec(
            num_scalar_prefetch=1,
            grid=(S // tb,),
            in_specs=[pl.BlockSpec((tb, D), lambda i, d: (i, 0)), pl.BlockSpec(memory_space=pl.ANY)],
            out_specs=pl.BlockSpec(memory_space=pl.ANY),
            scratch_shapes=[pltpu.SemaphoreType.DMA],
        ),
        out_shape=jax.ShapeDtypeStruct((n_rows, D), F32),
        input_output_aliases={2: 0},
        compiler_params=_cparams(("arbitrary",)),
        name="dispatch",
    )(dest, h, xs0)


def _experts_kernel(te_ref, nu_ref, xs_ref, wg_ref, wu_ref, wd_ref, ys_ref, xb_ref, acc_ref):
    i = pl.program_id(0)
    f = pl.program_id(1)
    last_f = pl.num_programs(1) - 1
    used = i < nu_ref[0]

    @pl.when(used & (f == 0))
    def _():
        xb_ref[...] = xs_ref[...].astype(BF16)

    @pl.when(used)
    def _():
        hid = _swiglu_hidden(xb_ref[...], wg_ref[0], wu_ref[0])
        part = _dot(hid.astype(BF16), wd_ref[0])

        @pl.when(f == 0)
        def _():
            acc_ref[...] = part

        @pl.when(f > 0)
        def _():
            acc_ref[...] += part

    @pl.when(used & (f == last_f))
    def _():
        ys_ref[...] = acc_ref[...]

    @pl.when(jnp.logical_not(used) & (f == last_f))
    def _():
        ys_ref[...] = jnp.zeros(ys_ref.shape, F32)


def _experts_call(tile_expert, n_used, xs, wg, wu, wd, tmx, tf):
    P, D = xs.shape
    F = wg.shape[2]
    nf = F // tf

    def fsel(i, f, nu):
        return jnp.where(i < nu[0], f, nf - 1)

    return pl.pallas_call(
        _experts_kernel,
        grid_spec=pltpu.PrefetchScalarGridSpec(
            num_scalar_prefetch=2,
            grid=(P // tmx, nf),
            in_specs=[pl.BlockSpec((tmx, D), lambda i, f, te, nu: (i, 0)),
                      pl.BlockSpec((1, D, tf), lambda i, f, te, nu: (te[i], 0, fsel(i, f, nu))),
                      pl.BlockSpec((1, D, tf), lambda i, f, te, nu: (te[i], 0, fsel(i, f, nu))),
                      pl.BlockSpec((1, tf, D), lambda i, f, te, nu: (te[i], fsel(i, f, nu), 0))],
            out_specs=pl.BlockSpec((tmx, D), lambda i, f, te, nu: (i, 0)),
            scratch_shapes=[pltpu.VMEM((tmx, D), BF16), pltpu.VMEM((tmx, D), F32)],
        ),
        out_shape=jax.ShapeDtypeStruct((P, D), F32),
        compiler_params=_cparams(("arbitrary", "arbitrary")),
        name="experts",
    )(tile_expert, n_used, xs, wg, wu, wd)


def _combine_kernel(tm, n_tok, dest_ref, ys_hbm, h_ref, w1_ref, w2_ref, g_ref, b_ref,
                    out_ref, outb_ref, buf_ref, sem):
    i = pl.program_id(0)
    slot = i % 2

    def gather(tile, slot_, start):
        base = tile * tm

        def one(t, c):
            for k in range(2):
                src_row = dest_ref[k * n_tok + base + t] if start else 0
                cp = pltpu.make_async_copy(ys_hbm.at[pl.ds(src_row, 1)],
                                           buf_ref.at[slot_, k, pl.ds(t, 1)], sem.at[slot_])
                if start:
                    cp.start(priority=k)
                else:
                    cp.wait()
            return c

        lax.fori_loop(0, tm, one, 0, unroll=DMA_UNROLL)

    @pl.when(i == 0)
    def _():
        gather(0, 0, True)

    @pl.when(i + 1 < pl.num_programs(0))
    def _():
        gather(i + 1, 1 - slot, True)

    gather(i, slot, False)

    w1 = w1_ref[...]
    w2 = w2_ref[...]
    y = jnp.concatenate(
        [w1 * buf_ref[slot, 0, :, c * LANES:(c + 1) * LANES]
         + w2 * buf_ref[slot, 1, :, c * LANES:(c + 1) * LANES]
         for c in range(h_ref.shape[1] // LANES)], axis=1)
    hn = _layer_norm(ALPHA * h_ref[...] + y, g_ref[...], b_ref[...])
    out_ref[...] = hn
    outb_ref[...] = hn.astype(BF16)


def _combine_call(dest, ys, h, w1b, w2b, g, b, tm):
    S, D = h.shape
    row = pl.BlockSpec((tm, D), lambda i, d: (i, 0))
    lrow = pl.BlockSpec((tm, LANES), lambda i, d: (i, 0))
    vec = pl.BlockSpec((1, D), lambda i, d: (0, 0))
    return pl.pallas_call(
        functools.partial(_combine_kernel, tm, S),
        grid_spec=pltpu.PrefetchScalarGridSpec(
            num_scalar_prefetch=1,
            grid=(S // tm,),
            in_specs=[pl.BlockSpec(memory_space=pl.ANY), row, lrow, lrow, vec, vec],
            out_specs=[row, row],
            scratch_shapes=[pltpu.VMEM((2, 2, tm, D), F32), pltpu.SemaphoreType.DMA((2,))],
        ),
        out_shape=[jax.ShapeDtypeStruct((S, D), F32), jax.ShapeDtypeStruct((S, D), BF16)],
        compiler_params=_cparams(("arbitrary",)),
        name="combine",
    )(dest, ys, h, w1b, w2b, g, b)


def _moe_call(h, w_router, b_router, wg, wu, wd, g, b, tm, tf):
    S, D = h.shape
    tmx = MOE_TILE
    sel, i1b, i2b, w1b, w2b = _route_call(h, w_router, b_router, tm)
    ranks, counts = _rank_call(sel, i1b, i2b, min(256, S))
    cnt = counts[0, :N_EXPERTS].astype(jnp.int32)
    padded = ((cnt + tmx - 1) // tmx) * tmx
    ends = jnp.cumsum(padded)
    off = ends - padded
    dest = jnp.concatenate([off[i1b[:, 0]] + ranks[:, 0].astype(jnp.int32),
                            off[i2b[:, 0]] + ranks[:, 1].astype(jnp.int32)])
    n_tiles = (2 * S) // tmx + N_EXPERTS
    tile_start = jnp.arange(n_tiles, dtype=jnp.int32) * tmx
    tile_expert = jnp.minimum(jnp.sum(tile_start[:, None] >= ends[None, :], axis=1),
                              N_EXPERTS - 1).astype(jnp.int32)
    n_used = (ends[-1] // tmx).astype(jnp.int32).reshape(1)
    xs = _dispatch_call(dest, h, n_tiles * tmx, tm)
    ys = _experts_call(tile_expert, n_used, xs, wg, wu, wd, tmx, tf)
    return _combine_call(dest, ys, h, w1b, w2b, g, b, min(256, S))


def _rope_tables(S):
    half = HEAD_DIM // 2
    pos = jnp.arange(S, dtype=F32)
    inv = ROPE_THETA ** (-jnp.arange(half, dtype=F32) / half)
    ang = pos[:, None] * inv[None, :]
    cos, sin = jnp.cos(ang), jnp.sin(ang)
    reps = LANES // HEAD_DIM
    cos_t = jnp.tile(jnp.concatenate([cos, cos], axis=1), (1, reps))
    sin_t = jnp.tile(jnp.concatenate([-sin, sin], axis=1), (1, reps))
    return cos_t, sin_t


def _pick(S, pref):
    t = min(pref, S)
    assert S % t == 0
    return t


def _forward(x, ln_in_g, ln_in_b, w_in, b_forget, lam_q1, lam_k1, lam_q2, lam_k2,
             diff_norm_g, w_up_diff, w_up_sb, w_up_fox, w_out, ln_mix_g, ln_mix_b,
             ln_ffn_g, ln_ffn_b, w_gate_dense, w_up_dense, w_down_dense, w_router,
             b_router, w_gate_moe, w_up_moe, w_down_moe):
    B, S, D = x.shape
    assert B == 1 and D == D_MODEL and S % LANES == 0
    depth = w_in.shape[0]
    tm = _pick(S, 512)
    tq = _pick(S, 512)
    tk = _pick(S, 256)
    tm_ffn = _pick(S, 1024)
    tf = 512
    vec = lambda a: a.reshape(1, -1).astype(F32)

    cos_t, sin_t = _rope_tables(S)
    h, hb = _ln_call(x.reshape(S, D), ln_in_g, ln_in_b, tm)
    w_atts, w_forgets, w_gate_logits = _wsplit_call(w_in, 128)
    for l in range(depth):
        w_att, w_forget, w_gate_logit = w_atts[l], w_forgets[l], w_gate_logits[l]
        fbias = jnp.pad(b_forget[l].astype(F32), (0, LANES - N_HEADS_FOX)).reshape(1, LANES)
        dq, dk, dv, sq, sk, sv, fq, fk, fv, logf = _proj_call(hb, w_att, w_forget, cos_t, sin_t,
                                                              fbias, tm)

        lf = logf[:, :N_HEADS_FOX].T.reshape(N_HEADS_FOX, S // LANES, LANES)
        fcum = _fcum_call(lf).reshape(N_HEADS_FOX, S // tq, 1, tq)

        lam_init = 0.8 - 0.6 * math.exp(-0.3 * l)
        lamv = jnp.stack([lam_q1[l], lam_k1[l], lam_q2[l], lam_k2[l]]).astype(F32)
        o_diff = _diff_call(dq, dk, dv, lamv, vec(diff_norm_g[l]), lam_init, tq, tk)
        o_sb = _sb_call(sq, sk, sv, tq, tq)
        o_fox = _fox_call(fq, fk, fv, fcum, tq, tq)

        h, hb = _merge_call(hb, h, o_diff, o_sb, o_fox, w_gate_logit,
                            w_up_diff[l].astype(BF16), w_up_sb[l].astype(BF16),
                            w_up_fox[l].astype(BF16), w_out[l].astype(BF16),
                            vec(ln_mix_g[l]), vec(ln_mix_b[l]), tm)
        j = l // 2
        if l % 2 == 0:
            h, hb = _ffn_call(hb, h, w_gate_dense[j].astype(BF16), w_up_dense[j].astype(BF16),
                              w_down_dense[j].astype(BF16), vec(ln_ffn_g[l]), vec(ln_ffn_b[l]),
                              tm_ffn, tf)
        else:
            h, hb = _moe_call(h, w_router[j], b_router[j], w_gate_moe[j].astype(BF16),
                              w_up_moe[j].astype(BF16), w_down_moe[j].astype(BF16),
                              vec(ln_ffn_g[l]), vec(ln_ffn_b[l]), tm, tf)
    return h.reshape(B, S, D)


def kernel(x, ln_in_g, ln_in_b, w_in, b_forget, lam_q1, lam_k1, lam_q2, lam_k2, diff_norm_g,
           w_up_diff, w_up_sb, w_up_fox, w_out, ln_mix_g, ln_mix_b, ln_ffn_g, ln_ffn_b,
           w_gate_dense, w_up_dense, w_down_dense, w_router, b_router, w_gate_moe, w_up_moe,
           w_down_moe):
    return _forward(x, ln_in_g, ln_in_b, w_in, b_forget, lam_q1, lam_k1, lam_q2, lam_k2,
                    diff_norm_g, w_up_diff, w_up_sb, w_up_fox, w_out, ln_mix_g, ln_mix_b,
                    ln_ffn_g, ln_ffn_b, w_gate_dense, w_up_dense, w_down_dense, w_router,
                    b_router, w_gate_moe, w_up_moe, w_down_moe)
```

```python
import functools
import math

import jax
import jax.numpy as jnp
import numpy as np
from jax import lax
from jax.experimental import pallas as pl
from jax.experimental.pallas import tpu as pltpu

D_MODEL = 1024
DEPTH = 2
CHUNK = 64
HEAD_DIM = 64
N_HEADS_DIFF = 4
N_HEADS_SB = 4
N_HEADS_FOX = 4
WIDTH_DIFF = N_HEADS_DIFF * 2 * HEAD_DIM
WIDTH_SB = N_HEADS_SB * HEAD_DIM
WIDTH_FOX = N_HEADS_FOX * HEAD_DIM
ROPE_THETA = 10000.0
N_EXPERTS = 8
D_FF = 3584
ALPHA = (2 * DEPTH) ** 0.25
Q_SCALE = HEAD_DIM ** -0.5
LOG2E = 1.4426950408889634
N_ATT = 3 * WIDTH_DIFF + 3 * WIDTH_SB + 3 * WIDTH_FOX
LANES = 128
NEG = -1e30

F32 = jnp.float32
BF16 = jnp.bfloat16

VMEM_LIMIT = 56 * 1024 * 1024


SWEEP_FLAGS = None


def _cparams(sem, flags=None):
    return pltpu.CompilerParams(dimension_semantics=sem, vmem_limit_bytes=VMEM_LIMIT, flags=flags)


def _layer_norm(z, g, b, eps=1e-5):
    mu = jnp.mean(z, axis=-1, keepdims=True)
    zc = z - mu
    var = jnp.mean(zc * zc, axis=-1, keepdims=True)
    return zc * lax.rsqrt(var + eps) * g + b


def _sigmoid(x):
    return 1.0 / (1.0 + jnp.exp(-x))


def _softplus(x):
    return jnp.maximum(x, 0.0) + jnp.log1p(jnp.exp(-jnp.abs(x)))


def _dot(a, b):
    return jnp.dot(a, b, preferred_element_type=F32)


def _dot_nt(a, b):
    return lax.dot_general(a, b, (((1,), (1,)), ((), ())), preferred_element_type=F32)


def _split3(x):
    x1 = x.astype(BF16)
    r = x - x1.astype(F32)
    x2 = r.astype(BF16)
    x3 = (r - x2.astype(F32)).astype(BF16)
    return x1, x2, x3


def _ln_kernel(x_ref, g_ref, b_ref, h_ref, hb_ref):
    h = _layer_norm(x_ref[...], g_ref[...], b_ref[...])
    h_ref[...] = h
    hb_ref[...] = h.astype(BF16)


def _ln_call(x, g, b, tm):
    S, D = x.shape
    row = pl.BlockSpec((tm, D), lambda i: (i, 0))
    vec = pl.BlockSpec((1, D), lambda i: (0, 0))
    return pl.pallas_call(
        _ln_kernel,
        grid=(S // tm,),
        in_specs=[row, vec, vec],
        out_specs=[row, row],
        out_shape=[jax.ShapeDtypeStruct((S, D), F32), jax.ShapeDtypeStruct((S, D), BF16)],
        compiler_params=_cparams(("parallel",)),
        name="ln_in",
    )(x, g.reshape(1, D), b.reshape(1, D))


def _wsplit_kernel(w_ref, att_ref, fgt_ref, gate_ref):
    att_ref[0] = w_ref[0, :, :N_ATT].astype(BF16)
    lane = lax.broadcasted_iota(jnp.int32, (1, LANES), 1)
    fgt_ref[0] = jnp.where(lane < N_HEADS_FOX, w_ref[0, :, N_ATT:N_ATT + LANES], 0.0).astype(BF16)
    gate_ref[0] = w_ref[0, :, N_ATT + N_HEADS_FOX:].astype(BF16)


def _wsplit_call(w_in, tr):
    depth, D, n_in = w_in.shape
    n_gate = n_in - N_ATT - N_HEADS_FOX
    spec = lambda n: pl.BlockSpec((1, tr, n), lambda l, i: (l, i, 0))
    return pl.pallas_call(
        _wsplit_kernel,
        grid=(depth, D // tr),
        in_specs=[spec(n_in)],
        out_specs=[spec(N_ATT), spec(LANES), spec(n_gate)],
        out_shape=[jax.ShapeDtypeStruct((depth, D, N_ATT), BF16),
                   jax.ShapeDtypeStruct((depth, D, LANES), BF16),
                   jax.ShapeDtypeStruct((depth, D, n_gate), BF16)],
        compiler_params=_cparams(("parallel", "parallel")),
        name="wsplit",
    )(w_in)


def _proj_kernel(hb_ref, w_ref, wf_ref, cos_ref, sin_ref, fb_ref,
                 dq_ref, dk_ref, dv_ref, sq_ref, sk_ref, sv_ref, fq_ref, fk_ref, fv_ref, lf_ref):
    x = hb_ref[...]
    cos = cos_ref[...]
    sin = sin_ref[...]
    lane = lax.broadcasted_iota(jnp.int32, (1, LANES), 1)
    lower = (lane & (HEAD_DIM // 2)) == 0

    def mm(c0, c1):
        return _dot(x, w_ref[:, c0:c1])

    def rope_store(y, out_ref, scale):
        for g in range(y.shape[1] // LANES):
            yg = y[:, g * LANES:(g + 1) * LANES]
            partner = jnp.where(lower, pltpu.roll(yg, LANES - HEAD_DIM // 2, 1),
                                pltpu.roll(yg, HEAD_DIM // 2, 1))
            r = yg * cos + partner * sin
            if scale != 1.0:
                r = r * scale
            out_ref[:, g * LANES:(g + 1) * LANES] = r.astype(BF16)

    rope_store(mm(0, 512), dq_ref, Q_SCALE * LOG2E)
    rope_store(mm(512, 1024), dk_ref, 1.0)
    dv_ref[...] = mm(1024, 1536).astype(BF16)
    sq_ref[...] = (mm(1536, 1792) * (Q_SCALE * LOG2E)).astype(BF16)
    sk_ref[...] = mm(1792, 2048).astype(BF16)
    sv_ref[...] = mm(2048, 2304).astype(BF16)
    fq_ref[...] = (mm(2304, 2560) * (Q_SCALE * LOG2E)).astype(BF16)
    fk_ref[...] = mm(2560, 2816).astype(BF16)
    fv_ref[...] = mm(2816, 3072).astype(BF16)
    lf_ref[...] = -_softplus(-(_dot(x, wf_ref[...]) + fb_ref[...]))


def _proj_call(hb, w_att, w_forget, cos, sin, fbias, tm):
    S, D = hb.shape
    row = lambda n: pl.BlockSpec((tm, n), lambda i: (i, 0))
    full = lambda a: pl.BlockSpec(a.shape, lambda i: (0, 0))
    widths = [512, 512, 512, 256, 256, 256, 256, 256, 256]
    out_shape = [jax.ShapeDtypeStruct((S, n), BF16) for n in widths]
    out_shape.append(jax.ShapeDtypeStruct((S, LANES), F32))
    return pl.pallas_call(
        _proj_kernel,
        grid=(S // tm,),
        in_specs=[row(D), full(w_att), full(w_forget), row(LANES), row(LANES), full(fbias)],
        out_specs=[row(n) for n in widths] + [row(LANES)],
        out_shape=out_shape,
        compiler_params=_cparams(("parallel",)),
        name="proj",
    )(hb, w_att, w_forget, cos, sin, fbias)


def _fcum_kernel(lf_ref, f_ref):
    nh, R, _ = lf_ref.shape
    r0 = lax.broadcasted_iota(jnp.int32, (LANES, LANES), 0)
    c0 = lax.broadcasted_iota(jnp.int32, (LANES, LANES), 1)
    upper = jnp.where(r0 <= c0, 1.0, 0.0).astype(BF16)
    ones = jnp.ones((LANES, LANES), BF16)
    r1 = lax.broadcasted_iota(jnp.int32, (R, R), 0)
    c1 = lax.broadcasted_iota(jnp.int32, (R, R), 1)
    below = jnp.where(c1 < r1, 1.0, 0.0).astype(BF16)
    for h in range(nh):
        parts = _split3(lf_ref[h])
        within = sum(_dot(p, upper) for p in parts)
        totals = sum(_dot(p, ones) for p in parts)
        offs = sum(_dot(below, t) for t in _split3(totals))
        f_ref[h] = within + offs


def _fcum_call(lf):
    return pl.pallas_call(
        _fcum_kernel,
        out_shape=jax.ShapeDtypeStruct(lf.shape, F32),
        name="fcum",
    )(lf)


def _tile_pos(qi, j, tq, tk):
    row = qi * tq + lax.broadcasted_iota(jnp.int32, (tq, tk), 0)
    col = j * tk + lax.broadcasted_iota(jnp.int32, (tq, tk), 1)
    return row, col


def _split_q(q):
    lane = lax.broadcasted_iota(jnp.int32, (1, LANES), 1)
    lo = lane < HEAD_DIM
    zero = jnp.zeros_like(q)
    return jnp.where(lo, q, zero), jnp.where(lo, zero, q)


STRIP = 16
TRI = 256
PACE_LAG = 128


def _chunk(ref, r0, c):
    return ref[r0:r0 + STRIP, c * LANES:(c + 1) * LANES]


def _zero_after(x):
    half = jnp.uint32(16)
    bits = lax.shift_right_logical(lax.shift_right_logical(pltpu.bitcast(x, jnp.uint32), half), half)
    return pltpu.bitcast(bits, F32)


def _softmax_tile(s_ref, p_ref, m_ref, l_ref, chunk_state, pace, finish_strip):
    tq, tk = s_ref.shape
    for r0 in range(0, tq, STRIP):
        states = [chunk_state(r0, c) for c in range(tk // LANES)]
        sc = []
        for c, st in enumerate(states):
            if st is False:
                sc.append(None)
            elif st is None:
                sc.append(_chunk(s_ref, r0, c))
            else:
                sc.append(jnp.where(st, _chunk(s_ref, r0, c), NEG))
        live = [s for s in sc if s is not None]
        m_prev = m_ref[r0:r0 + STRIP, :]
        zero = pace(r0)
        if zero is not None:
            m_prev = m_prev + zero
        if live:
            mx = functools.reduce(jnp.maximum, live)
            m_new = jnp.maximum(m_prev, jnp.max(mx, axis=1, keepdims=True))
        else:
            m_new = m_prev
        alpha = jnp.exp2(m_prev - m_new)
        psum = jnp.zeros((STRIP, LANES), F32)
        for c, s in enumerate(sc):
            if s is None:
                p_ref[r0:r0 + STRIP, c * LANES:(c + 1) * LANES] = jnp.zeros((STRIP, LANES), BF16)
            else:
                p = jnp.exp2(s - m_new)
                psum = psum + p
                p_ref[r0:r0 + STRIP, c * LANES:(c + 1) * LANES] = p.astype(BF16)
        m_ref[r0:r0 + STRIP, :] = m_new
        l_ref[r0:r0 + STRIP, :] = alpha * l_ref[r0:r0 + STRIP, :] + psum
        finish_strip(r0, alpha)


def _all_visible(r0, c):
    return None


def _init_softmax_state(m_refs, l_refs, acc_refs):
    for m_ref, l_ref, acc_ref in zip(m_refs, l_refs, acc_refs):
        m_ref[...] = jnp.full(m_ref.shape, NEG, F32)
        l_ref[...] = jnp.zeros(l_ref.shape, F32)
        acc_ref[...] = jnp.zeros(acc_ref.shape, F32)


def _softmax_scratch(t):
    per_stream = [pltpu.VMEM((t, t), F32), pltpu.VMEM((t, t), F32),
                  pltpu.VMEM((t, t), BF16), pltpu.VMEM((t, t), BF16),
                  pltpu.VMEM((t, LANES), F32), pltpu.VMEM((t, LANES), F32),
                  pltpu.VMEM((t, LANES), F32), pltpu.VMEM((t, LANES), F32)]
    return per_stream + per_stream


def _softmax_state_refs(scratch):
    return scratch[4::8], scratch[5::8], scratch[6::8]


def _row_total(l_ref):
    return jnp.sum(l_ref[...], axis=1, keepdims=True)


def _sweep(qi, t, scores, v_ref, scratch, diag_chunk_state):
    s0, s1, p0, p1, m_refs, l_refs, acc_refs, pv_refs = (scratch[i::8] for i in range(8))
    s_bufs, p_bufs = (s0, s1), (p0, p1)

    def values(j):
        return v_ref[pl.ds(pl.multiple_of(j * t, t), t), :]

    def step(j, par, chunk_state, prefetch, paced=True):
        s_next = scores(j + 1) if prefetch else None
        vb = values(jnp.maximum(j - 1, 0))
        for x in range(2):
            if prefetch:
                s_bufs[1 - par][x][...] = s_next[x]
            pv_refs[x][...] = _dot(p_bufs[1 - par][x][...], vb)

            def pace(r0, x=x):
                if not (prefetch and paced) or r0 < PACE_LAG:
                    return None
                return _zero_after(s_bufs[1 - par][x][r0 - PACE_LAG:r0 - PACE_LAG + STRIP, :LANES])

            def finish_strip(r0, alpha, x=x):
                rows = slice(r0, r0 + STRIP)
                acc_refs[x][rows, :] = alpha * (acc_refs[x][rows, :] + pv_refs[x][rows, :])

            _softmax_tile(s_bufs[par][x], p_bufs[par][x], m_refs[x], l_refs[x], chunk_state,
                          pace, finish_strip)

    def flush(par):
        vb = values(qi)
        for x in range(2):
            acc_refs[x][...] += _dot(p_bufs[par][x][...], vb)

    first = scores(0)
    for x in range(2):
        p1[x][...] = jnp.zeros(p1[x].shape, BF16)
        s0[x][...] = first[x]

    def pair(i, c):
        step(2 * i, 0, _all_visible, True)
        step(2 * i + 1, 1, _all_visible, True)
        return c

    lax.fori_loop(0, qi // 2, pair, 0)

    @pl.when(qi % 2 == 1)
    def _():
        step(qi - 1, 0, _all_visible, True)
        step(qi, 1, diag_chunk_state, False)
        flush(1)

    @pl.when(qi % 2 == 0)
    def _():
        step(qi, 0, diag_chunk_state, False)
        flush(0)


def _diff_chunk_state(col0):
    def state(r0, c):
        q_chunk = r0 // CHUNK
        k_lo = (col0 + c * LANES) // CHUNK
        k_hi = (col0 + c * LANES + LANES - 1) // CHUNK
        if k_hi <= q_chunk:
            return None
        if k_lo > q_chunk:
            return False
        return lax.broadcasted_iota(jnp.int32, (STRIP, LANES), 1) < CHUNK
    return state


def _diff_kernel(t, lam_init, q_ref, k_ref, v_ref, lamv_ref, g_ref, o_ref, *scratch):
    m_refs, l_refs, acc_refs = _softmax_state_refs(scratch)
    qi = pl.program_id(1)
    qs = _split_q(q_ref[...])
    _init_softmax_state(m_refs, l_refs, acc_refs)

    def scores(j):
        kb = k_ref[pl.ds(pl.multiple_of(j * t, t), t), :]
        return [_dot_nt(qs[x], kb) for x in range(2)]

    _sweep(qi, t, scores, v_ref, scratch, _diff_chunk_state(0))

    lv = lamv_ref[...]
    lam = (jnp.exp(jnp.sum(lv[0:1] * lv[1:2], axis=1, keepdims=True))
           - jnp.exp(jnp.sum(lv[2:3] * lv[3:4], axis=1, keepdims=True)) + lam_init)
    o = (acc_refs[0][...] / _row_total(l_refs[0])
         - lam * (acc_refs[1][...] / _row_total(l_refs[1])))
    o = o * lax.rsqrt(jnp.mean(o * o, axis=-1, keepdims=True) + 1e-6) * g_ref[...]
    o_ref[...] = (o * (1.0 - lam_init)).astype(BF16)


def _diff_call(dq, dk, dv, lamv, gnorm, lam_init, tq):
    S = dq.shape[0]
    nh = dq.shape[1] // LANES
    qspec = pl.BlockSpec((tq, LANES), lambda h, i: (i, h))
    kvspec = pl.BlockSpec((S, LANES), lambda h, i: (0, h))
    full = lambda a: pl.BlockSpec(a.shape, lambda h, i: (0, 0))
    return pl.pallas_call(
        functools.partial(_diff_kernel, tq, lam_init),
        grid=(nh, S // tq),
        in_specs=[qspec, kvspec, kvspec, full(lamv), full(gnorm)],
        out_specs=qspec,
        out_shape=jax.ShapeDtypeStruct(dq.shape, BF16),
        scratch_shapes=_softmax_scratch(tq),
        compiler_params=_cparams(("parallel", "arbitrary")),
        name="diff_attn",
    )(dq, dk, dv, lamv, gnorm)


def _causal_chunk_state(strict, col0=0):
    def state(r0, c):
        k_lo, k_hi = col0 + c * LANES, col0 + c * LANES + LANES - 1
        q_lo, q_hi = r0, r0 + STRIP - 1
        if k_hi < q_lo or (not strict and k_hi <= q_lo):
            return None
        if k_lo > q_hi or (strict and k_lo >= q_hi):
            return False
        row = r0 + lax.broadcasted_iota(jnp.int32, (STRIP, LANES), 0)
        col = k_lo + lax.broadcasted_iota(jnp.int32, (STRIP, LANES), 1)
        return col < row if strict else col <= row
    return state


def _fox_kernel(t, q_ref, k_ref, v_ref, f_ref, o_ref, *scratch):
    m_refs, l_refs, acc_refs = _softmax_state_refs(scratch)
    qi = pl.program_id(1)
    qs = _split_q(q_ref[...])
    _init_softmax_state(m_refs, l_refs, acc_refs)
    f0 = [f_ref[x, qi][:, 0:1] for x in range(2)]

    def scores(j):
        kb = k_ref[pl.ds(pl.multiple_of(j * t, t), t), :]
        return [_dot_nt(qs[x], kb) + (f0[x] - f_ref[x, j]) * LOG2E for x in range(2)]

    _sweep(qi, t, scores, v_ref, scratch, _causal_chunk_state(strict=False))

    lane = lax.broadcasted_iota(jnp.int32, (1, LANES), 1)
    o = jnp.where(lane < HEAD_DIM, acc_refs[0][...] / _row_total(l_refs[0]),
                  acc_refs[1][...] / _row_total(l_refs[1]))
    o_ref[...] = o.astype(BF16)


def _fox_call(fq, fk, fv, fcum, tq):
    S = fq.shape[0]
    npair = fq.shape[1] // LANES
    qspec = pl.BlockSpec((tq, LANES), lambda p, i: (i, p))
    kvspec = pl.BlockSpec((S, LANES), lambda p, i: (0, p))
    nh = fcum.shape[0]
    fspec = pl.BlockSpec((2, S // tq, 1, tq), lambda p, i: (p, 0, 0, 0))
    return pl.pallas_call(
        functools.partial(_fox_kernel, tq),
        grid=(npair, S // tq),
        in_specs=[qspec, kvspec, kvspec, fspec],
        out_specs=qspec,
        out_shape=jax.ShapeDtypeStruct(fq.shape, BF16),
        scratch_shapes=_softmax_scratch(tq),
        compiler_params=_cparams(("parallel", "arbitrary"), SWEEP_FLAGS),
        name="fox_attn",
    )(fq, fk, fv, fcum.reshape(nh, S // tq, 1, tq))


def _sb_kernel(tq, tk, q_ref, k_ref, v_ref, tri_ref, o_ref, *scratch):
    z_refs, u_refs, lb_refs, a_refs, tot_refs, carry_refs, acc_refs = (scratch[i::7] for i in range(7))
    qi = pl.program_id(1)
    qs = _split_q(q_ref[...])
    for x in range(2):
        carry_refs[x][...] = jnp.zeros((tq, LANES), F32)
        acc_refs[x][...] = jnp.zeros((tq, LANES), F32)
    assert tk == tq and tk == 2 * TRI
    nc = tk // LANES

    def scores(j):
        kb = k_ref[pl.ds(pl.multiple_of(j * tk, tk), tk), :]
        for x in range(2):
            z_refs[x][...] = _dot_nt(qs[x], kb)

    def gates(chunk_state):
        for x in range(2):
            z_ref, u_ref, lb_ref = z_refs[x], u_refs[x], lb_refs[x]
            for r0 in range(0, tq, STRIP):
                usum = jnp.zeros((STRIP, LANES), F32)
                for c in range(nc):
                    st = chunk_state(r0, c)
                    if st is False:
                        u_ref[r0:r0 + STRIP, c * LANES:(c + 1) * LANES] = jnp.zeros((STRIP, LANES), BF16)
                        lb_ref[r0:r0 + STRIP, c * LANES:(c + 1) * LANES] = jnp.zeros((STRIP, LANES), F32)
                        continue
                    z = _chunk(z_ref, r0, c)
                    neg_abs = pltpu.bitcast(pltpu.bitcast(z, jnp.uint32) | jnp.uint32(0x80000000), F32)
                    sp = jnp.maximum(z, 0.0) + LOG2E * jnp.log(1.0 + jnp.exp2(neg_abs))
                    u = sp if st is None else jnp.where(st, sp, 0.0)
                    u_ref[r0:r0 + STRIP, c * LANES:(c + 1) * LANES] = u.astype(BF16)
                    lb_ref[r0:r0 + STRIP, c * LANES:(c + 1) * LANES] = z - sp
                    usum = usum + u
                tot_refs[x][r0:r0 + STRIP, :] = (jnp.sum(usum, axis=1, keepdims=True)
                                                 + jnp.zeros((STRIP, LANES), F32))

    def weights_and_accumulate(j, masked):
        vb = v_ref[pl.ds(pl.multiple_of(j * tk, tk), tk), :]
        tri_stack = tri_ref[...]
        for x in range(2):
            u_ref, lb_ref, a_ref = u_refs[x], lb_refs[x], a_refs[x]
            suffix = (_dot(u_ref[...], tri_stack), _dot(u_ref[:, TRI:], tri_stack[:TRI]))
            carry = carry_refs[x][...]
            for c in range(nc):
                cols = slice(c * LANES, (c + 1) * LANES)
                blk, off = divmod(c * LANES, TRI)
                suf = suffix[blk][:, off:off + LANES]
                a = jnp.exp2(lb_ref[:, cols] - suf - carry)
                if masked:
                    row = lax.broadcasted_iota(jnp.int32, (tq, LANES), 0)
                    col = c * LANES + lax.broadcasted_iota(jnp.int32, (tq, LANES), 1)
                    a = jnp.where(col < row, a, 0.0)
                a_ref[:, cols] = a.astype(BF16)
            acc_refs[x][...] += _dot(a_ref[...], vb)
            carry_refs[x][...] = carry + tot_refs[x][...]

    scores(qi)
    gates(_causal_chunk_state(True))
    scores(jnp.maximum(qi - 1, 0))
    weights_and_accumulate(qi, True)

    def body(i, c):
        j = qi - 1 - i
        gates(_all_visible)
        scores(jnp.maximum(j - 1, 0))
        weights_and_accumulate(j, False)
        return c

    lax.fori_loop(0, qi, body, 0)

    lane = lax.broadcasted_iota(jnp.int32, (1, LANES), 1)
    o_ref[...] = jnp.where(lane < HEAD_DIM, acc_refs[0][...], acc_refs[1][...]).astype(BF16)


def _sb_call(sq, sk, sv, tq, tk):
    S = sq.shape[0]
    npair = sq.shape[1] // LANES
    qspec = pl.BlockSpec((tq, LANES), lambda p, i: (i, p))
    kvspec = pl.BlockSpec((S, LANES), lambda p, i: (0, p))
    tri = jnp.asarray(np.concatenate([np.tril(np.ones((TRI, TRI), np.float32), -1),
                                      np.ones((TRI, TRI), np.float32)]), BF16)
    return pl.pallas_call(
        functools.partial(_sb_kernel, tq, tk),
        grid=(npair, S // tq),
        in_specs=[qspec, kvspec, kvspec, pl.BlockSpec((2 * TRI, TRI), lambda p, i: (0, 0))],
        out_specs=qspec,
        out_shape=jax.ShapeDtypeStruct(sq.shape, BF16),
        scratch_shapes=2 * [pltpu.VMEM((tq, tk), F32), pltpu.VMEM((tq, tk), BF16),
                            pltpu.VMEM((tq, tk), F32), pltpu.VMEM((tq, tk), BF16),
                            pltpu.VMEM((tq, LANES), F32), pltpu.VMEM((tq, LANES), F32),
                            pltpu.VMEM((tq, LANES), F32)],
        compiler_params=_cparams(("parallel", "arbitrary")),
        name="sb_attn",
    )(sq, sk, sv, tri)


def _merge_kernel(hb_ref, h_ref, od_ref, os_ref, of_ref, wg_ref, wud_ref, wus_ref, wuf_ref, wo_ref,
                  g_ref, b_ref, out_ref, outb_ref):
    x = hb_ref[...]
    D = x.shape[1]
    merged = None
    for c, (o_ref, wu_ref) in enumerate(((od_ref, wud_ref), (os_ref, wus_ref), (of_ref, wuf_ref))):
        gate = _sigmoid(_dot(x, wg_ref[:, c * D:(c + 1) * D]))
        term = gate * _dot(o_ref[...], wu_ref[...])
        merged = term if merged is None else merged + term
    y = _dot(merged.astype(BF16), wo_ref[...])
    hn = _layer_norm(ALPHA * h_ref[...] + y, g_ref[...], b_ref[...])
    out_ref[...] = hn
    outb_ref[...] = hn.astype(BF16)


def _merge_call(hb, h, od, osb, of, wg, wud, wus, wuf, wo, g, b, tm):
    S, D = h.shape
    row = lambda a: pl.BlockSpec((tm, a.shape[1]), lambda i: (i, 0))
    full = lambda a: pl.BlockSpec(a.shape, lambda i: (0, 0))
    args = (hb, h, od, osb, of, wg, wud, wus, wuf, wo, g, b)
    in_specs = [row(a) for a in args[:5]] + [full(a) for a in args[5:]]
    return pl.pallas_call(
        _merge_kernel,
        grid=(S // tm,),
        in_specs=in_specs,
        out_specs=[row(h), row(h)],
        out_shape=[jax.ShapeDtypeStruct((S, D), F32), jax.ShapeDtypeStruct((S, D), BF16)],
        compiler_params=_cparams(("parallel",)),
        name="merge",
    )(*args)


def _swiglu_hidden(x, wg, wu):
    gate = _dot(x, wg)
    return (gate * _sigmoid(gate)) * _dot(x, wu)


def _ffn_kernel(hb_ref, h_ref, wg_ref, wu_ref, wd_ref, g_ref, b_ref, out_ref, outb_ref, acc_ref):
    f = pl.program_id(1)

    @pl.when(f == 0)
    def _():
        acc_ref[...] = jnp.zeros(acc_ref.shape, F32)

    hid = _swiglu_hidden(hb_ref[...], wg_ref[...], wu_ref[...])
    acc_ref[...] += _dot(hid.astype(BF16), wd_ref[...])

    @pl.when(f == pl.num_programs(1) - 1)
    def _():
        hn = _layer_norm(ALPHA * h_ref[...] + acc_ref[...], g_ref[...], b_ref[...])
        out_ref[...] = hn
        outb_ref[...] = hn.astype(BF16)


def _ffn_call(hb, h, wg, wu, wd, g, b, tm, tf):
    S, D = h.shape
    F = wg.shape[1]
    row = pl.BlockSpec((tm, D), lambda i, f: (i, 0))
    vec = pl.BlockSpec((1, D), lambda i, f: (0, 0))
    return pl.pallas_call(
        _ffn_kernel,
        grid=(S // tm, F // tf),
        in_specs=[row, row,
                  pl.BlockSpec((D, tf), lambda i, f: (0, f)),
                  pl.BlockSpec((D, tf), lambda i, f: (0, f)),
                  pl.BlockSpec((tf, D), lambda i, f: (f, 0)),
                  vec, vec],
        out_specs=[row, row],
        out_shape=[jax.ShapeDtypeStruct((S, D), F32), jax.ShapeDtypeStruct((S, D), BF16)],
        scratch_shapes=[pltpu.VMEM((tm, D), F32)],
        compiler_params=_cparams(("parallel", "arbitrary")),
        name="ffn",
    )(hb, h, wg, wu, wd, g, b)


MOE_TILE = 512
DMA_UNROLL = 8


def _route_kernel(h_ref, w_ref, b_ref, sel_ref, i1_ref, i2_ref, w1_ref, w2_ref):
    logits = jnp.dot(h_ref[...], w_ref[...], precision=lax.Precision.HIGHEST,
                     preferred_element_type=F32) + b_ref[...]
    lane = lax.broadcasted_iota(jnp.int32, logits.shape, 1)
    logits = jnp.where(lane < N_EXPERTS, logits, NEG)
    v1 = jnp.max(logits, axis=1, keepdims=True)
    i1 = jnp.min(jnp.where(logits == v1, lane, LANES), axis=1, keepdims=True)
    rest = jnp.where(lane == i1, NEG, logits)
    v2 = jnp.max(rest, axis=1, keepdims=True)
    i2 = jnp.min(jnp.where(rest == v2, lane, LANES), axis=1, keepdims=True)
    e2 = jnp.exp(v2 - v1)
    zero = jnp.zeros(logits.shape, F32)
    sel_ref[...] = jnp.where((lane == i1) | (lane == i2), 1.0, 0.0).astype(BF16)
    i1_ref[...] = i1 + jnp.zeros(logits.shape, jnp.int32)
    i2_ref[...] = i2 + jnp.zeros(logits.shape, jnp.int32)
    w1_ref[...] = 1.0 / (1.0 + e2) + zero
    w2_ref[...] = e2 / (1.0 + e2) + zero


def _route_call(h, w_router, b_router, tm):
    S, D = h.shape
    wpad = jnp.zeros((D, LANES), F32).at[:, :N_EXPERTS].set(w_router)
    bpad = jnp.zeros((1, LANES), F32).at[0, :N_EXPERTS].set(b_router)
    row = pl.BlockSpec((tm, LANES), lambda i: (i, 0))
    return pl.pallas_call(
        _route_kernel,
        grid=(S // tm,),
        in_specs=[pl.BlockSpec((tm, D), lambda i: (i, 0)),
                  pl.BlockSpec((D, LANES), lambda i: (0, 0)),
                  pl.BlockSpec((1, LANES), lambda i: (0, 0))],
        out_specs=[row] * 5,
        out_shape=[jax.ShapeDtypeStruct((S, LANES), BF16),
                   jax.ShapeDtypeStruct((S, LANES), jnp.int32),
                   jax.ShapeDtypeStruct((S, LANES), jnp.int32),
                   jax.ShapeDtypeStruct((S, LANES), F32),
                   jax.ShapeDtypeStruct((S, LANES), F32)],
        compiler_params=_cparams(("parallel",)),
        name="route",
    )(h, wpad, bpad)


def _rank_kernel(sel_ref, i1_ref, i2_ref, tril_ref, rank_ref, cnt_ref):
    @pl.when(pl.program_id(0) == 0)
    def _():
        cnt_ref[...] = jnp.zeros(cnt_ref.shape, F32)

    sel = sel_ref[...]
    rank = _dot(tril_ref[...], sel) + cnt_ref[...]
    lane = lax.broadcasted_iota(jnp.int32, rank.shape, 1)
    r1 = jnp.sum(jnp.where(lane == i1_ref[...], rank, 0.0), axis=1, keepdims=True)
    r2 = jnp.sum(jnp.where(lane == i2_ref[...], rank, 0.0), axis=1, keepdims=True)
    rank_ref[...] = jnp.where(lane == 0, r1, jnp.where(lane == 1, r2, 0.0))
    cnt_ref[...] += jnp.sum(sel.astype(F32), axis=0, keepdims=True)


def _rank_call(sel, i1b, i2b, tb):
    S = sel.shape[0]
    tril = jnp.asarray(np.tril(np.ones((tb, tb), np.float32), -1), BF16)
    row = pl.BlockSpec((tb, LANES), lambda i: (i, 0))
    return pl.pallas_call(
        _rank_kernel,
        grid=(S // tb,),
        in_specs=[row, row, row, pl.BlockSpec((tb, tb), lambda i: (0, 0))],
        out_specs=[row, pl.BlockSpec((1, LANES), lambda i: (0, 0))],
        out_shape=[jax.ShapeDtypeStruct((S, LANES), F32), jax.ShapeDtypeStruct((1, LANES), F32)],
        compiler_params=_cparams(("arbitrary",)),
        name="rank",
    )(sel, i1b, i2b, tril)


def _row_copy(src_ref, s, dst_ref, d, sem):
    return pltpu.make_async_copy(src_ref.at[pl.ds(s, 1)], dst_ref.at[pl.ds(d, 1)], sem)


def _dispatch_kernel(tb, n_tok, dest_ref, h_ref, xs_in_hbm, xs_hbm, sem):
    del xs_in_hbm
    base = pl.program_id(0) * tb

    def issue(t, c):
        _row_copy(h_ref, t, xs_hbm, dest_ref[base + t], sem).start(priority=0)
        _row_copy(h_ref, t, xs_hbm, dest_ref[n_tok + base + t], sem).start(priority=1)
        return c

    lax.fori_loop(0, tb, issue, 0, unroll=DMA_UNROLL)

    def drain(t, c):
        _row_copy(h_ref, 0, xs_hbm, 0, sem).wait()
        _row_copy(h_ref, 0, xs_hbm, 0, sem).wait()
        return c

    lax.fori_loop(0, tb, drain, 0, unroll=DMA_UNROLL)


def _dispatch_call(dest, h, n_rows, tb):
    S, D = h.shape
    xs0 = jnp.zeros((n_rows, D), F32)
    return pl.pallas_call(
        functools.partial(_dispatch_kernel, tb, S),
        grid_spec=pltpu.PrefetchScalarGridSpec(
            num_scalar_prefetch=1,
            grid=(S // tb,),
            in_specs=[pl.BlockSpec((tb, D), lambda i, d: (i, 0)), pl.BlockSpec(memory_space=pl.ANY)],
            out_specs=pl.BlockSpec(memory_space=pl.ANY),
            scratch_shapes=[pltpu.SemaphoreType.DMA],
        ),
        out_shape=jax.ShapeDtypeStruct((n_rows, D), F32),
        input_output_aliases={2: 0},
        compiler_params=_cparams(("arbitrary",)),
        name="dispatch",
    )(dest, h, xs0)


def _experts_kernel(te_ref, nu_ref, xs_ref, wg_ref, wu_ref, wd_ref, ys_ref, xb_ref, acc_ref):
    i = pl.program_id(0)
    f = pl.program_id(1)
    last_f = pl.num_programs(1) - 1
    used = i < nu_ref[0]

    @pl.when(used & (f == 0))
    def _():
        xb_ref[...] = xs_ref[...].astype(BF16)

    @pl.when(used)
    def _():
        hid = _swiglu_hidden(xb_ref[...], wg_ref[0], wu_ref[0])
        part = _dot(hid.astype(BF16), wd_ref[0])

        @pl.when(f == 0)
        def _():
            acc_ref[...] = part

        @pl.when(f > 0)
        def _():
            acc_ref[...] += part

    @pl.when(used & (f == last_f))
    def _():
        ys_ref[...] = acc_ref[...]

    @pl.when(jnp.logical_not(used) & (f == last_f))
    def _():
        ys_ref[...] = jnp.zeros(ys_ref.shape, F32)


def _experts_call(tile_expert, n_used, xs, wg, wu, wd, tmx, tf):
    P, D = xs.shape
    F = wg.shape[2]
    nf = F // tf

    def fsel(i, f, nu):
        return jnp.where(i < nu[0], f, nf - 1)

    return pl.pallas_call(
        _experts_kernel,
        grid_spec=pltpu.PrefetchScalarGridSpec(
            num_scalar_prefetch=2,
            grid=(P // tmx, nf),
            in_specs=[pl.BlockSpec((tmx, D), lambda i, f, te, nu: (i, 0)),
                      pl.BlockSpec((1, D, tf), lambda i, f, te, nu: (te[i], 0, fsel(i, f, nu))),
                      pl.BlockSpec((1, D, tf), lambda i, f, te, nu: (te[i], 0, fsel(i, f, nu))),
                      pl.BlockSpec((1, tf, D), lambda i, f, te, nu: (te[i], fsel(i, f, nu), 0))],
            out_specs=pl.BlockSpec((tmx, D), lambda i, f, te, nu: (i, 0)),
            scratch_shapes=[pltpu.VMEM((tmx, D), BF16), pltpu.VMEM((tmx, D), F32)],
        ),
        out_shape=jax.ShapeDtypeStruct((P, D), F32),
        compiler_params=_cparams(("arbitrary", "arbitrary")),
        name="experts",
    )(tile_expert, n_used, xs, wg, wu, wd)


def _combine_kernel(tm, n_tok, dest_ref, ys_hbm, h_ref, w1_ref, w2_ref, g_ref, b_ref,
                    out_ref, outb_ref, buf_ref, sem):
    i = pl.program_id(0)
    slot = i % 2

    def gather(tile, slot_, start):
        base = tile * tm

        def one(t, c):
            for k in range(2):
                src_row = dest_ref[k * n_tok + base + t] if start else 0
                cp = pltpu.make_async_copy(ys_hbm.at[pl.ds(src_row, 1)],
                                           buf_ref.at[slot_, k, pl.ds(t, 1)], sem.at[slot_])
                if start:
                    cp.start(priority=k)
                else:
                    cp.wait()
            return c

        lax.fori_loop(0, tm, one, 0, unroll=DMA_UNROLL)

    @pl.when(i == 0)
    def _():
        gather(0, 0, True)

    @pl.when(i + 1 < pl.num_programs(0))
    def _():
        gather(i + 1, 1 - slot, True)

    gather(i, slot, False)

    w1 = w1_ref[...]
    w2 = w2_ref[...]
    y = jnp.concatenate(
        [w1 * buf_ref[slot, 0, :, c * LANES:(c + 1) * LANES]
         + w2 * buf_ref[slot, 1, :, c * LANES:(c + 1) * LANES]
         for c in range(h_ref.shape[1] // LANES)], axis=1)
    hn = _layer_norm(ALPHA * h_ref[...] + y, g_ref[...], b_ref[...])
    out_ref[...] = hn
    outb_ref[...] = hn.astype(BF16)


def _combine_call(dest, ys, h, w1b, w2b, g, b, tm):
    S, D = h.shape
    row = pl.BlockSpec((tm, D), lambda i, d: (i, 0))
    lrow = pl.BlockSpec((tm, LANES), lambda i, d: (i, 0))
    vec = pl.BlockSpec((1, D), lambda i, d: (0, 0))
    return pl.pallas_call(
        functools.partial(_combine_kernel, tm, S),
        grid_spec=pltpu.PrefetchScalarGridSpec(
            num_scalar_prefetch=1,
            grid=(S // tm,),
            in_specs=[pl.BlockSpec(memory_space=pl.ANY), row, lrow, lrow, vec, vec],
            out_specs=[row, row],
            scratch_shapes=[pltpu.VMEM((2, 2, tm, D), F32), pltpu.SemaphoreType.DMA((2,))],
        ),
        out_shape=[jax.ShapeDtypeStruct((S, D), F32), jax.ShapeDtypeStruct((S, D), BF16)],
        compiler_params=_cparams(("arbitrary",)),
        name="combine",
    )(dest, ys, h, w1b, w2b, g, b)


def _moe_call(h, w_router, b_router, wg, wu, wd, g, b, tm, tf):
    S, D = h.shape
    tmx = MOE_TILE
    sel, i1b, i2b, w1b, w2b = _route_call(h, w_router, b_router, tm)
    ranks, counts = _rank_call(sel, i1b, i2b, min(256, S))
    cnt = counts[0, :N_EXPERTS].astype(jnp.int32)
    padded = ((cnt + tmx - 1) // tmx) * tmx
    ends = jnp.cumsum(padded)
    off = ends - padded
    dest = jnp.concatenate([off[i1b[:, 0]] + ranks[:, 0].astype(jnp.int32),
                            off[i2b[:, 0]] + ranks[:, 1].astype(jnp.int32)])
    n_tiles = (2 * S) // tmx + N_EXPERTS
    tile_start = jnp.arange(n_tiles, dtype=jnp.int32) * tmx
    tile_expert = jnp.minimum(jnp.sum(tile_start[:, None] >= ends[None, :], axis=1),
                              N_EXPERTS - 1).astype(jnp.int32)
    n_used = (ends[-1] // tmx).astype(jnp.int32).reshape(1)
    xs = _dispatch_call(dest, h, n_tiles * tmx, tm)
    ys = _experts_call(tile_expert, n_used, xs, wg, wu, wd, tmx, tf)
    return _combine_call(dest, ys, h, w1b, w2b, g, b, min(256, S))


def _rope_tables(S):
    half = HEAD_DIM // 2
    pos = jnp.arange(S, dtype=F32)
    inv = ROPE_THETA ** (-jnp.arange(half, dtype=F32) / half)
    ang = pos[:, None] * inv[None, :]
    cos, sin = jnp.cos(ang), jnp.sin(ang)
    reps = LANES // HEAD_DIM
    cos_t = jnp.tile(jnp.concatenate([cos, cos], axis=1), (1, reps))
    sin_t = jnp.tile(jnp.concatenate([-sin, sin], axis=1), (1, reps))
    return cos_t, sin_t


def _pick(S, pref):
    t = min(pref, S)
    assert S % t == 0
    return t


def _forward(x, ln_in_g, ln_in_b, w_in, b_forget, lam_q1, lam_k1, lam_q2, lam_k2,
             diff_norm_g, w_up_diff, w_up_sb, w_up_fox, w_out, ln_mix_g, ln_mix_b,
             ln_ffn_g, ln_ffn_b, w_gate_dense, w_up_dense, w_down_dense, w_router,
             b_router, w_gate_moe, w_up_moe, w_down_moe):
    B, S, D = x.shape
    assert B == 1 and D == D_MODEL and S % LANES == 0
    depth = w_in.shape[0]
    tm = _pick(S, 512)
    tq = _pick(S, 512)
    tm_ffn = _pick(S, 1024)
    tf = D_FF // 4
    tf_moe = D_FF // 2
    vec = lambda a: a.reshape(1, -1).astype(F32)

    cos_t, sin_t = _rope_tables(S)
    h, hb = _ln_call(x.reshape(S, D), ln_in_g, ln_in_b, tm)
    w_atts, w_forgets, w_gate_logits = _wsplit_call(w_in, 128)
    for l in range(depth):
        w_att, w_forget, w_gate_logit = w_atts[l], w_forgets[l], w_gate_logits[l]
        fbias = jnp.pad(b_forget[l].astype(F32), (0, LANES - N_HEADS_FOX)).reshape(1, LANES)
        dq, dk, dv, sq, sk, sv, fq, fk, fv, logf = _proj_call(hb, w_att, w_forget, cos_t, sin_t,
                                                              fbias, tm)

        lf = logf[:, :N_HEADS_FOX].T.reshape(N_HEADS_FOX, S // LANES, LANES)
        fcum = _fcum_call(lf)

        lam_init = 0.8 - 0.6 * math.exp(-0.3 * l)
        lamv = jnp.stack([lam_q1[l], lam_k1[l], lam_q2[l], lam_k2[l]]).astype(F32)
        o_diff = _diff_call(dq, dk, dv, lamv, vec(diff_norm_g[l]), lam_init, tq)
        o_sb = _sb_call(sq, sk, sv, tq, tq)
        o_fox = _fox_call(fq, fk, fv, fcum, tq)

        h, hb = _merge_call(hb, h, o_diff, o_sb, o_fox, w_gate_logit,
                            w_up_diff[l].astype(BF16), w_up_sb[l].astype(BF16),
                            w_up_fox[l].astype(BF16), w_out[l].astype(BF16),
                            vec(ln_mix_g[l]), vec(ln_mix_b[l]), tm)
        j = l // 2
        if l % 2 == 0:
            h, hb = _ffn_call(hb, h, w_gate_dense[j].astype(BF16), w_up_dense[j].astype(BF16),
                              w_down_dense[j].astype(BF16), vec(ln_ffn_g[l]), vec(ln_ffn_b[l]),
                              tm_ffn, tf)
        else:
            h, hb = _moe_call(h, w_router[j], b_router[j], w_gate_moe[j].astype(BF16),
                              w_up_moe[j].astype(BF16), w_down_moe[j].astype(BF16),
                              vec(ln_ffn_g[l]), vec(ln_ffn_b[l]), tm, tf_moe)
    return h.reshape(B, S, D)


def kernel(x, ln_in_g, ln_in_b, w_in, b_forget, lam_q1, lam_k1, lam_q2, lam_k2, diff_norm_g,
           w_up_diff, w_up_sb, w_up_fox, w_out, ln_mix_g, ln_mix_b, ln_ffn_g, ln_ffn_b,
           w_gate_dense, w_up_dense, w_down_dense, w_router, b_router, w_gate_moe, w_up_moe,
           w_down_moe):
    return _forward(x, ln_in_g, ln_in_b, w_in, b_forget, lam_q1, lam_k1, lam_q2, lam_k2,
                    diff_norm_g, w_up_diff, w_up_sb, w_up_fox, w_out, ln_mix_g, ln_mix_b,
                    ln_ffn_g, ln_ffn_b, w_gate_dense, w_up_dense, w_down_dense, w_router,
                    b_router, w_gate_moe, w_up_moe, w_down_moe)
```

```python
import functools
import math

import jax
import jax.numpy as jnp
import numpy as np
from jax import lax
from jax.experimental import pallas as pl
from jax.experimental.pallas import tpu as pltpu

D_MODEL = 1024
DEPTH = 2
CHUNK = 64
HEAD_DIM = 64
N_HEADS_DIFF = 4
N_HEADS_SB = 4
N_HEADS_FOX = 4
WIDTH_DIFF = N_HEADS_DIFF * 2 * HEAD_DIM
WIDTH_SB = N_HEADS_SB * HEAD_DIM
WIDTH_FOX = N_HEADS_FOX * HEAD_DIM
ROPE_THETA = 10000.0
N_EXPERTS = 8
D_FF = 3584
ALPHA = (2 * DEPTH) ** 0.25
Q_SCALE = HEAD_DIM ** -0.5
LOG2E = 1.4426950408889634
N_ATT = 3 * WIDTH_DIFF + 3 * WIDTH_SB + 3 * WIDTH_FOX
LANES = 128
NEG = -1e30

F32 = jnp.float32
BF16 = jnp.bfloat16

VMEM_LIMIT = 56 * 1024 * 1024


def _cparams(sem):
    return pltpu.CompilerParams(dimension_semantics=sem, vmem_limit_bytes=VMEM_LIMIT)


def _layer_norm(z, g, b, eps=1e-5):
    mu = jnp.mean(z, axis=-1, keepdims=True)
    zc = z - mu
    var = jnp.mean(zc * zc, axis=-1, keepdims=True)
    return zc * lax.rsqrt(var + eps) * g + b


def _sigmoid(x):
    return 1.0 / (1.0 + jnp.exp(-x))


def _softplus(x):
    return jnp.maximum(x, 0.0) + jnp.log1p(jnp.exp(-jnp.abs(x)))


def _dot(a, b):
    return jnp.dot(a, b, preferred_element_type=F32)


def _dot_nt(a, b):
    return lax.dot_general(a, b, (((1,), (1,)), ((), ())), preferred_element_type=F32)


def _split3(x):
    x1 = x.astype(BF16)
    r = x - x1.astype(F32)
    x2 = r.astype(BF16)
    x3 = (r - x2.astype(F32)).astype(BF16)
    return x1, x2, x3


def _ln_kernel(x_ref, g_ref, b_ref, h_ref, hb_ref):
    h = _layer_norm(x_ref[...], g_ref[...], b_ref[...])
    h_ref[...] = h
    hb_ref[...] = h.astype(BF16)


def _ln_call(x, g, b, tm):
    S, D = x.shape
    row = pl.BlockSpec((tm, D), lambda i: (i, 0))
    vec = pl.BlockSpec((1, D), lambda i: (0, 0))
    return pl.pallas_call(
        _ln_kernel,
        grid=(S // tm,),
        in_specs=[row, vec, vec],
        out_specs=[row, row],
        out_shape=[jax.ShapeDtypeStruct((S, D), F32), jax.ShapeDtypeStruct((S, D), BF16)],
        compiler_params=_cparams(("parallel",)),
        name="ln_in",
    )(x, g.reshape(1, D), b.reshape(1, D))


def _wsplit_kernel(w_ref, att_ref, fgt_ref, gate_ref):
    att_ref[0] = w_ref[0, :, :N_ATT].astype(BF16)
    lane = lax.broadcasted_iota(jnp.int32, (1, LANES), 1)
    fgt_ref[0] = jnp.where(lane < N_HEADS_FOX, w_ref[0, :, N_ATT:N_ATT + LANES], 0.0).astype(BF16)
    gate_ref[0] = w_ref[0, :, N_ATT + N_HEADS_FOX:].astype(BF16)


def _wsplit_call(w_in, tr):
    depth, D, n_in = w_in.shape
    n_gate = n_in - N_ATT - N_HEADS_FOX
    spec = lambda n: pl.BlockSpec((1, tr, n), lambda l, i: (l, i, 0))
    return pl.pallas_call(
        _wsplit_kernel,
        grid=(depth, D // tr),
        in_specs=[spec(n_in)],
        out_specs=[spec(N_ATT), spec(LANES), spec(n_gate)],
        out_shape=[jax.ShapeDtypeStruct((depth, D, N_ATT), BF16),
                   jax.ShapeDtypeStruct((depth, D, LANES), BF16),
                   jax.ShapeDtypeStruct((depth, D, n_gate), BF16)],
        compiler_params=_cparams(("parallel", "parallel")),
        name="wsplit",
    )(w_in)


def _proj_kernel(hb_ref, w_ref, wf_ref, cos_ref, sin_ref, fb_ref,
                 dq_ref, dk_ref, dv_ref, sq_ref, sk_ref, sv_ref, fq_ref, fk_ref, fv_ref, lf_ref):
    x = hb_ref[...]
    cos = cos_ref[...]
    sin = sin_ref[...]
    lane = lax.broadcasted_iota(jnp.int32, (1, LANES), 1)
    lower = (lane & (HEAD_DIM // 2)) == 0

    def mm(c0, c1):
        return _dot(x, w_ref[:, c0:c1])

    def rope_store(y, out_ref, scale):
        for g in range(y.shape[1] // LANES):
            yg = y[:, g * LANES:(g + 1) * LANES]
            partner = jnp.where(lower, pltpu.roll(yg, LANES - HEAD_DIM // 2, 1),
                                pltpu.roll(yg, HEAD_DIM // 2, 1))
            r = yg * cos + partner * sin
            if scale != 1.0:
                r = r * scale
            out_ref[:, g * LANES:(g + 1) * LANES] = r.astype(BF16)

    rope_store(mm(0, 512), dq_ref, Q_SCALE * LOG2E)
    rope_store(mm(512, 1024), dk_ref, 1.0)
    dv_ref[...] = mm(1024, 1536).astype(BF16)
    sq_ref[...] = (mm(1536, 1792) * (Q_SCALE * LOG2E)).astype(BF16)
    sk_ref[...] = mm(1792, 2048).astype(BF16)
    sv_ref[...] = mm(2048, 2304).astype(BF16)
    fq_ref[...] = (mm(2304, 2560) * (Q_SCALE * LOG2E)).astype(BF16)
    fk_ref[...] = mm(2560, 2816).astype(BF16)
    fv_ref[...] = mm(2816, 3072).astype(BF16)
    lf_ref[...] = -_softplus(-(_dot(x, wf_ref[...]) + fb_ref[...]))


def _proj_call(hb, w_att, w_forget, cos, sin, fbias, tm):
    S, D = hb.shape
    row = lambda n: pl.BlockSpec((tm, n), lambda i: (i, 0))
    full = lambda a: pl.BlockSpec(a.shape, lambda i: (0, 0))
    widths = [512, 512, 512, 256, 256, 256, 256, 256, 256]
    out_shape = [jax.ShapeDtypeStruct((S, n), BF16) for n in widths]
    out_shape.append(jax.ShapeDtypeStruct((S, LANES), F32))
    return pl.pallas_call(
        _proj_kernel,
        grid=(S // tm,),
        in_specs=[row(D), full(w_att), full(w_forget), row(LANES), row(LANES), full(fbias)],
        out_specs=[row(n) for n in widths] + [row(LANES)],
        out_shape=out_shape,
        compiler_params=_cparams(("parallel",)),
        name="proj",
    )(hb, w_att, w_forget, cos, sin, fbias)


def _fcum_kernel(lf_ref, f_ref):
    nh, R, _ = lf_ref.shape
    r0 = lax.broadcasted_iota(jnp.int32, (LANES, LANES), 0)
    c0 = lax.broadcasted_iota(jnp.int32, (LANES, LANES), 1)
    upper = jnp.where(r0 <= c0, 1.0, 0.0).astype(BF16)
    ones = jnp.ones((LANES, LANES), BF16)
    r1 = lax.broadcasted_iota(jnp.int32, (R, R), 0)
    c1 = lax.broadcasted_iota(jnp.int32, (R, R), 1)
    below = jnp.where(c1 < r1, 1.0, 0.0).astype(BF16)
    for h in range(nh):
        parts = _split3(lf_ref[h])
        within = sum(_dot(p, upper) for p in parts)
        totals = sum(_dot(p, ones) for p in parts)
        offs = sum(_dot(below, t) for t in _split3(totals))
        f_ref[h] = within + offs


def _fcum_call(lf):
    return pl.pallas_call(
        _fcum_kernel,
        out_shape=jax.ShapeDtypeStruct(lf.shape, F32),
        name="fcum",
    )(lf)


def _tile_pos(qi, j, tq, tk):
    row = qi * tq + lax.broadcasted_iota(jnp.int32, (tq, tk), 0)
    col = j * tk + lax.broadcasted_iota(jnp.int32, (tq, tk), 1)
    return row, col


def _split_q(q):
    lane = lax.broadcasted_iota(jnp.int32, (1, LANES), 1)
    lo = lane < HEAD_DIM
    zero = jnp.zeros_like(q)
    return jnp.where(lo, q, zero), jnp.where(lo, zero, q)


STRIP = 16
TRI = 256
PACE_LAG = 128


def _chunk(ref, r0, c):
    return ref[r0:r0 + STRIP, c * LANES:(c + 1) * LANES]


def _zero_after(x):
    half = jnp.uint32(16)
    bits = lax.shift_right_logical(lax.shift_right_logical(pltpu.bitcast(x, jnp.uint32), half), half)
    return pltpu.bitcast(bits, F32)


def _softmax_tile(s_ref, p_ref, m_ref, l_ref, chunk_state, pace, finish_strip):
    tq, tk = s_ref.shape
    for r0 in range(0, tq, STRIP):
        states = [chunk_state(r0, c) for c in range(tk // LANES)]
        sc = []
        for c, st in enumerate(states):
            if st is False:
                sc.append(None)
            elif st is None:
                sc.append(_chunk(s_ref, r0, c))
            else:
                sc.append(jnp.where(st, _chunk(s_ref, r0, c), NEG))
        live = [s for s in sc if s is not None]
        m_prev = m_ref[r0:r0 + STRIP, :]
        zero = pace(r0)
        if zero is not None:
            m_prev = m_prev + zero
        if live:
            mx = functools.reduce(jnp.maximum, live)
            m_new = jnp.maximum(m_prev, jnp.max(mx, axis=1, keepdims=True))
        else:
            m_new = m_prev
        alpha = jnp.exp2(m_prev - m_new)
        psum = jnp.zeros((STRIP, LANES), F32)
        for c, s in enumerate(sc):
            if s is None:
                p_ref[r0:r0 + STRIP, c * LANES:(c + 1) * LANES] = jnp.zeros((STRIP, LANES), BF16)
            else:
                p = jnp.exp2(s - m_new)
                psum = psum + p
                p_ref[r0:r0 + STRIP, c * LANES:(c + 1) * LANES] = p.astype(BF16)
        m_ref[r0:r0 + STRIP, :] = m_new
        l_ref[r0:r0 + STRIP, :] = alpha * l_ref[r0:r0 + STRIP, :] + psum
        finish_strip(r0, alpha)


def _all_visible(r0, c):
    return None


def _init_softmax_state(m_refs, l_refs, acc_refs):
    for m_ref, l_ref, acc_ref in zip(m_refs, l_refs, acc_refs):
        m_ref[...] = jnp.full(m_ref.shape, NEG, F32)
        l_ref[...] = jnp.zeros(l_ref.shape, F32)
        acc_ref[...] = jnp.zeros(acc_ref.shape, F32)


def _softmax_scratch(t):
    per_stream = [pltpu.VMEM((t, t), F32), pltpu.VMEM((t, t), F32),
                  pltpu.VMEM((t, t), BF16), pltpu.VMEM((t, t), BF16),
                  pltpu.VMEM((t, LANES), F32), pltpu.VMEM((t, LANES), F32),
                  pltpu.VMEM((t, LANES), F32)]
    return per_stream + per_stream


def _softmax_state_refs(scratch):
    return scratch[4::7], scratch[5::7], scratch[6::7]


def _row_total(l_ref):
    return jnp.sum(l_ref[...], axis=1, keepdims=True)


def _sweep(qi, t, scores, v_ref, scratch, diag_chunk_state):
    s0, s1, p0, p1, m_refs, l_refs, acc_refs = (scratch[i::7] for i in range(7))
    s_bufs, p_bufs = (s0, s1), (p0, p1)

    def step(j, par, chunk_state, prefetch, earlier_pv=None):
        s_next = scores(j + 1) if prefetch else None
        vb = v_ref[pl.ds(pl.multiple_of(j * t, t), t), :]
        pvs = []
        for x in range(2):
            if prefetch:
                s_bufs[1 - par][x][...] = s_next[x]

            def pace(r0, x=x):
                zero = None
                if earlier_pv is not None:
                    zero = _zero_after(earlier_pv[x][r0:r0 + STRIP, :])
                if prefetch and r0 >= PACE_LAG:
                    rows = slice(r0 - PACE_LAG, r0 - PACE_LAG + STRIP)
                    z2 = _zero_after(s_bufs[1 - par][x][rows, :LANES])
                    zero = z2 if zero is None else zero + z2
                return zero

            def finish_strip(r0, alpha, x=x):
                rows = slice(r0, r0 + STRIP)
                acc_refs[x][rows, :] = alpha * acc_refs[x][rows, :]

            _softmax_tile(s_bufs[par][x], p_bufs[par][x], m_refs[x], l_refs[x], chunk_state,
                          pace, finish_strip)
            pv = _dot(p_bufs[par][x][...], vb)
            acc_refs[x][...] += pv
            pvs.append(pv)
        return pvs

    first = scores(0)
    for x in range(2):
        s0[x][...] = first[x]

    def pair(i, c):
        pvs = step(2 * i, 0, _all_visible, True)
        step(2 * i + 1, 1, _all_visible, True, pvs)
        return c

    lax.fori_loop(0, qi // 2, pair, 0)

    @pl.when(qi % 2 == 1)
    def _():
        pvs = step(qi - 1, 0, _all_visible, True)
        step(qi, 1, diag_chunk_state, False, pvs)

    @pl.when(qi % 2 == 0)
    def _():
        step(qi, 0, diag_chunk_state, False)


def _diff_chunk_state(col0):
    def state(r0, c):
        q_chunk = r0 // CHUNK
        k_lo = (col0 + c * LANES) // CHUNK
        k_hi = (col0 + c * LANES + LANES - 1) // CHUNK
        if k_hi <= q_chunk:
            return None
        if k_lo > q_chunk:
            return False
        return lax.broadcasted_iota(jnp.int32, (STRIP, LANES), 1) < CHUNK
    return state


def _diff_kernel(t, lam_init, q_ref, k_ref, v_ref, lamv_ref, g_ref, o_ref, *scratch):
    m_refs, l_refs, acc_refs = _softmax_state_refs(scratch)
    qi = pl.program_id(1)
    qs = _split_q(q_ref[...])
    _init_softmax_state(m_refs, l_refs, acc_refs)

    def scores(j):
        kb = k_ref[pl.ds(pl.multiple_of(j * t, t), t), :]
        return [_dot_nt(qs[x], kb) for x in range(2)]

    _sweep(qi, t, scores, v_ref, scratch, _diff_chunk_state(0))

    lv = lamv_ref[...]
    lam = (jnp.exp(jnp.sum(lv[0:1] * lv[1:2], axis=1, keepdims=True))
           - jnp.exp(jnp.sum(lv[2:3] * lv[3:4], axis=1, keepdims=True)) + lam_init)
    o = (acc_refs[0][...] / _row_total(l_refs[0])
         - lam * (acc_refs[1][...] / _row_total(l_refs[1])))
    o = o * lax.rsqrt(jnp.mean(o * o, axis=-1, keepdims=True) + 1e-6) * g_ref[...]
    o_ref[...] = (o * (1.0 - lam_init)).astype(BF16)


def _diff_call(dq, dk, dv, lamv, gnorm, lam_init, tq):
    S = dq.shape[0]
    nh = dq.shape[1] // LANES
    qspec = pl.BlockSpec((tq, LANES), lambda h, i: (i, h))
    kvspec = pl.BlockSpec((S, LANES), lambda h, i: (0, h))
    full = lambda a: pl.BlockSpec(a.shape, lambda h, i: (0, 0))
    return pl.pallas_call(
        functools.partial(_diff_kernel, tq, lam_init),
        grid=(nh, S // tq),
        in_specs=[qspec, kvspec, kvspec, full(lamv), full(gnorm)],
        out_specs=qspec,
        out_shape=jax.ShapeDtypeStruct(dq.shape, BF16),
        scratch_shapes=_softmax_scratch(tq),
        compiler_params=_cparams(("parallel", "arbitrary")),
        name="diff_attn",
    )(dq, dk, dv, lamv, gnorm)


def _causal_chunk_state(strict, col0=0):
    def state(r0, c):
        k_lo, k_hi = col0 + c * LANES, col0 + c * LANES + LANES - 1
        q_lo, q_hi = r0, r0 + STRIP - 1
        if k_hi < q_lo or (not strict and k_hi <= q_lo):
            return None
        if k_lo > q_hi or (strict and k_lo >= q_hi):
            return False
        row = r0 + lax.broadcasted_iota(jnp.int32, (STRIP, LANES), 0)
        col = k_lo + lax.broadcasted_iota(jnp.int32, (STRIP, LANES), 1)
        return col < row if strict else col <= row
    return state


def _fox_kernel(t, q_ref, k_ref, v_ref, f_ref, o_ref, *scratch):
    m_refs, l_refs, acc_refs = _softmax_state_refs(scratch)
    qi = pl.program_id(1)
    qs = _split_q(q_ref[...])
    _init_softmax_state(m_refs, l_refs, acc_refs)
    f0 = [f_ref[x, qi][:, 0:1] for x in range(2)]

    def scores(j):
        kb = k_ref[pl.ds(pl.multiple_of(j * t, t), t), :]
        return [_dot_nt(qs[x], kb) + (f0[x] - f_ref[x, j]) * LOG2E for x in range(2)]

    _sweep(qi, t, scores, v_ref, scratch, _causal_chunk_state(strict=False))

    lane = lax.broadcasted_iota(jnp.int32, (1, LANES), 1)
    o = jnp.where(lane < HEAD_DIM, acc_refs[0][...] / _row_total(l_refs[0]),
                  acc_refs[1][...] / _row_total(l_refs[1]))
    o_ref[...] = o.astype(BF16)


def _fox_call(fq, fk, fv, fcum, tq):
    S = fq.shape[0]
    npair = fq.shape[1] // LANES
    qspec = pl.BlockSpec((tq, LANES), lambda p, i: (i, p))
    kvspec = pl.BlockSpec((S, LANES), lambda p, i: (0, p))
    nh = fcum.shape[0]
    fspec = pl.BlockSpec((2, S // tq, 1, tq), lambda p, i: (p, 0, 0, 0))
    return pl.pallas_call(
        functools.partial(_fox_kernel, tq),
        grid=(npair, S // tq),
        in_specs=[qspec, kvspec, kvspec, fspec],
        out_specs=qspec,
        out_shape=jax.ShapeDtypeStruct(fq.shape, BF16),
        scratch_shapes=_softmax_scratch(tq),
        compiler_params=_cparams(("parallel", "arbitrary")),
        name="fox_attn",
    )(fq, fk, fv, fcum.reshape(nh, S // tq, 1, tq))


def _sb_kernel(tq, tk, q_ref, k_ref, v_ref, tri_ref, o_ref, *scratch):
    z_refs, u_refs, lb_refs, a_refs, tot_refs, carry_refs, acc_refs = (scratch[i::7] for i in range(7))
    qi = pl.program_id(1)
    qs = _split_q(q_ref[...])
    for x in range(2):
        carry_refs[x][...] = jnp.zeros((tq, LANES), F32)
        acc_refs[x][...] = jnp.zeros((tq, LANES), F32)
    assert tk == tq and tk == 2 * TRI
    nc = tk // LANES

    def scores(j):
        kb = k_ref[pl.ds(pl.multiple_of(j * tk, tk), tk), :]
        for x in range(2):
            z_refs[x][...] = _dot_nt(qs[x], kb)

    def gates(chunk_state):
        for x in range(2):
            z_ref, u_ref, lb_ref = z_refs[x], u_refs[x], lb_refs[x]
            for r0 in range(0, tq, STRIP):
                usum = jnp.zeros((STRIP, LANES), F32)
                for c in range(nc):
                    st = chunk_state(r0, c)
                    if st is False:
                        u_ref[r0:r0 + STRIP, c * LANES:(c + 1) * LANES] = jnp.zeros((STRIP, LANES), BF16)
                        lb_ref[r0:r0 + STRIP, c * LANES:(c + 1) * LANES] = jnp.zeros((STRIP, LANES), F32)
                        continue
                    z = _chunk(z_ref, r0, c)
                    sp = jnp.maximum(z, 0.0) + LOG2E * jnp.log(1.0 + jnp.exp2(-jnp.abs(z)))
                    u = sp if st is None else jnp.where(st, sp, 0.0)
                    u_ref[r0:r0 + STRIP, c * LANES:(c + 1) * LANES] = u.astype(BF16)
                    lb_ref[r0:r0 + STRIP, c * LANES:(c + 1) * LANES] = z - sp
                    usum = usum + u
                tot_refs[x][r0:r0 + STRIP, :] = (jnp.sum(usum, axis=1, keepdims=True)
                                                 + jnp.zeros((STRIP, LANES), F32))

    def weights_and_accumulate(j, masked):
        vb = v_ref[pl.ds(pl.multiple_of(j * tk, tk), tk), :]
        tri_stack = tri_ref[...]
        for x in range(2):
            u_ref, lb_ref, a_ref = u_refs[x], lb_refs[x], a_refs[x]
            suffix = (_dot(u_ref[...], tri_stack), _dot(u_ref[:, TRI:], tri_stack[:TRI]))
            carry = carry_refs[x][...]
            for c in range(nc):
                cols = slice(c * LANES, (c + 1) * LANES)
                blk, off = divmod(c * LANES, TRI)
                suf = suffix[blk][:, off:off + LANES]
                a = jnp.exp2(lb_ref[:, cols] - suf - carry)
                if masked:
                    row = lax.broadcasted_iota(jnp.int32, (tq, LANES), 0)
                    col = c * LANES + lax.broadcasted_iota(jnp.int32, (tq, LANES), 1)
                    a = jnp.where(col < row, a, 0.0)
                a_ref[:, cols] = a.astype(BF16)
            acc_refs[x][...] += _dot(a_ref[...], vb)
            carry_refs[x][...] = carry + tot_refs[x][...]

    scores(qi)
    gates(_causal_chunk_state(True))
    scores(jnp.maximum(qi - 1, 0))
    weights_and_accumulate(qi, True)

    def body(i, c):
        j = qi - 1 - i
        gates(_all_visible)
        scores(jnp.maximum(j - 1, 0))
        weights_and_accumulate(j, False)
        return c

    lax.fori_loop(0, qi, body, 0)

    lane = lax.broadcasted_iota(jnp.int32, (1, LANES), 1)
    o_ref[...] = jnp.where(lane < HEAD_DIM, acc_refs[0][...], acc_refs[1][...]).astype(BF16)


def _sb_call(sq, sk, sv, tq, tk):
    S = sq.shape[0]
    npair = sq.shape[1] // LANES
    qspec = pl.BlockSpec((tq, LANES), lambda p, i: (i, p))
    kvspec = pl.BlockSpec((S, LANES), lambda p, i: (0, p))
    tri = jnp.asarray(np.concatenate([np.tril(np.ones((TRI, TRI), np.float32), -1),
                                      np.ones((TRI, TRI), np.float32)]), BF16)
    return pl.pallas_call(
        functools.partial(_sb_kernel, tq, tk),
        grid=(npair, S // tq),
        in_specs=[qspec, kvspec, kvspec, pl.BlockSpec((2 * TRI, TRI), lambda p, i: (0, 0))],
        out_specs=qspec,
        out_shape=jax.ShapeDtypeStruct(sq.shape, BF16),
        scratch_shapes=2 * [pltpu.VMEM((tq, tk), F32), pltpu.VMEM((tq, tk), BF16),
                            pltpu.VMEM((tq, tk), F32), pltpu.VMEM((tq, tk), BF16),
                            pltpu.VMEM((tq, LANES), F32), pltpu.VMEM((tq, LANES), F32),
                            pltpu.VMEM((tq, LANES), F32)],
        compiler_params=_cparams(("parallel", "arbitrary")),
        name="sb_attn",
    )(sq, sk, sv, tri)


def _merge_kernel(hb_ref, h_ref, od_ref, os_ref, of_ref, wg_ref, wud_ref, wus_ref, wuf_ref, wo_ref,
                  g_ref, b_ref, out_ref, outb_ref):
    x = hb_ref[...]
    D = x.shape[1]
    merged = None
    for c, (o_ref, wu_ref) in enumerate(((od_ref, wud_ref), (os_ref, wus_ref), (of_ref, wuf_ref))):
        gate = _sigmoid(_dot(x, wg_ref[:, c * D:(c + 1) * D]))
        term = gate * _dot(o_ref[...], wu_ref[...])
        merged = term if merged is None else merged + term
    y = _dot(merged.astype(BF16), wo_ref[...])
    hn = _layer_norm(ALPHA * h_ref[...] + y, g_ref[...], b_ref[...])
    out_ref[...] = hn
    outb_ref[...] = hn.astype(BF16)


def _merge_call(hb, h, od, osb, of, wg, wud, wus, wuf, wo, g, b, tm):
    S, D = h.shape
    row = lambda a: pl.BlockSpec((tm, a.shape[1]), lambda i: (i, 0))
    full = lambda a: pl.BlockSpec(a.shape, lambda i: (0, 0))
    args = (hb, h, od, osb, of, wg, wud, wus, wuf, wo, g, b)
    in_specs = [row(a) for a in args[:5]] + [full(a) for a in args[5:]]
    return pl.pallas_call(
        _merge_kernel,
        grid=(S // tm,),
        in_specs=in_specs,
        out_specs=[row(h), row(h)],
        out_shape=[jax.ShapeDtypeStruct((S, D), F32), jax.ShapeDtypeStruct((S, D), BF16)],
        compiler_params=_cparams(("parallel",)),
        name="merge",
    )(*args)


def _swiglu_hidden(x, wg, wu):
    gate = _dot(x, wg)
    return (gate * _sigmoid(gate)) * _dot(x, wu)


def _ffn_kernel(hb_ref, h_ref, wg_ref, wu_ref, wd_ref, g_ref, b_ref, out_ref, outb_ref, acc_ref):
    f = pl.program_id(1)

    @pl.when(f == 0)
    def _():
        acc_ref[...] = jnp.zeros(acc_ref.shape, F32)

    hid = _swiglu_hidden(hb_ref[...], wg_ref[...], wu_ref[...])
    acc_ref[...] += _dot(hid.astype(BF16), wd_ref[...])

    @pl.when(f == pl.num_programs(1) - 1)
    def _():
        hn = _layer_norm(ALPHA * h_ref[...] + acc_ref[...], g_ref[...], b_ref[...])
        out_ref[...] = hn
        outb_ref[...] = hn.astype(BF16)


def _ffn_call(hb, h, wg, wu, wd, g, b, tm, tf):
    S, D = h.shape
    F = wg.shape[1]
    row = pl.BlockSpec((tm, D), lambda i, f: (i, 0))
    vec = pl.BlockSpec((1, D), lambda i, f: (0, 0))
    return pl.pallas_call(
        _ffn_kernel,
        grid=(S // tm, F // tf),
        in_specs=[row, row,
                  pl.BlockSpec((D, tf), lambda i, f: (0, f)),
                  pl.BlockSpec((D, tf), lambda i, f: (0, f)),
                  pl.BlockSpec((tf, D), lambda i, f: (f, 0)),
                  vec, vec],
        out_specs=[row, row],
        out_shape=[jax.ShapeDtypeStruct((S, D), F32), jax.ShapeDtypeStruct((S, D), BF16)],
        scratch_shapes=[pltpu.VMEM((tm, D), F32)],
        compiler_params=_cparams(("parallel", "arbitrary")),
        name="ffn",
    )(hb, h, wg, wu, wd, g, b)


MOE_TILE = 512
DMA_UNROLL = 8


def _route_kernel(h_ref, w_ref, b_ref, sel_ref, i1_ref, i2_ref, w1_ref, w2_ref):
    logits = jnp.dot(h_ref[...], w_ref[...], precision=lax.Precision.HIGHEST,
                     preferred_element_type=F32) + b_ref[...]
    lane = lax.broadcasted_iota(jnp.int32, logits.shape, 1)
    logits = jnp.where(lane < N_EXPERTS, logits, NEG)
    v1 = jnp.max(logits, axis=1, keepdims=True)
    i1 = jnp.min(jnp.where(logits == v1, lane, LANES), axis=1, keepdims=True)
    rest = jnp.where(lane == i1, NEG, logits)
    v2 = jnp.max(rest, axis=1, keepdims=True)
    i2 = jnp.min(jnp.where(rest == v2, lane, LANES), axis=1, keepdims=True)
    e2 = jnp.exp(v2 - v1)
    zero = jnp.zeros(logits.shape, F32)
    sel_ref[...] = jnp.where((lane == i1) | (lane == i2), 1.0, 0.0).astype(BF16)
    i1_ref[...] = i1 + jnp.zeros(logits.shape, jnp.int32)
    i2_ref[...] = i2 + jnp.zeros(logits.shape, jnp.int32)
    w1_ref[...] = 1.0 / (1.0 + e2) + zero
    w2_ref[...] = e2 / (1.0 + e2) + zero


def _route_call(h, w_router, b_router, tm):
    S, D = h.shape
    wpad = jnp.zeros((D, LANES), F32).at[:, :N_EXPERTS].set(w_router)
    bpad = jnp.zeros((1, LANES), F32).at[0, :N_EXPERTS].set(b_router)
    row = pl.BlockSpec((tm, LANES), lambda i: (i, 0))
    return pl.pallas_call(
        _route_kernel,
        grid=(S // tm,),
        in_specs=[pl.BlockSpec((tm, D), lambda i: (i, 0)),
                  pl.BlockSpec((D, LANES), lambda i: (0, 0)),
                  pl.BlockSpec((1, LANES), lambda i: (0, 0))],
        out_specs=[row] * 5,
        out_shape=[jax.ShapeDtypeStruct((S, LANES), BF16),
                   jax.ShapeDtypeStruct((S, LANES), jnp.int32),
                   jax.ShapeDtypeStruct((S, LANES), jnp.int32),
                   jax.ShapeDtypeStruct((S, LANES), F32),
                   jax.ShapeDtypeStruct((S, LANES), F32)],
        compiler_params=_cparams(("parallel",)),
        name="route",
    )(h, wpad, bpad)


def _rank_kernel(sel_ref, i1_ref, i2_ref, tril_ref, rank_ref, cnt_ref):
    @pl.when(pl.program_id(0) == 0)
    def _():
        cnt_ref[...] = jnp.zeros(cnt_ref.shape, F32)

    sel = sel_ref[...]
    rank = _dot(tril_ref[...], sel) + cnt_ref[...]
    lane = lax.broadcasted_iota(jnp.int32, rank.shape, 1)
    r1 = jnp.sum(jnp.where(lane == i1_ref[...], rank, 0.0), axis=1, keepdims=True)
    r2 = jnp.sum(jnp.where(lane == i2_ref[...], rank, 0.0), axis=1, keepdims=True)
    rank_ref[...] = jnp.where(lane == 0, r1, jnp.where(lane == 1, r2, 0.0))
    cnt_ref[...] += jnp.sum(sel.astype(F32), axis=0, keepdims=True)


def _rank_call(sel, i1b, i2b, tb):
    S = sel.shape[0]
    tril = jnp.asarray(np.tril(np.ones((tb, tb), np.float32), -1), BF16)
    row = pl.BlockSpec((tb, LANES), lambda i: (i, 0))
    return pl.pallas_call(
        _rank_kernel,
        grid=(S // tb,),
        in_specs=[row, row, row, pl.BlockSpec((tb, tb), lambda i: (0, 0))],
        out_specs=[row, pl.BlockSpec((1, LANES), lambda i: (0, 0))],
        out_shape=[jax.ShapeDtypeStruct((S, LANES), F32), jax.ShapeDtypeStruct((1, LANES), F32)],
        compiler_params=_cparams(("arbitrary",)),
        name="rank",
    )(sel, i1b, i2b, tril)


def _row_copy(src_ref, s, dst_ref, d, sem):
    return pltpu.make_async_copy(src_ref.at[pl.ds(s, 1)], dst_ref.at[pl.ds(d, 1)], sem)


def _dispatch_kernel(tb, n_tok, dest_ref, h_ref, xs_in_hbm, xs_hbm, sem):
    del xs_in_hbm
    base = pl.program_id(0) * tb

    def issue(t, c):
        _row_copy(h_ref, t, xs_hbm, dest_ref[base + t], sem).start(priority=0)
        _row_copy(h_ref, t, xs_hbm, dest_ref[n_tok + base + t], sem).start(priority=1)
        return c

    lax.fori_loop(0, tb, issue, 0, unroll=DMA_UNROLL)

    def drain(t, c):
        _row_copy(h_ref, 0, xs_hbm, 0, sem).wait()
        _row_copy(h_ref, 0, xs_hbm, 0, sem).wait()
        return c

    lax.fori_loop(0, tb, drain, 0, unroll=DMA_UNROLL)


def _dispatch_call(dest, h, n_rows, tb):
    S, D = h.shape
    xs0 = jnp.zeros((n_rows, D), F32)
    return pl.pallas_call(
        functools.partial(_dispatch_kernel, tb, S),
        grid_spec=pltpu.PrefetchScalarGridSpec(
            num_scalar_prefetch=1,
            grid=(S // tb,),
            in_specs=[pl.BlockSpec((tb, D), lambda i, d: (i, 0)), pl.BlockSpec(memory_space=pl.ANY)],
            out_specs=pl.BlockSpec(memory_space=pl.ANY),
            scratch_shapes=[pltpu.SemaphoreType.DMA],
        ),
        out_shape=jax.ShapeDtypeStruct((n_rows, D), F32),
        input_output_aliases={2: 0},
        compiler_params=_cparams(("arbitrary",)),
        name="dispatch",
    )(dest, h, xs0)


def _experts_kernel(te_ref, nu_ref, xs_ref, wg_ref, wu_ref, wd_ref, ys_ref, xb_ref, acc_ref):
    i = pl.program_id(0)
    f = pl.program_id(1)
    last_f = pl.num_programs(1) - 1
    used = i < nu_ref[0]

    @pl.when(used & (f == 0))
    def _():
        xb_ref[...] = xs_ref[...].astype(BF16)

    @pl.when(used)
    def _():
        hid = _swiglu_hidden(xb_ref[...], wg_ref[0], wu_ref[0])
        part = _dot(hid.astype(BF16), wd_ref[0])

        @pl.when(f == 0)
        def _():
            acc_ref[...] = part

        @pl.when(f > 0)
        def _():
            acc_ref[...] += part

    @pl.when(used & (f == last_f))
    def _():
        ys_ref[...] = acc_ref[...]

    @pl.when(jnp.logical_not(used) & (f == last_f))
    def _():
        ys_ref[...] = jnp.zeros(ys_ref.shape, F32)


def _experts_call(tile_expert, n_used, xs, wg, wu, wd, tmx, tf):
    P, D = xs.shape
    F = wg.shape[2]
    nf = F // tf

    def fsel(i, f, nu):
        return jnp.where(i < nu[0], f, nf - 1)

    return pl.pallas_call(
        _experts_kernel,
        grid_spec=pltpu.PrefetchScalarGridSpec(
            num_scalar_prefetch=2,
            grid=(P // tmx, nf),
            in_specs=[pl.BlockSpec((tmx, D), lambda i, f, te, nu: (i, 0)),
                      pl.BlockSpec((1, D, tf), lambda i, f, te, nu: (te[i], 0, fsel(i, f, nu))),
                      pl.BlockSpec((1, D, tf), lambda i, f, te, nu: (te[i], 0, fsel(i, f, nu))),
                      pl.BlockSpec((1, tf, D), lambda i, f, te, nu: (te[i], fsel(i, f, nu), 0))],
            out_specs=pl.BlockSpec((tmx, D), lambda i, f, te, nu: (i, 0)),
            scratch_shapes=[pltpu.VMEM((tmx, D), BF16), pltpu.VMEM((tmx, D), F32)],
        ),
        out_shape=jax.ShapeDtypeStruct((P, D), F32),
        compiler_params=_cparams(("arbitrary", "arbitrary")),
        name="experts",
    )(tile_expert, n_used, xs, wg, wu, wd)


def _combine_kernel(tm, n_tok, dest_ref, ys_hbm, h_ref, w1_ref, w2_ref, g_ref, b_ref,
                    out_ref, outb_ref, buf_ref, sem):
    i = pl.program_id(0)
    slot = i % 2

    def gather(tile, slot_, start):
        base = tile * tm

        def one(t, c):
            for k in range(2):
                src_row = dest_ref[k * n_tok + base + t] if start else 0
                cp = pltpu.make_async_copy(ys_hbm.at[pl.ds(src_row, 1)],
                                           buf_ref.at[slot_, k, pl.ds(t, 1)], sem.at[slot_])
                if start:
                    cp.start(priority=k)
                else:
                    cp.wait()
            return c

        lax.fori_loop(0, tm, one, 0, unroll=DMA_UNROLL)

    @pl.when(i == 0)
    def _():
        gather(0, 0, True)

    @pl.when(i + 1 < pl.num_programs(0))
    def _():
        gather(i + 1, 1 - slot, True)

    gather(i, slot, False)

    w1 = w1_ref[...]
    w2 = w2_ref[...]
    y = jnp.concatenate(
        [w1 * buf_ref[slot, 0, :, c * LANES:(c + 1) * LANES]
         + w2 * buf_ref[slot, 1, :, c * LANES:(c + 1) * LANES]
         for c in range(h_ref.shape[1] // LANES)], axis=1)
    hn = _layer_norm(ALPHA * h_ref[...] + y, g_ref[...], b_ref[...])
    out_ref[...] = hn
    outb_ref[...] = hn.astype(BF16)


def _combine_call(dest, ys, h, w1b, w2b, g, b, tm):
    S, D = h.shape
    row = pl.BlockSpec((tm, D), lambda i, d: (i, 0))
    lrow = pl.BlockSpec((tm, LANES), lambda i, d: (i, 0))
    vec = pl.BlockSpec((1, D), lambda i, d: (0, 0))
    return pl.pallas_call(
        functools.partial(_combine_kernel, tm, S),
        grid_spec=pltpu.PrefetchScalarGridSpec(
            num_scalar_prefetch=1,
            grid=(S // tm,),
            in_specs=[pl.BlockSpec(memory_space=pl.ANY), row, lrow, lrow, vec, vec],
            out_specs=[row, row],
            scratch_shapes=[pltpu.VMEM((2, 2, tm, D), F32), pltpu.SemaphoreType.DMA((2,))],
        ),
        out_shape=[jax.ShapeDtypeStruct((S, D), F32), jax.ShapeDtypeStruct((S, D), BF16)],
        compiler_params=_cparams(("arbitrary",)),
        name="combine",
    )(dest, ys, h, w1b, w2b, g, b)


def _moe_call(h, w_router, b_router, wg, wu, wd, g, b, tm, tf):
    S, D = h.shape
    tmx = MOE_TILE
    sel, i1b, i2b, w1b, w2b = _route_call(h, w_router, b_router, tm)
    ranks, counts = _rank_call(sel, i1b, i2b, min(256, S))
    cnt = counts[0, :N_EXPERTS].astype(jnp.int32)
    padded = ((cnt + tmx - 1) // tmx) * tmx
    ends = jnp.cumsum(padded)
    off = ends - padded
    dest = jnp.concatenate([off[i1b[:, 0]] + ranks[:, 0].astype(jnp.int32),
                            off[i2b[:, 0]] + ranks[:, 1].astype(jnp.int32)])
    n_tiles = (2 * S) // tmx + N_EXPERTS
    tile_start = jnp.arange(n_tiles, dtype=jnp.int32) * tmx
    tile_expert = jnp.minimum(jnp.sum(tile_start[:, None] >= ends[None, :], axis=1),
                              N_EXPERTS - 1).astype(jnp.int32)
    n_used = (ends[-1] // tmx).astype(jnp.int32).reshape(1)
    xs = _dispatch_call(dest, h, n_tiles * tmx, tm)
    ys = _experts_call(tile_expert, n_used, xs, wg, wu, wd, tmx, tf)
    return _combine_call(dest, ys, h, w1b, w2b, g, b, min(256, S))


def _rope_tables(S):
    half = HEAD_DIM // 2
    pos = jnp.arange(S, dtype=F32)
    inv = ROPE_THETA ** (-jnp.arange(half, dtype=F32) / half)
    ang = pos[:, None] * inv[None, :]
    cos, sin = jnp.cos(ang), jnp.sin(ang)
    reps = LANES // HEAD_DIM
    cos_t = jnp.tile(jnp.concatenate([cos, cos], axis=1), (1, reps))
    sin_t = jnp.tile(jnp.concatenate([-sin, sin], axis=1), (1, reps))
    return cos_t, sin_t


def _pick(S, pref):
    t = min(pref, S)
    assert S % t == 0
    return t


def _forward(x, ln_in_g, ln_in_b, w_in, b_forget, lam_q1, lam_k1, lam_q2, lam_k2,
             diff_norm_g, w_up_diff, w_up_sb, w_up_fox, w_out, ln_mix_g, ln_mix_b,
             ln_ffn_g, ln_ffn_b, w_gate_dense, w_up_dense, w_down_dense, w_router,
             b_router, w_gate_moe, w_up_moe, w_down_moe):
    B, S, D = x.shape
    assert B == 1 and D == D_MODEL and S % LANES == 0
    depth = w_in.shape[0]
    tm = _pick(S, 512)
    tq = _pick(S, 512)
    tm_ffn = _pick(S, 1024)
    tf = D_FF // 7
    tf_moe = D_FF // 2
    vec = lambda a: a.reshape(1, -1).astype(F32)

    cos_t, sin_t = _rope_tables(S)
    h, hb = _ln_call(x.reshape(S, D), ln_in_g, ln_in_b, tm)
    w_atts, w_forgets, w_gate_logits = _wsplit_call(w_in, 128)
    for l in range(depth):
        w_att, w_forget, w_gate_logit = w_atts[l], w_forgets[l], w_gate_logits[l]
        fbias = jnp.pad(b_forget[l].astype(F32), (0, LANES - N_HEADS_FOX)).reshape(1, LANES)
        dq, dk, dv, sq, sk, sv, fq, fk, fv, logf = _proj_call(hb, w_att, w_forget, cos_t, sin_t,
                                                              fbias, tm)

        lf = logf[:, :N_HEADS_FOX].T.reshape(N_HEADS_FOX, S // LANES, LANES)
        fcum = _fcum_call(lf)

        lam_init = 0.8 - 0.6 * math.exp(-0.3 * l)
        lamv = jnp.stack([lam_q1[l], lam_k1[l], lam_q2[l], lam_k2[l]]).astype(F32)
        o_diff = _diff_call(dq, dk, dv, lamv, vec(diff_norm_g[l]), lam_init, tq)
        o_sb = _sb_call(sq, sk, sv, tq, tq)
        o_fox = _fox_call(fq, fk, fv, fcum, tq)

        h, hb = _merge_call(hb, h, o_diff, o_sb, o_fox, w_gate_logit,
                            w_up_diff[l].astype(BF16), w_up_sb[l].astype(BF16),
                            w_up_fox[l].astype(BF16), w_out[l].astype(BF16),
                            vec(ln_mix_g[l]), vec(ln_mix_b[l]), tm)
        j = l // 2
        if l % 2 == 0:
            h, hb = _ffn_call(hb, h, w_gate_dense[j].astype(BF16), w_up_dense[j].astype(BF16),
                              w_down_dense[j].astype(BF16), vec(ln_ffn_g[l]), vec(ln_ffn_b[l]),
                              tm_ffn, tf)
        else:
            h, hb = _moe_call(h, w_router[j], b_router[j], w_gate_moe[j].astype(BF16),
                              w_up_moe[j].astype(BF16), w_down_moe[j].astype(BF16),
                              vec(ln_ffn_g[l]), vec(ln_ffn_b[l]), tm, tf_moe)
    return h.reshape(B, S, D)


def kernel(x, ln_in_g, ln_in_b, w_in, b_forget, lam_q1, lam_k1, lam_q2, lam_k2, diff_norm_g,
           w_up_diff, w_up_sb, w_up_fox, w_out, ln_mix_g, ln_mix_b, ln_ffn_g, ln_ffn_b,
           w_gate_dense, w_up_dense, w_down_dense, w_router, b_router, w_gate_moe, w_up_moe,
           w_down_moe):
    return _forward(x, ln_in_g, ln_in_b, w_in, b_forget, lam_q1, lam_k1, lam_q2, lam_k2,
                    diff_norm_g, w_up_diff, w_up_sb, w_up_fox, w_out, ln_mix_g, ln_mix_b,
                    ln_ffn_g, ln_ffn_b, w_gate_dense, w_up_dense, w_down_dense, w_router,
                    b_router, w_gate_moe, w_up_moe, w_down_moe)
```

```python
import functools
import math

import jax
import jax.numpy as jnp
import numpy as np
from jax import lax
from jax.experimental import pallas as pl
from jax.experimental.pallas import tpu as pltpu

D_MODEL = 1024
DEPTH = 2
CHUNK = 64
HEAD_DIM = 64
N_HEADS_DIFF = 4
N_HEADS_SB = 4
N_HEADS_FOX = 4
WIDTH_DIFF = N_HEADS_DIFF * 2 * HEAD_DIM
WIDTH_SB = N_HEADS_SB * HEAD_DIM
WIDTH_FOX = N_HEADS_FOX * HEAD_DIM
ROPE_THETA = 10000.0
N_EXPERTS = 8
D_FF = 3584
ALPHA = (2 * DEPTH) ** 0.25
Q_SCALE = HEAD_DIM ** -0.5
LOG2E = 1.4426950408889634
N_ATT = 3 * WIDTH_DIFF + 3 * WIDTH_SB + 3 * WIDTH_FOX
LANES = 128
NEG = -1e30

F32 = jnp.float32
BF16 = jnp.bfloat16

VMEM_LIMIT = 56 * 1024 * 1024


def _cparams(sem):
    return pltpu.CompilerParams(dimension_semantics=sem, vmem_limit_bytes=VMEM_LIMIT)


def _layer_norm(z, g, b, eps=1e-5):
    mu = jnp.mean(z, axis=-1, keepdims=True)
    zc = z - mu
    var = jnp.mean(zc * zc, axis=-1, keepdims=True)
    return zc * lax.rsqrt(var + eps) * g + b


def _sigmoid(x):
    return 1.0 / (1.0 + jnp.exp(-x))


def _softplus(x):
    return jnp.maximum(x, 0.0) + jnp.log1p(jnp.exp(-jnp.abs(x)))


def _dot(a, b):
    return jnp.dot(a, b, preferred_element_type=F32)


def _dot_nt(a, b):
    return lax.dot_general(a, b, (((1,), (1,)), ((), ())), preferred_element_type=F32)


def _split3(x):
    x1 = x.astype(BF16)
    r = x - x1.astype(F32)
    x2 = r.astype(BF16)
    x3 = (r - x2.astype(F32)).astype(BF16)
    return x1, x2, x3


def _ln_kernel(x_ref, g_ref, b_ref, h_ref, hb_ref):
    h = _layer_norm(x_ref[...], g_ref[...], b_ref[...])
    h_ref[...] = h
    hb_ref[...] = h.astype(BF16)


def _ln_call(x, g, b, tm):
    S, D = x.shape
    row = pl.BlockSpec((tm, D), lambda i: (i, 0))
    vec = pl.BlockSpec((1, D), lambda i: (0, 0))
    return pl.pallas_call(
        _ln_kernel,
        grid=(S // tm,),
        in_specs=[row, vec, vec],
        out_specs=[row, row],
        out_shape=[jax.ShapeDtypeStruct((S, D), F32), jax.ShapeDtypeStruct((S, D), BF16)],
        compiler_params=_cparams(("parallel",)),
        name="ln_in",
    )(x, g.reshape(1, D), b.reshape(1, D))


def _wsplit_kernel(w_ref, att_ref, fgt_ref, gate_ref):
    att_ref[0] = w_ref[0, :, :N_ATT].astype(BF16)
    lane = lax.broadcasted_iota(jnp.int32, (1, LANES), 1)
    fgt_ref[0] = jnp.where(lane < N_HEADS_FOX, w_ref[0, :, N_ATT:N_ATT + LANES], 0.0).astype(BF16)
    gate_ref[0] = w_ref[0, :, N_ATT + N_HEADS_FOX:].astype(BF16)


def _wsplit_call(w_in, tr):
    depth, D, n_in = w_in.shape
    n_gate = n_in - N_ATT - N_HEADS_FOX
    spec = lambda n: pl.BlockSpec((1, tr, n), lambda l, i: (l, i, 0))
    return pl.pallas_call(
        _wsplit_kernel,
        grid=(depth, D // tr),
        in_specs=[spec(n_in)],
        out_specs=[spec(N_ATT), spec(LANES), spec(n_gate)],
        out_shape=[jax.ShapeDtypeStruct((depth, D, N_ATT), BF16),
                   jax.ShapeDtypeStruct((depth, D, LANES), BF16),
                   jax.ShapeDtypeStruct((depth, D, n_gate), BF16)],
        compiler_params=_cparams(("parallel", "parallel")),
        name="wsplit",
    )(w_in)


def _proj_kernel(hb_ref, w_ref, wf_ref, cos_ref, sin_ref, fb_ref,
                 dq_ref, dk_ref, dv_ref, sq_ref, sk_ref, sv_ref, fq_ref, fk_ref, fv_ref, lf_ref):
    x = hb_ref[...]
    cos = cos_ref[...]
    sin = sin_ref[...]
    lane = lax.broadcasted_iota(jnp.int32, (1, LANES), 1)
    lower = (lane & (HEAD_DIM // 2)) == 0

    def mm(c0, c1):
        return _dot(x, w_ref[:, c0:c1])

    def rope_store(y, out_ref, scale):
        for g in range(y.shape[1] // LANES):
            yg = y[:, g * LANES:(g + 1) * LANES]
            partner = jnp.where(lower, pltpu.roll(yg, LANES - HEAD_DIM // 2, 1),
                                pltpu.roll(yg, HEAD_DIM // 2, 1))
            r = yg * cos + partner * sin
            if scale != 1.0:
                r = r * scale
            out_ref[:, g * LANES:(g + 1) * LANES] = r.astype(BF16)

    rope_store(mm(0, 512), dq_ref, Q_SCALE * LOG2E)
    rope_store(mm(512, 1024), dk_ref, 1.0)
    dv_ref[...] = mm(1024, 1536).astype(BF16)
    sq_ref[...] = (mm(1536, 1792) * (Q_SCALE * LOG2E)).astype(BF16)
    sk_ref[...] = mm(1792, 2048).astype(BF16)
    sv_ref[...] = mm(2048, 2304).astype(BF16)
    fq_ref[...] = (mm(2304, 2560) * (Q_SCALE * LOG2E)).astype(BF16)
    fk_ref[...] = mm(2560, 2816).astype(BF16)
    fv_ref[...] = mm(2816, 3072).astype(BF16)
    lf_ref[...] = -_softplus(-(_dot(x, wf_ref[...]) + fb_ref[...]))


def _proj_call(hb, w_att, w_forget, cos, sin, fbias, tm):
    S, D = hb.shape
    row = lambda n: pl.BlockSpec((tm, n), lambda i: (i, 0))
    full = lambda a: pl.BlockSpec(a.shape, lambda i: (0, 0))
    widths = [512, 512, 512, 256, 256, 256, 256, 256, 256]
    out_shape = [jax.ShapeDtypeStruct((S, n), BF16) for n in widths]
    out_shape.append(jax.ShapeDtypeStruct((S, LANES), F32))
    return pl.pallas_call(
        _proj_kernel,
        grid=(S // tm,),
        in_specs=[row(D), full(w_att), full(w_forget), row(LANES), row(LANES), full(fbias)],
        out_specs=[row(n) for n in widths] + [row(LANES)],
        out_shape=out_shape,
        compiler_params=_cparams(("parallel",)),
        name="proj",
    )(hb, w_att, w_forget, cos, sin, fbias)


def _fcum_kernel(lf_ref, f_ref):
    nh, R, _ = lf_ref.shape
    r0 = lax.broadcasted_iota(jnp.int32, (LANES, LANES), 0)
    c0 = lax.broadcasted_iota(jnp.int32, (LANES, LANES), 1)
    upper = jnp.where(r0 <= c0, 1.0, 0.0).astype(BF16)
    ones = jnp.ones((LANES, LANES), BF16)
    r1 = lax.broadcasted_iota(jnp.int32, (R, R), 0)
    c1 = lax.broadcasted_iota(jnp.int32, (R, R), 1)
    below = jnp.where(c1 < r1, 1.0, 0.0).astype(BF16)
    for h in range(nh):
        parts = _split3(lf_ref[h])
        within = sum(_dot(p, upper) for p in parts)
        totals = sum(_dot(p, ones) for p in parts)
        offs = sum(_dot(below, t) for t in _split3(totals))
        f_ref[h] = within + offs


def _fcum_call(lf):
    return pl.pallas_call(
        _fcum_kernel,
        out_shape=jax.ShapeDtypeStruct(lf.shape, F32),
        name="fcum",
    )(lf)


def _tile_pos(qi, j, tq, tk):
    row = qi * tq + lax.broadcasted_iota(jnp.int32, (tq, tk), 0)
    col = j * tk + lax.broadcasted_iota(jnp.int32, (tq, tk), 1)
    return row, col


def _split_q(q):
    lane = lax.broadcasted_iota(jnp.int32, (1, LANES), 1)
    lo = lane < HEAD_DIM
    zero = jnp.zeros_like(q)
    return jnp.where(lo, q, zero), jnp.where(lo, zero, q)


STRIP = 16
TRI = 256
PACE_LAG = 128


def _chunk(ref, r0, c):
    return ref[r0:r0 + STRIP, c * LANES:(c + 1) * LANES]


def _zero_after(x):
    half = jnp.uint32(16)
    bits = lax.shift_right_logical(lax.shift_right_logical(pltpu.bitcast(x, jnp.uint32), half), half)
    return pltpu.bitcast(bits, F32)


def _softmax_tile(s_ref, p_ref, m_ref, l_ref, chunk_state, pace, finish_strip):
    tq, tk = s_ref.shape
    for r0 in range(0, tq, STRIP):
        states = [chunk_state(r0, c) for c in range(tk // LANES)]
        sc = []
        for c, st in enumerate(states):
            if st is False:
                sc.append(None)
            elif st is None:
                sc.append(_chunk(s_ref, r0, c))
            else:
                sc.append(jnp.where(st, _chunk(s_ref, r0, c), NEG))
        live = [s for s in sc if s is not None]
        m_prev = m_ref[r0:r0 + STRIP, :]
        zero = pace(r0)
        if zero is not None:
            m_prev = m_prev + zero
        if live:
            mx = functools.reduce(jnp.maximum, live)
            m_new = jnp.maximum(m_prev, jnp.max(mx, axis=1, keepdims=True))
        else:
            m_new = m_prev
        alpha = jnp.exp2(m_prev - m_new)
        psum = jnp.zeros((STRIP, LANES), F32)
        for c, s in enumerate(sc):
            if s is None:
                p_ref[r0:r0 + STRIP, c * LANES:(c + 1) * LANES] = jnp.zeros((STRIP, LANES), BF16)
            else:
                p = jnp.exp2(s - m_new)
                psum = psum + p
                p_ref[r0:r0 + STRIP, c * LANES:(c + 1) * LANES] = p.astype(BF16)
        m_ref[r0:r0 + STRIP, :] = m_new
        l_ref[r0:r0 + STRIP, :] = alpha * l_ref[r0:r0 + STRIP, :] + psum
        finish_strip(r0, alpha)


def _all_visible(r0, c):
    return None


def _init_softmax_state(m_refs, l_refs, acc_refs):
    for m_ref, l_ref, acc_ref in zip(m_refs, l_refs, acc_refs):
        m_ref[...] = jnp.full(m_ref.shape, NEG, F32)
        l_ref[...] = jnp.zeros(l_ref.shape, F32)
        acc_ref[...] = jnp.zeros(acc_ref.shape, F32)


def _softmax_scratch(t):
    per_stream = [pltpu.VMEM((t, t), F32), pltpu.VMEM((t, t), F32),
                  pltpu.VMEM((t, t), BF16), pltpu.VMEM((t, t), BF16),
                  pltpu.VMEM((t, LANES), F32), pltpu.VMEM((t, LANES), F32),
                  pltpu.VMEM((t, LANES), F32)]
    return per_stream + per_stream


def _softmax_state_refs(scratch):
    return scratch[4::7], scratch[5::7], scratch[6::7]


def _row_total(l_ref):
    return jnp.sum(l_ref[...], axis=1, keepdims=True)


def _sweep(qi, t, scores, v_ref, scratch, diag_chunk_state):
    s0, s1, p0, p1, m_refs, l_refs, acc_refs = (scratch[i::7] for i in range(7))
    s_bufs, p_bufs = (s0, s1), (p0, p1)

    def step(j, par, chunk_state, prefetch, earlier_pv=None):
        s_next = scores(j + 1) if prefetch else None
        vb = v_ref[pl.ds(pl.multiple_of(j * t, t), t), :]
        pvs = []
        for x in range(2):
            if prefetch:
                s_bufs[1 - par][x][...] = s_next[x]

            def pace(r0, x=x):
                zero = None
                if earlier_pv is not None:
                    zero = _zero_after(earlier_pv[x][r0:r0 + STRIP, :])
                if prefetch and r0 >= PACE_LAG:
                    rows = slice(r0 - PACE_LAG, r0 - PACE_LAG + STRIP)
                    z2 = _zero_after(s_bufs[1 - par][x][rows, :LANES])
                    zero = z2 if zero is None else zero + z2
                return zero

            def finish_strip(r0, alpha, x=x):
                rows = slice(r0, r0 + STRIP)
                acc_refs[x][rows, :] = alpha * acc_refs[x][rows, :]

            _softmax_tile(s_bufs[par][x], p_bufs[par][x], m_refs[x], l_refs[x], chunk_state,
                          pace, finish_strip)
            pv = _dot(p_bufs[par][x][...], vb)
            acc_refs[x][...] += pv
            pvs.append(pv)
        return pvs

    first = scores(0)
    for x in range(2):
        s0[x][...] = first[x]

    def pair(i, c):
        pvs = step(2 * i, 0, _all_visible, True)
        step(2 * i + 1, 1, _all_visible, True, pvs)
        return c

    lax.fori_loop(0, qi // 2, pair, 0)

    @pl.when(qi % 2 == 1)
    def _():
        pvs = step(qi - 1, 0, _all_visible, True)
        step(qi, 1, diag_chunk_state, False, pvs)

    @pl.when(qi % 2 == 0)
    def _():
        step(qi, 0, diag_chunk_state, False)


def _diff_chunk_state(col0):
    def state(r0, c):
        q_chunk = r0 // CHUNK
        k_lo = (col0 + c * LANES) // CHUNK
        k_hi = (col0 + c * LANES + LANES - 1) // CHUNK
        if k_hi <= q_chunk:
            return None
        if k_lo > q_chunk:
            return False
        return lax.broadcasted_iota(jnp.int32, (STRIP, LANES), 1) < CHUNK
    return state


def _diff_kernel(t, lam_init, q_ref, k_ref, v_ref, lamv_ref, g_ref, o_ref, *scratch):
    m_refs, l_refs, acc_refs = _softmax_state_refs(scratch)
    qi = pl.program_id(1)
    qs = _split_q(q_ref[...])
    _init_softmax_state(m_refs, l_refs, acc_refs)

    def scores(j):
        kb = k_ref[pl.ds(pl.multiple_of(j * t, t), t), :]
        return [_dot_nt(qs[x], kb) for x in range(2)]

    _sweep(qi, t, scores, v_ref, scratch, _diff_chunk_state(0))

    lv = lamv_ref[...]
    lam = (jnp.exp(jnp.sum(lv[0:1] * lv[1:2], axis=1, keepdims=True))
           - jnp.exp(jnp.sum(lv[2:3] * lv[3:4], axis=1, keepdims=True)) + lam_init)
    o = (acc_refs[0][...] / _row_total(l_refs[0])
         - lam * (acc_refs[1][...] / _row_total(l_refs[1])))
    o = o * lax.rsqrt(jnp.mean(o * o, axis=-1, keepdims=True) + 1e-6) * g_ref[...]
    o_ref[...] = (o * (1.0 - lam_init)).astype(BF16)


def _diff_call(dq, dk, dv, lamv, gnorm, lam_init, tq):
    S = dq.shape[0]
    nh = dq.shape[1] // LANES
    qspec = pl.BlockSpec((tq, LANES), lambda h, i: (i, h))
    kvspec = pl.BlockSpec((S, LANES), lambda h, i: (0, h))
    full = lambda a: pl.BlockSpec(a.shape, lambda h, i: (0, 0))
    return pl.pallas_call(
        functools.partial(_diff_kernel, tq, lam_init),
        grid=(nh, S // tq),
        in_specs=[qspec, kvspec, kvspec, full(lamv), full(gnorm)],
        out_specs=qspec,
        out_shape=jax.ShapeDtypeStruct(dq.shape, BF16),
        scratch_shapes=_softmax_scratch(tq),
        compiler_params=_cparams(("parallel", "arbitrary")),
        name="diff_attn",
    )(dq, dk, dv, lamv, gnorm)


def _causal_chunk_state(strict, col0=0):
    def state(r0, c):
        k_lo, k_hi = col0 + c * LANES, col0 + c * LANES + LANES - 1
        q_lo, q_hi = r0, r0 + STRIP - 1
        if k_hi < q_lo or (not strict and k_hi <= q_lo):
            return None
        if k_lo > q_hi or (strict and k_lo >= q_hi):
            return False
        row = r0 + lax.broadcasted_iota(jnp.int32, (STRIP, LANES), 0)
        col = k_lo + lax.broadcasted_iota(jnp.int32, (STRIP, LANES), 1)
        return col < row if strict else col <= row
    return state


def _fox_kernel(t, q_ref, k_ref, v_ref, f_ref, o_ref, *scratch):
    m_refs, l_refs, acc_refs = _softmax_state_refs(scratch)
    qi = pl.program_id(1)
    qs = _split_q(q_ref[...])
    _init_softmax_state(m_refs, l_refs, acc_refs)
    f0 = [f_ref[x, qi][:, 0:1] for x in range(2)]

    def scores(j):
        kb = k_ref[pl.ds(pl.multiple_of(j * t, t), t), :]
        return [_dot_nt(qs[x], kb) + (f0[x] - f_ref[x, j]) * LOG2E for x in range(2)]

    _sweep(qi, t, scores, v_ref, scratch, _causal_chunk_state(strict=False))

    lane = lax.broadcasted_iota(jnp.int32, (1, LANES), 1)
    o = jnp.where(lane < HEAD_DIM, acc_refs[0][...] / _row_total(l_refs[0]),
                  acc_refs[1][...] / _row_total(l_refs[1]))
    o_ref[...] = o.astype(BF16)


def _fox_call(fq, fk, fv, fcum, tq):
    S = fq.shape[0]
    npair = fq.shape[1] // LANES
    qspec = pl.BlockSpec((tq, LANES), lambda p, i: (i, p))
    kvspec = pl.BlockSpec((S, LANES), lambda p, i: (0, p))
    nh = fcum.shape[0]
    fspec = pl.BlockSpec((2, S // tq, 1, tq), lambda p, i: (p, 0, 0, 0))
    return pl.pallas_call(
        functools.partial(_fox_kernel, tq),
        grid=(npair, S // tq),
        in_specs=[qspec, kvspec, kvspec, fspec],
        out_specs=qspec,
        out_shape=jax.ShapeDtypeStruct(fq.shape, BF16),
        scratch_shapes=_softmax_scratch(tq),
        compiler_params=_cparams(("parallel", "arbitrary")),
        name="fox_attn",
    )(fq, fk, fv, fcum.reshape(nh, S // tq, 1, tq))


def _sb_kernel(tq, tk, q_ref, k_ref, v_ref, tri_ref, o_ref, *scratch):
    (z0, z1, u0, u1, lb0, lb1, a0, a1, tot_refs, carry_refs, acc_refs) = (scratch[i::11]
                                                                          for i in range(11))
    z_bufs, u_bufs, lb_bufs, a_bufs = (z0, z1), (u0, u1), (lb0, lb1), (a0, a1)
    qi = pl.program_id(1)
    qs = _split_q(q_ref[...])
    for x in range(2):
        carry_refs[x][...] = jnp.zeros((tq, LANES), F32)
        acc_refs[x][...] = jnp.zeros((tq, LANES), F32)
    assert tk == tq and tk == 2 * TRI
    nc = tk // LANES

    def scores(j, par):
        kb = k_ref[pl.ds(pl.multiple_of(j * tk, tk), tk), :]
        for x in range(2):
            z_bufs[par][x][...] = _dot_nt(qs[x], kb)

    def gates(par, chunk_state, earlier_pv):
        for x in range(2):
            z_ref, u_ref, lb_ref = z_bufs[par][x], u_bufs[par][x], lb_bufs[par][x]
            for r0 in range(0, tq, STRIP):
                usum = jnp.zeros((STRIP, LANES), F32)
                if earlier_pv is not None:
                    usum = _zero_after(earlier_pv[x][r0:r0 + STRIP, :])
                if r0 >= PACE_LAG:
                    rows = slice(r0 - PACE_LAG, r0 - PACE_LAG + STRIP)
                    usum = usum + _zero_after(z_bufs[1 - par][x][rows, :LANES])
                for c in range(nc):
                    st = chunk_state(r0, c)
                    if st is False:
                        u_ref[r0:r0 + STRIP, c * LANES:(c + 1) * LANES] = jnp.zeros((STRIP, LANES), BF16)
                        lb_ref[r0:r0 + STRIP, c * LANES:(c + 1) * LANES] = jnp.zeros((STRIP, LANES), F32)
                        continue
                    z = _chunk(z_ref, r0, c)
                    neg_abs = pltpu.bitcast(pltpu.bitcast(z, jnp.uint32) | jnp.uint32(0x80000000), F32)
                    sp = jnp.maximum(z, 0.0) + jnp.log2(1.0 + jnp.exp2(neg_abs))
                    u = sp if st is None else jnp.where(st, sp, 0.0)
                    u_ref[r0:r0 + STRIP, c * LANES:(c + 1) * LANES] = u.astype(BF16)
                    lb_ref[r0:r0 + STRIP, c * LANES:(c + 1) * LANES] = z - sp
                    usum = usum + u
                tot_refs[x][r0:r0 + STRIP, :] = (jnp.sum(usum, axis=1, keepdims=True)
                                                 + jnp.zeros((STRIP, LANES), F32))

    def weights_and_accumulate(j, par, masked):
        vb = v_ref[pl.ds(pl.multiple_of(j * tk, tk), tk), :]
        tri_stack = tri_ref[...]
        pvs = []
        for x in range(2):
            u_ref, lb_ref, a_ref = u_bufs[par][x], lb_bufs[par][x], a_bufs[par][x]
            suffix = (_dot(u_ref[...], tri_stack), _dot(u_ref[:, TRI:], tri_stack[:TRI]))
            carry = carry_refs[x][...]
            for c in range(nc):
                cols = slice(c * LANES, (c + 1) * LANES)
                blk, off = divmod(c * LANES, TRI)
                suf = suffix[blk][:, off:off + LANES]
                a = jnp.exp2(lb_ref[:, cols] - suf - carry)
                if masked:
                    row = lax.broadcasted_iota(jnp.int32, (tq, LANES), 0)
                    col = c * LANES + lax.broadcasted_iota(jnp.int32, (tq, LANES), 1)
                    a = jnp.where(col < row, a, 0.0)
                a_ref[:, cols] = a.astype(BF16)
            pv = _dot(a_ref[...], vb)
            acc_refs[x][...] += pv
            carry_refs[x][...] = carry + tot_refs[x][...]
            pvs.append(pv)
        return pvs

    def step(j, par, masked, earlier_pv=None):
        scores(jnp.maximum(j - 1, 0), 1 - par)
        gates(par, _causal_chunk_state(True) if masked else _all_visible, earlier_pv)
        return weights_and_accumulate(j, par, masked)

    scores(qi, 0)
    step(qi, 0, True)

    def pair(i, c):
        j = qi - 1 - 2 * i
        pvs = step(j, 1, False)
        step(j - 1, 0, False, pvs)
        return c

    lax.fori_loop(0, qi // 2, pair, 0)

    @pl.when(qi % 2 == 1)
    def _():
        step(0, 1, False)

    lane = lax.broadcasted_iota(jnp.int32, (1, LANES), 1)
    o_ref[...] = jnp.where(lane < HEAD_DIM, acc_refs[0][...], acc_refs[1][...]).astype(BF16)


def _sb_call(sq, sk, sv, tq, tk):
    S = sq.shape[0]
    npair = sq.shape[1] // LANES
    qspec = pl.BlockSpec((tq, LANES), lambda p, i: (i, p))
    kvspec = pl.BlockSpec((S, LANES), lambda p, i: (0, p))
    tri = jnp.asarray(np.concatenate([np.tril(np.ones((TRI, TRI), np.float32), -1),
                                      np.ones((TRI, TRI), np.float32)]), BF16)
    return pl.pallas_call(
        functools.partial(_sb_kernel, tq, tk),
        grid=(npair, S // tq),
        in_specs=[qspec, kvspec, kvspec, pl.BlockSpec((2 * TRI, TRI), lambda p, i: (0, 0))],
        out_specs=qspec,
        out_shape=jax.ShapeDtypeStruct(sq.shape, BF16),
        scratch_shapes=2 * [pltpu.VMEM((tq, tk), F32), pltpu.VMEM((tq, tk), F32),
                            pltpu.VMEM((tq, tk), BF16), pltpu.VMEM((tq, tk), BF16),
                            pltpu.VMEM((tq, tk), F32), pltpu.VMEM((tq, tk), F32),
                            pltpu.VMEM((tq, tk), BF16), pltpu.VMEM((tq, tk), BF16),
                            pltpu.VMEM((tq, LANES), F32), pltpu.VMEM((tq, LANES), F32),
                            pltpu.VMEM((tq, LANES), F32)],
        compiler_params=_cparams(("parallel", "arbitrary")),
        name="sb_attn",
    )(sq, sk, sv, tri)


def _merge_kernel(hb_ref, h_ref, od_ref, os_ref, of_ref, wg_ref, wud_ref, wus_ref, wuf_ref, wo_ref,
                  g_ref, b_ref, out_ref, outb_ref):
    x = hb_ref[...]
    D = x.shape[1]
    merged = None
    for c, (o_ref, wu_ref) in enumerate(((od_ref, wud_ref), (os_ref, wus_ref), (of_ref, wuf_ref))):
        gate = _sigmoid(_dot(x, wg_ref[:, c * D:(c + 1) * D]))
        term = gate * _dot(o_ref[...], wu_ref[...])
        merged = term if merged is None else merged + term
    y = _dot(merged.astype(BF16), wo_ref[...])
    hn = _layer_norm(ALPHA * h_ref[...] + y, g_ref[...], b_ref[...])
    out_ref[...] = hn
    outb_ref[...] = hn.astype(BF16)


def _merge_call(hb, h, od, osb, of, wg, wud, wus, wuf, wo, g, b, tm):
    S, D = h.shape
    row = lambda a: pl.BlockSpec((tm, a.shape[1]), lambda i: (i, 0))
    full = lambda a: pl.BlockSpec(a.shape, lambda i: (0, 0))
    args = (hb, h, od, osb, of, wg, wud, wus, wuf, wo, g, b)
    in_specs = [row(a) for a in args[:5]] + [full(a) for a in args[5:]]
    return pl.pallas_call(
        _merge_kernel,
        grid=(S // tm,),
        in_specs=in_specs,
        out_specs=[row(h), row(h)],
        out_shape=[jax.ShapeDtypeStruct((S, D), F32), jax.ShapeDtypeStruct((S, D), BF16)],
        compiler_params=_cparams(("parallel",)),
        name="merge",
    )(*args)


def _swiglu_hidden(x, wg, wu):
    gate = _dot(x, wg)
    return (gate * _sigmoid(gate)) * _dot(x, wu)


def _ffn_kernel(hb_ref, h_ref, wg_ref, wu_ref, wd_ref, g_ref, b_ref, out_ref, outb_ref, acc_ref):
    f = pl.program_id(1)

    @pl.when(f == 0)
    def _():
        acc_ref[...] = jnp.zeros(acc_ref.shape, F32)

    hid = _swiglu_hidden(hb_ref[...], wg_ref[...], wu_ref[...])
    acc_ref[...] += _dot(hid.astype(BF16), wd_ref[...])

    @pl.when(f == pl.num_programs(1) - 1)
    def _():
        hn = _layer_norm(ALPHA * h_ref[...] + acc_ref[...], g_ref[...], b_ref[...])
        out_ref[...] = hn
        outb_ref[...] = hn.astype(BF16)


def _ffn_call(hb, h, wg, wu, wd, g, b, tm, tf):
    S, D = h.shape
    F = wg.shape[1]
    row = pl.BlockSpec((tm, D), lambda i, f: (i, 0))
    vec = pl.BlockSpec((1, D), lambda i, f: (0, 0))
    return pl.pallas_call(
        _ffn_kernel,
        grid=(S // tm, F // tf),
        in_specs=[row, row,
                  pl.BlockSpec((D, tf), lambda i, f: (0, f)),
                  pl.BlockSpec((D, tf), lambda i, f: (0, f)),
                  pl.BlockSpec((tf, D), lambda i, f: (f, 0)),
                  vec, vec],
        out_specs=[row, row],
        out_shape=[jax.ShapeDtypeStruct((S, D), F32), jax.ShapeDtypeStruct((S, D), BF16)],
        scratch_shapes=[pltpu.VMEM((tm, D), F32)],
        compiler_params=_cparams(("parallel", "arbitrary")),
        name="ffn",
    )(hb, h, wg, wu, wd, g, b)


MOE_TILE = 512
DMA_UNROLL = 8


def _route_kernel(h_ref, w_ref, b_ref, sel_ref, i1_ref, i2_ref, w1_ref, w2_ref):
    logits = jnp.dot(h_ref[...], w_ref[...], precision=lax.Precision.HIGHEST,
                     preferred_element_type=F32) + b_ref[...]
    lane = lax.broadcasted_iota(jnp.int32, logits.shape, 1)
    logits = jnp.where(lane < N_EXPERTS, logits, NEG)
    v1 = jnp.max(logits, axis=1, keepdims=True)
    i1 = jnp.min(jnp.where(logits == v1, lane, LANES), axis=1, keepdims=True)
    rest = jnp.where(lane == i1, NEG, logits)
    v2 = jnp.max(rest, axis=1, keepdims=True)
    i2 = jnp.min(jnp.where(rest == v2, lane, LANES), axis=1, keepdims=True)
    e2 = jnp.exp(v2 - v1)
    zero = jnp.zeros(logits.shape, F32)
    sel_ref[...] = jnp.where((lane == i1) | (lane == i2), 1.0, 0.0).astype(BF16)
    i1_ref[...] = i1 + jnp.zeros(logits.shape, jnp.int32)
    i2_ref[...] = i2 + jnp.zeros(logits.shape, jnp.int32)
    w1_ref[...] = 1.0 / (1.0 + e2) + zero
    w2_ref[...] = e2 / (1.0 + e2) + zero


def _route_call(h, w_router, b_router, tm):
    S, D = h.shape
    wpad = jnp.zeros((D, LANES), F32).at[:, :N_EXPERTS].set(w_router)
    bpad = jnp.zeros((1, LANES), F32).at[0, :N_EXPERTS].set(b_router)
    row = pl.BlockSpec((tm, LANES), lambda i: (i, 0))
    return pl.pallas_call(
        _route_kernel,
        grid=(S // tm,),
        in_specs=[pl.BlockSpec((tm, D), lambda i: (i, 0)),
                  pl.BlockSpec((D, LANES), lambda i: (0, 0)),
                  pl.BlockSpec((1, LANES), lambda i: (0, 0))],
        out_specs=[row] * 5,
        out_shape=[jax.ShapeDtypeStruct((S, LANES), BF16),
                   jax.ShapeDtypeStruct((S, LANES), jnp.int32),
                   jax.ShapeDtypeStruct((S, LANES), jnp.int32),
                   jax.ShapeDtypeStruct((S, LANES), F32),
                   jax.ShapeDtypeStruct((S, LANES), F32)],
        compiler_params=_cparams(("parallel",)),
        name="route",
    )(h, wpad, bpad)


def _rank_kernel(sel_ref, i1_ref, i2_ref, tril_ref, rank_ref, cnt_ref):
    @pl.when(pl.program_id(0) == 0)
    def _():
        cnt_ref[...] = jnp.zeros(cnt_ref.shape, F32)

    sel = sel_ref[...]
    rank = _dot(tril_ref[...], sel) + cnt_ref[...]
    lane = lax.broadcasted_iota(jnp.int32, rank.shape, 1)
    r1 = jnp.sum(jnp.where(lane == i1_ref[...], rank, 0.0), axis=1, keepdims=True)
    r2 = jnp.sum(jnp.where(lane == i2_ref[...], rank, 0.0), axis=1, keepdims=True)
    rank_ref[...] = jnp.where(lane == 0, r1, jnp.where(lane == 1, r2, 0.0))
    cnt_ref[...] += jnp.sum(sel.astype(F32), axis=0, keepdims=True)


def _rank_call(sel, i1b, i2b, tb):
    S = sel.shape[0]
    tril = jnp.asarray(np.tril(np.ones((tb, tb), np.float32), -1), BF16)
    row = pl.BlockSpec((tb, LANES), lambda i: (i, 0))
    return pl.pallas_call(
        _rank_kernel,
        grid=(S // tb,),
        in_specs=[row, row, row, pl.BlockSpec((tb, tb), lambda i: (0, 0))],
        out_specs=[row, pl.BlockSpec((1, LANES), lambda i: (0, 0))],
        out_shape=[jax.ShapeDtypeStruct((S, LANES), F32), jax.ShapeDtypeStruct((1, LANES), F32)],
        compiler_params=_cparams(("arbitrary",)),
        name="rank",
    )(sel, i1b, i2b, tril)


def _row_copy(src_ref, s, dst_ref, d, sem):
    return pltpu.make_async_copy(src_ref.at[pl.ds(s, 1)], dst_ref.at[pl.ds(d, 1)], sem)


def _dispatch_kernel(tb, tmx, n_tok, dest_ref, ends_ref, h_ref, xs_hbm, zero_ref, sem, zsem):
    base = pl.program_id(0) * tb

    @pl.when(pl.program_id(0) == 0)
    def _():
        zero_ref[...] = jnp.zeros(zero_ref.shape, F32)

        def fill(e, wait):
            start = 0 if e == 0 else ends_ref[e - 1]
            end = ends_ref[e]

            @pl.when(end > start)
            def _():
                cp = pltpu.make_async_copy(
                    zero_ref, xs_hbm.at[pl.ds(pl.multiple_of(end - tmx, tmx), tmx)], zsem)
                if wait:
                    cp.wait()
                else:
                    cp.start()

        def fill_tail(k, wait):
            start = ends_ref[N_EXPERTS - 1] + k * tmx

            @pl.when(start < xs_hbm.shape[0])
            def _():
                cp = pltpu.make_async_copy(
                    zero_ref, xs_hbm.at[pl.ds(pl.multiple_of(start, tmx), tmx)], zsem)
                if wait:
                    cp.wait()
                else:
                    cp.start()

        for wait in (False, True):
            for e in range(N_EXPERTS):
                fill(e, wait)
                fill_tail(e, wait)

    def issue(t, c):
        _row_copy(h_ref, t, xs_hbm, dest_ref[base + t], sem).start(priority=0)
        _row_copy(h_ref, t, xs_hbm, dest_ref[n_tok + base + t], sem).start(priority=1)
        return c

    lax.fori_loop(0, tb, issue, 0, unroll=DMA_UNROLL)

    def drain(t, c):
        _row_copy(h_ref, 0, xs_hbm, 0, sem).wait()
        _row_copy(h_ref, 0, xs_hbm, 0, sem).wait()
        return c

    lax.fori_loop(0, tb, drain, 0, unroll=DMA_UNROLL)


def _dispatch_call(dest, ends, h, n_rows, tb, tmx):
    S, D = h.shape
    return pl.pallas_call(
        functools.partial(_dispatch_kernel, tb, tmx, S),
        grid_spec=pltpu.PrefetchScalarGridSpec(
            num_scalar_prefetch=2,
            grid=(S // tb,),
            in_specs=[pl.BlockSpec((tb, D), lambda i, d, e: (i, 0))],
            out_specs=pl.BlockSpec(memory_space=pl.ANY),
            scratch_shapes=[pltpu.VMEM((tmx, D), F32), pltpu.SemaphoreType.DMA,
                            pltpu.SemaphoreType.DMA],
        ),
        out_shape=jax.ShapeDtypeStruct((n_rows, D), F32),
        compiler_params=_cparams(("arbitrary",)),
        name="dispatch",
    )(dest, ends, h)


def _experts_kernel(te_ref, nu_ref, xs_ref, wg_ref, wu_ref, wd_ref, ys_ref, xb_ref, acc_ref):
    i = pl.program_id(0)
    f = pl.program_id(1)
    last_f = pl.num_programs(1) - 1
    used = i < nu_ref[0]

    @pl.when(used & (f == 0))
    def _():
        xb_ref[...] = xs_ref[...].astype(BF16)

    @pl.when(used)
    def _():
        hid = _swiglu_hidden(xb_ref[...], wg_ref[0], wu_ref[0])
        part = _dot(hid.astype(BF16), wd_ref[0])

        @pl.when(f == 0)
        def _():
            acc_ref[...] = part

        @pl.when(f > 0)
        def _():
            acc_ref[...] += part

    @pl.when(used & (f == last_f))
    def _():
        ys_ref[...] = acc_ref[...]

    @pl.when(jnp.logical_not(used) & (f == last_f))
    def _():
        ys_ref[...] = jnp.zeros(ys_ref.shape, F32)


def _experts_call(tile_expert, n_used, xs, wg, wu, wd, tmx, tf):
    P, D = xs.shape
    F = wg.shape[2]
    nf = F // tf

    def fsel(i, f, nu):
        return jnp.where(i < nu[0], f, nf - 1)

    return pl.pallas_call(
        _experts_kernel,
        grid_spec=pltpu.PrefetchScalarGridSpec(
            num_scalar_prefetch=2,
            grid=(P // tmx, nf),
            in_specs=[pl.BlockSpec((tmx, D), lambda i, f, te, nu: (jnp.where(i < nu[0], i, 0), 0)),
                      pl.BlockSpec((1, D, tf), lambda i, f, te, nu: (te[i], 0, fsel(i, f, nu))),
                      pl.BlockSpec((1, D, tf), lambda i, f, te, nu: (te[i], 0, fsel(i, f, nu))),
                      pl.BlockSpec((1, tf, D), lambda i, f, te, nu: (te[i], fsel(i, f, nu), 0))],
            out_specs=pl.BlockSpec((tmx, D), lambda i, f, te, nu: (i, 0)),
            scratch_shapes=[pltpu.VMEM((tmx, D), BF16), pltpu.VMEM((tmx, D), F32)],
        ),
        out_shape=jax.ShapeDtypeStruct((P, D), F32),
        compiler_params=_cparams(("arbitrary", "arbitrary")),
        name="experts",
    )(tile_expert, n_used, xs, wg, wu, wd)


def _combine_kernel(tm, n_tok, dest_ref, ys_hbm, h_ref, w1_ref, w2_ref, g_ref, b_ref,
                    out_ref, outb_ref, buf_ref, sem):
    i = pl.program_id(0)
    slot = i % 2

    def gather(tile, slot_, start):
        base = tile * tm

        def one(t, c):
            for k in range(2):
                src_row = dest_ref[k * n_tok + base + t] if start else 0
                cp = pltpu.make_async_copy(ys_hbm.at[pl.ds(src_row, 1)],
                                           buf_ref.at[slot_, k, pl.ds(t, 1)], sem.at[slot_])
                if start:
                    cp.start(priority=k)
                else:
                    cp.wait()
            return c

        lax.fori_loop(0, tm, one, 0, unroll=DMA_UNROLL)

    @pl.when(i == 0)
    def _():
        gather(0, 0, True)

    @pl.when(i + 1 < pl.num_programs(0))
    def _():
        gather(i + 1, 1 - slot, True)

    gather(i, slot, False)

    w1 = w1_ref[...]
    w2 = w2_ref[...]
    y = jnp.concatenate(
        [w1 * buf_ref[slot, 0, :, c * LANES:(c + 1) * LANES]
         + w2 * buf_ref[slot, 1, :, c * LANES:(c + 1) * LANES]
         for c in range(h_ref.shape[1] // LANES)], axis=1)
    hn = _layer_norm(ALPHA * h_ref[...] + y, g_ref[...], b_ref[...])
    out_ref[...] = hn
    outb_ref[...] = hn.astype(BF16)


def _combine_call(dest, ys, h, w1b, w2b, g, b, tm):
    S, D = h.shape
    row = pl.BlockSpec((tm, D), lambda i, d: (i, 0))
    lrow = pl.BlockSpec((tm, LANES), lambda i, d: (i, 0))
    vec = pl.BlockSpec((1, D), lambda i, d: (0, 0))
    return pl.pallas_call(
        functools.partial(_combine_kernel, tm, S),
        grid_spec=pltpu.PrefetchScalarGridSpec(
            num_scalar_prefetch=1,
            grid=(S // tm,),
            in_specs=[pl.BlockSpec(memory_space=pl.ANY), row, lrow, lrow, vec, vec],
            out_specs=[row, row],
            scratch_shapes=[pltpu.VMEM((2, 2, tm, D), F32), pltpu.SemaphoreType.DMA((2,))],
        ),
        out_shape=[jax.ShapeDtypeStruct((S, D), F32), jax.ShapeDtypeStruct((S, D), BF16)],
        compiler_params=_cparams(("arbitrary",)),
        name="combine",
    )(dest, ys, h, w1b, w2b, g, b)


def _moe_call(h, w_router, b_router, wg, wu, wd, g, b, tm, tf):
    S, D = h.shape
    tmx = MOE_TILE
    sel, i1b, i2b, w1b, w2b = _route_call(h, w_router, b_router, tm)
    ranks, counts = _rank_call(sel, i1b, i2b, min(256, S))
    cnt = counts[0, :N_EXPERTS].astype(jnp.int32)
    padded = ((cnt + tmx - 1) // tmx) * tmx
    ends = jnp.cumsum(padded)
    off = ends - padded
    dest = jnp.concatenate([off[i1b[:, 0]] + ranks[:, 0].astype(jnp.int32),
                            off[i2b[:, 0]] + ranks[:, 1].astype(jnp.int32)])
    n_tiles = (2 * S) // tmx + N_EXPERTS
    tile_start = jnp.arange(n_tiles, dtype=jnp.int32) * tmx
    tile_expert = jnp.minimum(jnp.sum(tile_start[:, None] >= ends[None, :], axis=1),
                              N_EXPERTS - 1).astype(jnp.int32)
    n_used = (ends[-1] // tmx).astype(jnp.int32).reshape(1)
    xs = _dispatch_call(dest, ends.astype(jnp.int32), h, n_tiles * tmx, tm, tmx)
    ys = _experts_call(tile_expert, n_used, xs, wg, wu, wd, tmx, tf)
    return _combine_call(dest, ys, h, w1b, w2b, g, b, min(256, S))


def _rope_tables(S):
    half = HEAD_DIM // 2
    pos = jnp.arange(S, dtype=F32)
    inv = ROPE_THETA ** (-jnp.arange(half, dtype=F32) / half)
    ang = pos[:, None] * inv[None, :]
    cos, sin = jnp.cos(ang), jnp.sin(ang)
    reps = LANES // HEAD_DIM
    cos_t = jnp.tile(jnp.concatenate([cos, cos], axis=1), (1, reps))
    sin_t = jnp.tile(jnp.concatenate([-sin, sin], axis=1), (1, reps))
    return cos_t, sin_t


def _pick(S, pref):
    t = min(pref, S)
    assert S % t == 0
    return t


def _forward(x, ln_in_g, ln_in_b, w_in, b_forget, lam_q1, lam_k1, lam_q2, lam_k2,
             diff_norm_g, w_up_diff, w_up_sb, w_up_fox, w_out, ln_mix_g, ln_mix_b,
             ln_ffn_g, ln_ffn_b, w_gate_dense, w_up_dense, w_down_dense, w_router,
             b_router, w_gate_moe, w_up_moe, w_down_moe):
    B, S, D = x.shape
    assert B == 1 and D == D_MODEL and S % LANES == 0
    depth = w_in.shape[0]
    tm = _pick(S, 512)
    tq = _pick(S, 512)
    tm_ffn = _pick(S, 1024)
    tf = D_FF // 7
    tf_moe = D_FF // 2
    vec = lambda a: a.reshape(1, -1).astype(F32)

    cos_t, sin_t = _rope_tables(S)
    h, hb = _ln_call(x.reshape(S, D), ln_in_g, ln_in_b, tm)
    w_atts, w_forgets, w_gate_logits = _wsplit_call(w_in, 128)
    for l in range(depth):
        w_att, w_forget, w_gate_logit = w_atts[l], w_forgets[l], w_gate_logits[l]
        fbias = jnp.pad(b_forget[l].astype(F32), (0, LANES - N_HEADS_FOX)).reshape(1, LANES)
        dq, dk, dv, sq, sk, sv, fq, fk, fv, logf = _proj_call(hb, w_att, w_forget, cos_t, sin_t,
                                                              fbias, tm)

        lf = logf[:, :N_HEADS_FOX].T.reshape(N_HEADS_FOX, S // LANES, LANES)
        fcum = _fcum_call(lf)

        lam_init = 0.8 - 0.6 * math.exp(-0.3 * l)
        lamv = jnp.stack([lam_q1[l], lam_k1[l], lam_q2[l], lam_k2[l]]).astype(F32)
        o_diff = _diff_call(dq, dk, dv, lamv, vec(diff_norm_g[l]), lam_init, tq)
        o_sb = _sb_call(sq, sk, sv, tq, tq)
        o_fox = _fox_call(fq, fk, fv, fcum, tq)

        h, hb = _merge_call(hb, h, o_diff, o_sb, o_fox, w_gate_logit,
                            w_up_diff[l].astype(BF16), w_up_sb[l].astype(BF16),
                            w_up_fox[l].astype(BF16), w_out[l].astype(BF16),
                            vec(ln_mix_g[l]), vec(ln_mix_b[l]), tm)
        j = l // 2
        if l % 2 == 0:
            h, hb = _ffn_call(hb, h, w_gate_dense[j].astype(BF16), w_up_dense[j].astype(BF16),
                              w_down_dense[j].astype(BF16), vec(ln_ffn_g[l]), vec(ln_ffn_b[l]),
                              tm_ffn, tf)
        else:
            h, hb = _moe_call(h, w_router[j], b_router[j], w_gate_moe[j].astype(BF16),
                              w_up_moe[j].astype(BF16), w_down_moe[j].astype(BF16),
                              vec(ln_ffn_g[l]), vec(ln_ffn_b[l]), tm, tf_moe)
    return h.reshape(B, S, D)


def kernel(x, ln_in_g, ln_in_b, w_in, b_forget, lam_q1, lam_k1, lam_q2, lam_k2, diff_norm_g,
           w_up_diff, w_up_sb, w_up_fox, w_out, ln_mix_g, ln_mix_b, ln_ffn_g, ln_ffn_b,
           w_gate_dense, w_up_dense, w_down_dense, w_router, b_router, w_gate_moe, w_up_moe,
           w_down_moe):
    return _forward(x, ln_in_g, ln_in_b, w_in, b_forget, lam_q1, lam_k1, lam_q2, lam_k2,
                    diff_norm_g, w_up_diff, w_up_sb, w_up_fox, w_out, ln_mix_g, ln_mix_b,
                    ln_ffn_g, ln_ffn_b, w_gate_dense, w_up_dense, w_down_dense, w_router,
                    b_router, w_gate_moe, w_up_moe, w_down_moe)
```

```python
import functools
import math

import jax
import jax.numpy as jnp
import numpy as np
from jax import lax
from jax.experimental import pallas as pl
from jax.experimental.pallas import tpu as pltpu

D_MODEL = 1024
DEPTH = 2
CHUNK = 64
HEAD_DIM = 64
N_HEADS_DIFF = 4
N_HEADS_SB = 4
N_HEADS_FOX = 4
WIDTH_DIFF = N_HEADS_DIFF * 2 * HEAD_DIM
WIDTH_SB = N_HEADS_SB * HEAD_DIM
WIDTH_FOX = N_HEADS_FOX * HEAD_DIM
ROPE_THETA = 10000.0
N_EXPERTS = 8
D_FF = 3584
ALPHA = (2 * DEPTH) ** 0.25
Q_SCALE = HEAD_DIM ** -0.5
LOG2E = 1.4426950408889634
N_ATT = 3 * WIDTH_DIFF + 3 * WIDTH_SB + 3 * WIDTH_FOX
LANES = 128
NEG = -1e30

F32 = jnp.float32
BF16 = jnp.bfloat16

VMEM_LIMIT = 56 * 1024 * 1024


def _cparams(sem):
    return pltpu.CompilerParams(dimension_semantics=sem, vmem_limit_bytes=VMEM_LIMIT)


def _layer_norm(z, g, b, eps=1e-5):
    mu = jnp.mean(z, axis=-1, keepdims=True)
    zc = z - mu
    var = jnp.mean(zc * zc, axis=-1, keepdims=True)
    return zc * lax.rsqrt(var + eps) * g + b


def _sigmoid(x):
    return 1.0 / (1.0 + jnp.exp(-x))


def _softplus(x):
    return jnp.maximum(x, 0.0) + jnp.log1p(jnp.exp(-jnp.abs(x)))


def _dot(a, b):
    return jnp.dot(a, b, preferred_element_type=F32)


def _dot_nt(a, b):
    return lax.dot_general(a, b, (((1,), (1,)), ((), ())), preferred_element_type=F32)


def _split3(x):
    x1 = x.astype(BF16)
    r = x - x1.astype(F32)
    x2 = r.astype(BF16)
    x3 = (r - x2.astype(F32)).astype(BF16)
    return x1, x2, x3


def _ln_kernel(x_ref, g_ref, b_ref, h_ref, hb_ref):
    h = _layer_norm(x_ref[...], g_ref[...], b_ref[...])
    h_ref[...] = h
    hb_ref[...] = h.astype(BF16)


def _ln_call(x, g, b, tm):
    S, D = x.shape
    row = pl.BlockSpec((tm, D), lambda i: (i, 0))
    vec = pl.BlockSpec((1, D), lambda i: (0, 0))
    return pl.pallas_call(
        _ln_kernel,
        grid=(S // tm,),
        in_specs=[row, vec, vec],
        out_specs=[row, row],
        out_shape=[jax.ShapeDtypeStruct((S, D), F32), jax.ShapeDtypeStruct((S, D), BF16)],
        compiler_params=_cparams(("parallel",)),
        name="ln_in",
    )(x, g.reshape(1, D), b.reshape(1, D))


def _wsplit_kernel(w_ref, att_ref, fgt_ref, gate_ref):
    att_ref[0] = w_ref[0, :, :N_ATT].astype(BF16)
    lane = lax.broadcasted_iota(jnp.int32, (1, LANES), 1)
    fgt_ref[0] = jnp.where(lane < N_HEADS_FOX, w_ref[0, :, N_ATT:N_ATT + LANES], 0.0).astype(BF16)
    gate_ref[0] = w_ref[0, :, N_ATT + N_HEADS_FOX:].astype(BF16)


def _wsplit_call(w_in, tr):
    depth, D, n_in = w_in.shape
    n_gate = n_in - N_ATT - N_HEADS_FOX
    spec = lambda n: pl.BlockSpec((1, tr, n), lambda l, i: (l, i, 0))
    return pl.pallas_call(
        _wsplit_kernel,
        grid=(depth, D // tr),
        in_specs=[spec(n_in)],
        out_specs=[spec(N_ATT), spec(LANES), spec(n_gate)],
        out_shape=[jax.ShapeDtypeStruct((depth, D, N_ATT), BF16),
                   jax.ShapeDtypeStruct((depth, D, LANES), BF16),
                   jax.ShapeDtypeStruct((depth, D, n_gate), BF16)],
        compiler_params=_cparams(("parallel", "parallel")),
        name="wsplit",
    )(w_in)


def _proj_kernel(hb_ref, w_ref, wf_ref, cos_ref, sin_ref, fb_ref,
                 dq_ref, dk_ref, dv_ref, sq_ref, sk_ref, sv_ref, fq_ref, fk_ref, fv_ref, lf_ref):
    x = hb_ref[...]
    cos = cos_ref[...]
    sin = sin_ref[...]
    lane = lax.broadcasted_iota(jnp.int32, (1, LANES), 1)
    lower = (lane & (HEAD_DIM // 2)) == 0

    def mm(c0, c1):
        return _dot(x, w_ref[:, c0:c1])

    def rope_store(y, out_ref, scale):
        for g in range(y.shape[1] // LANES):
            yg = y[:, g * LANES:(g + 1) * LANES]
            partner = jnp.where(lower, pltpu.roll(yg, LANES - HEAD_DIM // 2, 1),
                                pltpu.roll(yg, HEAD_DIM // 2, 1))
            r = yg * cos + partner * sin
            if scale != 1.0:
                r = r * scale
            out_ref[:, g * LANES:(g + 1) * LANES] = r.astype(BF16)

    rope_store(mm(0, 512), dq_ref, Q_SCALE * LOG2E)
    rope_store(mm(512, 1024), dk_ref, 1.0)
    dv_ref[...] = mm(1024, 1536).astype(BF16)
    sq_ref[...] = (mm(1536, 1792) * (Q_SCALE * LOG2E)).astype(BF16)
    sk_ref[...] = mm(1792, 2048).astype(BF16)
    sv_ref[...] = mm(2048, 2304).astype(BF16)
    fq_ref[...] = (mm(2304, 2560) * (Q_SCALE * LOG2E)).astype(BF16)
    fk_ref[...] = mm(2560, 2816).astype(BF16)
    fv_ref[...] = mm(2816, 3072).astype(BF16)
    lf_ref[...] = -_softplus(-(_dot(x, wf_ref[...]) + fb_ref[...]))


def _proj_call(hb, w_att, w_forget, cos, sin, fbias, tm):
    S, D = hb.shape
    row = lambda n: pl.BlockSpec((tm, n), lambda i: (i, 0))
    full = lambda a: pl.BlockSpec(a.shape, lambda i: (0, 0))
    widths = [512, 512, 512, 256, 256, 256, 256, 256, 256]
    out_shape = [jax.ShapeDtypeStruct((S, n), BF16) for n in widths]
    out_shape.append(jax.ShapeDtypeStruct((S, LANES), F32))
    return pl.pallas_call(
        _proj_kernel,
        grid=(S // tm,),
        in_specs=[row(D), full(w_att), full(w_forget), row(LANES), row(LANES), full(fbias)],
        out_specs=[row(n) for n in widths] + [row(LANES)],
        out_shape=out_shape,
        compiler_params=_cparams(("parallel",)),
        name="proj",
    )(hb, w_att, w_forget, cos, sin, fbias)


def _fcum_kernel(lf_ref, f_ref):
    nh, R, _ = lf_ref.shape
    r0 = lax.broadcasted_iota(jnp.int32, (LANES, LANES), 0)
    c0 = lax.broadcasted_iota(jnp.int32, (LANES, LANES), 1)
    upper = jnp.where(r0 <= c0, 1.0, 0.0).astype(BF16)
    ones = jnp.ones((LANES, LANES), BF16)
    r1 = lax.broadcasted_iota(jnp.int32, (R, R), 0)
    c1 = lax.broadcasted_iota(jnp.int32, (R, R), 1)
    below = jnp.where(c1 < r1, 1.0, 0.0).astype(BF16)
    for h in range(nh):
        parts = _split3(lf_ref[h])
        within = sum(_dot(p, upper) for p in parts)
        totals = sum(_dot(p, ones) for p in parts)
        offs = sum(_dot(below, t) for t in _split3(totals))
        f_ref[h] = within + offs


def _fcum_call(lf):
    return pl.pallas_call(
        _fcum_kernel,
        out_shape=jax.ShapeDtypeStruct(lf.shape, F32),
        name="fcum",
    )(lf)


def _tile_pos(qi, j, tq, tk):
    row = qi * tq + lax.broadcasted_iota(jnp.int32, (tq, tk), 0)
    col = j * tk + lax.broadcasted_iota(jnp.int32, (tq, tk), 1)
    return row, col


def _split_q(q):
    lane = lax.broadcasted_iota(jnp.int32, (1, LANES), 1)
    lo = lane < HEAD_DIM
    zero = jnp.zeros_like(q)
    return jnp.where(lo, q, zero), jnp.where(lo, zero, q)


STRIP = 16
TRI = 256
PACE_LAG = 128


def _chunk(ref, r0, c):
    return ref[r0:r0 + STRIP, c * LANES:(c + 1) * LANES]


def _zero_after(x):
    half = jnp.uint32(16)
    bits = lax.shift_right_logical(lax.shift_right_logical(pltpu.bitcast(x, jnp.uint32), half), half)
    return pltpu.bitcast(bits, F32)


def _softmax_tile(s_ref, p_ref, m_ref, l_ref, chunk_state, pace, finish_strip):
    tq, tk = s_ref.shape
    for r0 in range(0, tq, STRIP):
        states = [chunk_state(r0, c) for c in range(tk // LANES)]
        sc = []
        for c, st in enumerate(states):
            if st is False:
                sc.append(None)
            elif st is None:
                sc.append(_chunk(s_ref, r0, c))
            else:
                sc.append(jnp.where(st, _chunk(s_ref, r0, c), NEG))
        live = [s for s in sc if s is not None]
        m_prev = m_ref[r0:r0 + STRIP, :]
        zero = pace(r0)
        if zero is not None:
            m_prev = m_prev + zero
        if live:
            mx = functools.reduce(jnp.maximum, live)
            m_new = jnp.maximum(m_prev, jnp.max(mx, axis=1, keepdims=True))
        else:
            m_new = m_prev
        alpha = jnp.exp2(m_prev - m_new)
        psum = jnp.zeros((STRIP, LANES), F32)
        for c, s in enumerate(sc):
            if s is None:
                p_ref[r0:r0 + STRIP, c * LANES:(c + 1) * LANES] = jnp.zeros((STRIP, LANES), BF16)
            else:
                p = jnp.exp2(s - m_new)
                psum = psum + p
                p_ref[r0:r0 + STRIP, c * LANES:(c + 1) * LANES] = p.astype(BF16)
        m_ref[r0:r0 + STRIP, :] = m_new
        l_ref[r0:r0 + STRIP, :] = alpha * l_ref[r0:r0 + STRIP, :] + psum
        finish_strip(r0, alpha)


def _all_visible(r0, c):
    return None


def _init_softmax_state(m_refs, l_refs, acc_refs):
    for m_ref, l_ref, acc_ref in zip(m_refs, l_refs, acc_refs):
        m_ref[...] = jnp.full(m_ref.shape, NEG, F32)
        l_ref[...] = jnp.zeros(l_ref.shape, F32)
        acc_ref[...] = jnp.zeros(acc_ref.shape, F32)


def _softmax_scratch(t):
    per_stream = [pltpu.VMEM((t, t), F32), pltpu.VMEM((t, t), F32),
                  pltpu.VMEM((t, t), BF16), pltpu.VMEM((t, t), BF16),
                  pltpu.VMEM((t, LANES), F32), pltpu.VMEM((t, LANES), F32),
                  pltpu.VMEM((t, LANES), F32)]
    return per_stream + per_stream


def _softmax_state_refs(scratch):
    return scratch[4::7], scratch[5::7], scratch[6::7]


def _row_total(l_ref):
    return jnp.sum(l_ref[...], axis=1, keepdims=True)


def _sweep(qi, t, scores, v_ref, scratch, diag_chunk_state):
    s0, s1, p0, p1, m_refs, l_refs, acc_refs = (scratch[i::7] for i in range(7))
    s_bufs, p_bufs = (s0, s1), (p0, p1)

    def step(j, par, chunk_state, prefetch, earlier_pv=None):
        s_next = scores(j + 1) if prefetch else None
        vb = v_ref[pl.ds(pl.multiple_of(j * t, t), t), :]
        pvs = []
        for x in range(2):
            if prefetch:
                s_bufs[1 - par][x][...] = s_next[x]

            def pace(r0, x=x):
                zero = None
                if earlier_pv is not None:
                    zero = _zero_after(earlier_pv[x][r0:r0 + STRIP, :])
                if prefetch and r0 >= PACE_LAG:
                    rows = slice(r0 - PACE_LAG, r0 - PACE_LAG + STRIP)
                    z2 = _zero_after(s_bufs[1 - par][x][rows, :LANES])
                    zero = z2 if zero is None else zero + z2
                return zero

            def finish_strip(r0, alpha, x=x):
                rows = slice(r0, r0 + STRIP)
                acc_refs[x][rows, :] = alpha * acc_refs[x][rows, :]

            _softmax_tile(s_bufs[par][x], p_bufs[par][x], m_refs[x], l_refs[x], chunk_state,
                          pace, finish_strip)
            pv = _dot(p_bufs[par][x][...], vb)
            acc_refs[x][...] += pv
            pvs.append(pv)
        return pvs

    first = scores(0)
    for x in range(2):
        s0[x][...] = first[x]

    def pair(i, c):
        pvs = step(2 * i, 0, _all_visible, True)
        step(2 * i + 1, 1, _all_visible, True, pvs)
        return c

    lax.fori_loop(0, qi // 2, pair, 0)

    @pl.when(qi % 2 == 1)
    def _():
        pvs = step(qi - 1, 0, _all_visible, True)
        step(qi, 1, diag_chunk_state, False, pvs)

    @pl.when(qi % 2 == 0)
    def _():
        step(qi, 0, diag_chunk_state, False)


def _diff_chunk_state(col0):
    def state(r0, c):
        q_chunk = r0 // CHUNK
        k_lo = (col0 + c * LANES) // CHUNK
        k_hi = (col0 + c * LANES + LANES - 1) // CHUNK
        if k_hi <= q_chunk:
            return None
        if k_lo > q_chunk:
            return False
        return lax.broadcasted_iota(jnp.int32, (STRIP, LANES), 1) < CHUNK
    return state


def _diff_kernel(t, lam_init, q_ref, k_ref, v_ref, lamv_ref, g_ref, o_ref, *scratch):
    m_refs, l_refs, acc_refs = _softmax_state_refs(scratch)
    qi = pl.program_id(1)
    qs = _split_q(q_ref[...])
    _init_softmax_state(m_refs, l_refs, acc_refs)

    def scores(j):
        kb = k_ref[pl.ds(pl.multiple_of(j * t, t), t), :]
        return [_dot_nt(qs[x], kb) for x in range(2)]

    _sweep(qi, t, scores, v_ref, scratch, _diff_chunk_state(0))

    lv = lamv_ref[...]
    lam = (jnp.exp(jnp.sum(lv[0:1] * lv[1:2], axis=1, keepdims=True))
           - jnp.exp(jnp.sum(lv[2:3] * lv[3:4], axis=1, keepdims=True)) + lam_init)
    o = (acc_refs[0][...] / _row_total(l_refs[0])
         - lam * (acc_refs[1][...] / _row_total(l_refs[1])))
    o = o * lax.rsqrt(jnp.mean(o * o, axis=-1, keepdims=True) + 1e-6) * g_ref[...]
    o_ref[...] = (o * (1.0 - lam_init)).astype(BF16)


def _diff_call(dq, dk, dv, lamv, gnorm, lam_init, tq):
    S = dq.shape[0]
    nh = dq.shape[1] // LANES
    qspec = pl.BlockSpec((tq, LANES), lambda h, i: (i, h))
    kvspec = pl.BlockSpec((S, LANES), lambda h, i: (0, h))
    full = lambda a: pl.BlockSpec(a.shape, lambda h, i: (0, 0))
    return pl.pallas_call(
        functools.partial(_diff_kernel, tq, lam_init),
        grid=(nh, S // tq),
        in_specs=[qspec, kvspec, kvspec, full(lamv), full(gnorm)],
        out_specs=qspec,
        out_shape=jax.ShapeDtypeStruct(dq.shape, BF16),
        scratch_shapes=_softmax_scratch(tq),
        compiler_params=_cparams(("parallel", "arbitrary")),
        name="diff_attn",
    )(dq, dk, dv, lamv, gnorm)


def _causal_chunk_state(strict, col0=0):
    def state(r0, c):
        k_lo, k_hi = col0 + c * LANES, col0 + c * LANES + LANES - 1
        q_lo, q_hi = r0, r0 + STRIP - 1
        if k_hi < q_lo or (not strict and k_hi <= q_lo):
            return None
        if k_lo > q_hi or (strict and k_lo >= q_hi):
            return False
        row = r0 + lax.broadcasted_iota(jnp.int32, (STRIP, LANES), 0)
        col = k_lo + lax.broadcasted_iota(jnp.int32, (STRIP, LANES), 1)
        return col < row if strict else col <= row
    return state


def _fox_kernel(t, q_ref, k_ref, v_ref, f_ref, o_ref, *scratch):
    m_refs, l_refs, acc_refs = _softmax_state_refs(scratch)
    qi = pl.program_id(1)
    qs = _split_q(q_ref[...])
    _init_softmax_state(m_refs, l_refs, acc_refs)
    f0 = [f_ref[x, qi][:, 0:1] for x in range(2)]

    def scores(j):
        kb = k_ref[pl.ds(pl.multiple_of(j * t, t), t), :]
        return [_dot_nt(qs[x], kb) + (f0[x] - f_ref[x, j]) * LOG2E for x in range(2)]

    _sweep(qi, t, scores, v_ref, scratch, _causal_chunk_state(strict=False))

    lane = lax.broadcasted_iota(jnp.int32, (1, LANES), 1)
    o = jnp.where(lane < HEAD_DIM, acc_refs[0][...] / _row_total(l_refs[0]),
                  acc_refs[1][...] / _row_total(l_refs[1]))
    o_ref[...] = o.astype(BF16)


def _fox_call(fq, fk, fv, fcum, tq):
    S = fq.shape[0]
    npair = fq.shape[1] // LANES
    qspec = pl.BlockSpec((tq, LANES), lambda p, i: (i, p))
    kvspec = pl.BlockSpec((S, LANES), lambda p, i: (0, p))
    nh = fcum.shape[0]
    fspec = pl.BlockSpec((2, S // tq, 1, tq), lambda p, i: (p, 0, 0, 0))
    return pl.pallas_call(
        functools.partial(_fox_kernel, tq),
        grid=(npair, S // tq),
        in_specs=[qspec, kvspec, kvspec, fspec],
        out_specs=qspec,
        out_shape=jax.ShapeDtypeStruct(fq.shape, BF16),
        scratch_shapes=_softmax_scratch(tq),
        compiler_params=_cparams(("parallel", "arbitrary")),
        name="fox_attn",
    )(fq, fk, fv, fcum.reshape(nh, S // tq, 1, tq))


def _sb_kernel(tq, tk, q_ref, k_ref, v_ref, tri_ref, o_ref, *scratch):
    (z0, z1, u0, u1, lb0, lb1, a0, a1, tot_refs, carry_refs, acc_refs) = (scratch[i::11]
                                                                          for i in range(11))
    z_bufs, u_bufs, lb_bufs, a_bufs = (z0, z1), (u0, u1), (lb0, lb1), (a0, a1)
    qi = pl.program_id(1)
    qs = _split_q(q_ref[...])
    for x in range(2):
        carry_refs[x][...] = jnp.zeros((tq, LANES), F32)
        acc_refs[x][...] = jnp.zeros((tq, LANES), F32)
    assert tk == tq and tk == 2 * TRI
    nc = tk // LANES

    def scores(j, par):
        kb = k_ref[pl.ds(pl.multiple_of(j * tk, tk), tk), :]
        for x in range(2):
            z_bufs[par][x][...] = _dot_nt(qs[x], kb)

    def gates(par, chunk_state, earlier_pv):
        for x in range(2):
            z_ref, u_ref, lb_ref = z_bufs[par][x], u_bufs[par][x], lb_bufs[par][x]
            for r0 in range(0, tq, STRIP):
                usum = jnp.zeros((STRIP, LANES), F32)
                if earlier_pv is not None:
                    usum = _zero_after(earlier_pv[x][r0:r0 + STRIP, :])
                if r0 >= PACE_LAG:
                    rows = slice(r0 - PACE_LAG, r0 - PACE_LAG + STRIP)
                    usum = usum + _zero_after(z_bufs[1 - par][x][rows, :LANES])
                for c in range(nc):
                    st = chunk_state(r0, c)
                    if st is False:
                        u_ref[r0:r0 + STRIP, c * LANES:(c + 1) * LANES] = jnp.zeros((STRIP, LANES), BF16)
                        lb_ref[r0:r0 + STRIP, c * LANES:(c + 1) * LANES] = jnp.zeros((STRIP, LANES), F32)
                        continue
                    z = _chunk(z_ref, r0, c)
                    zb = z.astype(BF16)
                    corr = jnp.log(1.0 + jnp.exp2(-jnp.abs(zb))) * LOG2E
                    sp = jnp.maximum(z, 0.0) + corr.astype(F32)
                    u = sp if st is None else jnp.where(st, sp, 0.0)
                    u_ref[r0:r0 + STRIP, c * LANES:(c + 1) * LANES] = u.astype(BF16)
                    lb_ref[r0:r0 + STRIP, c * LANES:(c + 1) * LANES] = z - sp
                    usum = usum + u
                tot_refs[x][r0:r0 + STRIP, :] = (jnp.sum(usum, axis=1, keepdims=True)
                                                 + jnp.zeros((STRIP, LANES), F32))

    def weights_and_accumulate(j, par, masked):
        vb = v_ref[pl.ds(pl.multiple_of(j * tk, tk), tk), :]
        tri_stack = tri_ref[...]
        pvs = []
        for x in range(2):
            u_ref, lb_ref, a_ref = u_bufs[par][x], lb_bufs[par][x], a_bufs[par][x]
            suffix = (_dot(u_ref[...], tri_stack), _dot(u_ref[:, TRI:], tri_stack[:TRI]))
            carry = carry_refs[x][...]
            for c in range(nc):
                cols = slice(c * LANES, (c + 1) * LANES)
                blk, off = divmod(c * LANES, TRI)
                suf = suffix[blk][:, off:off + LANES]
                a = jnp.exp2(lb_ref[:, cols] - suf - carry)
                if masked:
                    row = lax.broadcasted_iota(jnp.int32, (tq, LANES), 0)
                    col = c * LANES + lax.broadcasted_iota(jnp.int32, (tq, LANES), 1)
                    a = jnp.where(col < row, a, 0.0)
                a_ref[:, cols] = a.astype(BF16)
            pv = _dot(a_ref[...], vb)
            acc_refs[x][...] += pv
            carry_refs[x][...] = carry + tot_refs[x][...]
            pvs.append(pv)
        return pvs

    def step(j, par, masked, earlier_pv=None):
        scores(jnp.maximum(j - 1, 0), 1 - par)
        gates(par, _causal_chunk_state(True) if masked else _all_visible, earlier_pv)
        return weights_and_accumulate(j, par, masked)

    scores(qi, 0)
    step(qi, 0, True)

    def pair(i, c):
        j = qi - 1 - 2 * i
        pvs = step(j, 1, False)
        step(j - 1, 0, False, pvs)
        return c

    lax.fori_loop(0, qi // 2, pair, 0)

    @pl.when(qi % 2 == 1)
    def _():
        step(0, 1, False)

    lane = lax.broadcasted_iota(jnp.int32, (1, LANES), 1)
    o_ref[...] = jnp.where(lane < HEAD_DIM, acc_refs[0][...], acc_refs[1][...]).astype(BF16)


def _sb_call(sq, sk, sv, tq, tk):
    S = sq.shape[0]
    npair = sq.shape[1] // LANES
    qspec = pl.BlockSpec((tq, LANES), lambda p, i: (i, p))
    kvspec = pl.BlockSpec((S, LANES), lambda p, i: (0, p))
    tri = jnp.asarray(np.concatenate([np.tril(np.ones((TRI, TRI), np.float32), -1),
                                      np.ones((TRI, TRI), np.float32)]), BF16)
    return pl.pallas_call(
        functools.partial(_sb_kernel, tq, tk),
        grid=(npair, S // tq),
        in_specs=[qspec, kvspec, kvspec, pl.BlockSpec((2 * TRI, TRI), lambda p, i: (0, 0))],
        out_specs=qspec,
        out_shape=jax.ShapeDtypeStruct(sq.shape, BF16),
        scratch_shapes=2 * [pltpu.VMEM((tq, tk), F32), pltpu.VMEM((tq, tk), F32),
                            pltpu.VMEM((tq, tk), BF16), pltpu.VMEM((tq, tk), BF16),
                            pltpu.VMEM((tq, tk), F32), pltpu.VMEM((tq, tk), F32),
                            pltpu.VMEM((tq, tk), BF16), pltpu.VMEM((tq, tk), BF16),
                            pltpu.VMEM((tq, LANES), F32), pltpu.VMEM((tq, LANES), F32),
                            pltpu.VMEM((tq, LANES), F32)],
        compiler_params=_cparams(("parallel", "arbitrary")),
        name="sb_attn",
    )(sq, sk, sv, tri)


def _merge_kernel(hb_ref, h_ref, od_ref, os_ref, of_ref, wg_ref, wud_ref, wus_ref, wuf_ref, wo_ref,
                  g_ref, b_ref, out_ref, outb_ref):
    x = hb_ref[...]
    D = x.shape[1]
    merged = None
    for c, (o_ref, wu_ref) in enumerate(((od_ref, wud_ref), (os_ref, wus_ref), (of_ref, wuf_ref))):
        gate = _sigmoid(_dot(x, wg_ref[:, c * D:(c + 1) * D]))
        term = gate * _dot(o_ref[...], wu_ref[...])
        merged = term if merged is None else merged + term
    y = _dot(merged.astype(BF16), wo_ref[...])
    hn = _layer_norm(ALPHA * h_ref[...] + y, g_ref[...], b_ref[...])
    out_ref[...] = hn
    outb_ref[...] = hn.astype(BF16)


def _merge_call(hb, h, od, osb, of, wg, wud, wus, wuf, wo, g, b, tm):
    S, D = h.shape
    row = lambda a: pl.BlockSpec((tm, a.shape[1]), lambda i: (i, 0))
    full = lambda a: pl.BlockSpec(a.shape, lambda i: (0, 0))
    args = (hb, h, od, osb, of, wg, wud, wus, wuf, wo, g, b)
    in_specs = [row(a) for a in args[:5]] + [full(a) for a in args[5:]]
    return pl.pallas_call(
        _merge_kernel,
        grid=(S // tm,),
        in_specs=in_specs,
        out_specs=[row(h), row(h)],
        out_shape=[jax.ShapeDtypeStruct((S, D), F32), jax.ShapeDtypeStruct((S, D), BF16)],
        compiler_params=_cparams(("parallel",)),
        name="merge",
    )(*args)


def _swiglu_hidden(x, wg, wu):
    gate = _dot(x, wg)
    return (gate * _sigmoid(gate)) * _dot(x, wu)


def _ffn_kernel(hb_ref, h_ref, wg_ref, wu_ref, wd_ref, g_ref, b_ref, out_ref, outb_ref, acc_ref):
    f = pl.program_id(1)

    @pl.when(f == 0)
    def _():
        acc_ref[...] = jnp.zeros(acc_ref.shape, F32)

    hid = _swiglu_hidden(hb_ref[...], wg_ref[...], wu_ref[...])
    acc_ref[...] += _dot(hid.astype(BF16), wd_ref[...])

    @pl.when(f == pl.num_programs(1) - 1)
    def _():
        hn = _layer_norm(ALPHA * h_ref[...] + acc_ref[...], g_ref[...], b_ref[...])
        out_ref[...] = hn
        outb_ref[...] = hn.astype(BF16)


def _ffn_call(hb, h, wg, wu, wd, g, b, tm, tf):
    S, D = h.shape
    F = wg.shape[1]
    row = pl.BlockSpec((tm, D), lambda i, f: (i, 0))
    vec = pl.BlockSpec((1, D), lambda i, f: (0, 0))
    return pl.pallas_call(
        _ffn_kernel,
        grid=(S // tm, F // tf),
        in_specs=[row, row,
                  pl.BlockSpec((D, tf), lambda i, f: (0, f)),
                  pl.BlockSpec((D, tf), lambda i, f: (0, f)),
                  pl.BlockSpec((tf, D), lambda i, f: (f, 0)),
                  vec, vec],
        out_specs=[row, row],
        out_shape=[jax.ShapeDtypeStruct((S, D), F32), jax.ShapeDtypeStruct((S, D), BF16)],
        scratch_shapes=[pltpu.VMEM((tm, D), F32)],
        compiler_params=_cparams(("parallel", "arbitrary")),
        name="ffn",
    )(hb, h, wg, wu, wd, g, b)


MOE_TILE = 512
DMA_UNROLL = 8


def _route_kernel(h_ref, w_ref, b_ref, sel_ref, i1_ref, i2_ref, w1_ref, w2_ref):
    logits = jnp.dot(h_ref[...], w_ref[...], precision=lax.Precision.HIGHEST,
                     preferred_element_type=F32) + b_ref[...]
    lane = lax.broadcasted_iota(jnp.int32, logits.shape, 1)
    logits = jnp.where(lane < N_EXPERTS, logits, NEG)
    v1 = jnp.max(logits, axis=1, keepdims=True)
    i1 = jnp.min(jnp.where(logits == v1, lane, LANES), axis=1, keepdims=True)
    rest = jnp.where(lane == i1, NEG, logits)
    v2 = jnp.max(rest, axis=1, keepdims=True)
    i2 = jnp.min(jnp.where(rest == v2, lane, LANES), axis=1, keepdims=True)
    e2 = jnp.exp(v2 - v1)
    zero = jnp.zeros(logits.shape, F32)
    sel_ref[...] = jnp.where((lane == i1) | (lane == i2), 1.0, 0.0).astype(BF16)
    i1_ref[...] = i1 + jnp.zeros(logits.shape, jnp.int32)
    i2_ref[...] = i2 + jnp.zeros(logits.shape, jnp.int32)
    w1_ref[...] = 1.0 / (1.0 + e2) + zero
    w2_ref[...] = e2 / (1.0 + e2) + zero


def _route_call(h, w_router, b_router, tm):
    S, D = h.shape
    wpad = jnp.zeros((D, LANES), F32).at[:, :N_EXPERTS].set(w_router)
    bpad = jnp.zeros((1, LANES), F32).at[0, :N_EXPERTS].set(b_router)
    row = pl.BlockSpec((tm, LANES), lambda i: (i, 0))
    return pl.pallas_call(
        _route_kernel,
        grid=(S // tm,),
        in_specs=[pl.BlockSpec((tm, D), lambda i: (i, 0)),
                  pl.BlockSpec((D, LANES), lambda i: (0, 0)),
                  pl.BlockSpec((1, LANES), lambda i: (0, 0))],
        out_specs=[row] * 5,
        out_shape=[jax.ShapeDtypeStruct((S, LANES), BF16),
                   jax.ShapeDtypeStruct((S, LANES), jnp.int32),
                   jax.ShapeDtypeStruct((S, LANES), jnp.int32),
                   jax.ShapeDtypeStruct((S, LANES), F32),
                   jax.ShapeDtypeStruct((S, LANES), F32)],
        compiler_params=_cparams(("parallel",)),
        name="route",
    )(h, wpad, bpad)


def _rank_kernel(sel_ref, i1_ref, i2_ref, tril_ref, rank_ref, cnt_ref):
    @pl.when(pl.program_id(0) == 0)
    def _():
        cnt_ref[...] = jnp.zeros(cnt_ref.shape, F32)

    sel = sel_ref[...]
    rank = _dot(tril_ref[...], sel) + cnt_ref[...]
    lane = lax.broadcasted_iota(jnp.int32, rank.shape, 1)
    r1 = jnp.sum(jnp.where(lane == i1_ref[...], rank, 0.0), axis=1, keepdims=True)
    r2 = jnp.sum(jnp.where(lane == i2_ref[...], rank, 0.0), axis=1, keepdims=True)
    rank_ref[...] = jnp.where(lane == 0, r1, jnp.where(lane == 1, r2, 0.0))
    cnt_ref[...] += jnp.sum(sel.astype(F32), axis=0, keepdims=True)


def _rank_call(sel, i1b, i2b, tb):
    S = sel.shape[0]
    tril = jnp.asarray(np.tril(np.ones((tb, tb), np.float32), -1), BF16)
    row = pl.BlockSpec((tb, LANES), lambda i: (i, 0))
    return pl.pallas_call(
        _rank_kernel,
        grid=(S // tb,),
        in_specs=[row, row, row, pl.BlockSpec((tb, tb), lambda i: (0, 0))],
        out_specs=[row, pl.BlockSpec((1, LANES), lambda i: (0, 0))],
        out_shape=[jax.ShapeDtypeStruct((S, LANES), F32), jax.ShapeDtypeStruct((1, LANES), F32)],
        compiler_params=_cparams(("arbitrary",)),
        name="rank",
    )(sel, i1b, i2b, tril)


def _row_copy(src_ref, s, dst_ref, d, sem):
    return pltpu.make_async_copy(src_ref.at[pl.ds(s, 1)], dst_ref.at[pl.ds(d, 1)], sem)


def _dispatch_kernel(tb, tmx, n_tok, dest_ref, ends_ref, h_ref, xs_hbm, zero_ref, sem, zsem):
    base = pl.program_id(0) * tb

    @pl.when(pl.program_id(0) == 0)
    def _():
        zero_ref[...] = jnp.zeros(zero_ref.shape, F32)

        def fill(e, wait):
            start = 0 if e == 0 else ends_ref[e - 1]
            end = ends_ref[e]

            @pl.when(end > start)
            def _():
                cp = pltpu.make_async_copy(
                    zero_ref, xs_hbm.at[pl.ds(pl.multiple_of(end - tmx, tmx), tmx)], zsem)
                if wait:
                    cp.wait()
                else:
                    cp.start()

        def fill_tail(k, wait):
            start = ends_ref[N_EXPERTS - 1] + k * tmx

            @pl.when(start < xs_hbm.shape[0])
            def _():
                cp = pltpu.make_async_copy(
                    zero_ref, xs_hbm.at[pl.ds(pl.multiple_of(start, tmx), tmx)], zsem)
                if wait:
                    cp.wait()
                else:
                    cp.start()

        for wait in (False, True):
            for e in range(N_EXPERTS):
                fill(e, wait)
                fill_tail(e, wait)

    def issue(t, c):
        _row_copy(h_ref, t, xs_hbm, dest_ref[base + t], sem).start(priority=0)
        _row_copy(h_ref, t, xs_hbm, dest_ref[n_tok + base + t], sem).start(priority=1)
        return c

    lax.fori_loop(0, tb, issue, 0, unroll=DMA_UNROLL)

    def drain(t, c):
        _row_copy(h_ref, 0, xs_hbm, 0, sem).wait()
        _row_copy(h_ref, 0, xs_hbm, 0, sem).wait()
        return c

    lax.fori_loop(0, tb, drain, 0, unroll=DMA_UNROLL)


def _dispatch_call(dest, ends, h, n_rows, tb, tmx):
    S, D = h.shape
    return pl.pallas_call(
        functools.partial(_dispatch_kernel, tb, tmx, S),
        grid_spec=pltpu.PrefetchScalarGridSpec(
            num_scalar_prefetch=2,
            grid=(S // tb,),
            in_specs=[pl.BlockSpec((tb, D), lambda i, d, e: (i, 0))],
            out_specs=pl.BlockSpec(memory_space=pl.ANY),
            scratch_shapes=[pltpu.VMEM((tmx, D), F32), pltpu.SemaphoreType.DMA,
                            pltpu.SemaphoreType.DMA],
        ),
        out_shape=jax.ShapeDtypeStruct((n_rows, D), F32),
        compiler_params=_cparams(("arbitrary",)),
        name="dispatch",
    )(dest, ends, h)


def _experts_kernel(te_ref, nu_ref, xs_ref, wg_ref, wu_ref, wd_ref, ys_ref, xb_ref, acc_ref):
    i = pl.program_id(0)
    f = pl.program_id(1)
    last_f = pl.num_programs(1) - 1
    used = i < nu_ref[0]

    @pl.when(used & (f == 0))
    def _():
        xb_ref[...] = xs_ref[...].astype(BF16)

    @pl.when(used)
    def _():
        hid = _swiglu_hidden(xb_ref[...], wg_ref[0], wu_ref[0])
        part = _dot(hid.astype(BF16), wd_ref[0])

        @pl.when(f == 0)
        def _():
            acc_ref[...] = part

        @pl.when(f > 0)
        def _():
            acc_ref[...] += part

    @pl.when(used & (f == last_f))
    def _():
        ys_ref[...] = acc_ref[...]

    @pl.when(jnp.logical_not(used) & (f == last_f))
    def _():
        ys_ref[...] = jnp.zeros(ys_ref.shape, F32)


def _experts_call(tile_expert, n_used, xs, wg, wu, wd, tmx, tf):
    P, D = xs.shape
    F = wg.shape[2]
    nf = F // tf

    def fsel(i, f, nu):
        return jnp.where(i < nu[0], f, nf - 1)

    return pl.pallas_call(
        _experts_kernel,
        grid_spec=pltpu.PrefetchScalarGridSpec(
            num_scalar_prefetch=2,
            grid=(P // tmx, nf),
            in_specs=[pl.BlockSpec((tmx, D), lambda i, f, te, nu: (jnp.where(i < nu[0], i, 0), 0)),
                      pl.BlockSpec((1, D, tf), lambda i, f, te, nu: (te[i], 0, fsel(i, f, nu))),
                      pl.BlockSpec((1, D, tf), lambda i, f, te, nu: (te[i], 0, fsel(i, f, nu))),
                      pl.BlockSpec((1, tf, D), lambda i, f, te, nu: (te[i], fsel(i, f, nu), 0))],
            out_specs=pl.BlockSpec((tmx, D), lambda i, f, te, nu: (i, 0)),
            scratch_shapes=[pltpu.VMEM((tmx, D), BF16), pltpu.VMEM((tmx, D), F32)],
        ),
        out_shape=jax.ShapeDtypeStruct((P, D), F32),
        compiler_params=_cparams(("arbitrary", "arbitrary")),
        name="experts",
    )(tile_expert, n_used, xs, wg, wu, wd)


def _combine_kernel(tm, n_tok, dest_ref, ys_hbm, h_ref, w1_ref, w2_ref, g_ref, b_ref,
                    out_ref, outb_ref, buf_ref, sem):
    i = pl.program_id(0)
    slot = i % 2

    def gather(tile, slot_, start):
        base = tile * tm

        def one(t, c):
            for k in range(2):
                src_row = dest_ref[k * n_tok + base + t] if start else 0
                cp = pltpu.make_async_copy(ys_hbm.at[pl.ds(src_row, 1)],
                                           buf_ref.at[slot_, k, pl.ds(t, 1)], sem.at[slot_])
                if start:
                    cp.start(priority=k)
                else:
                    cp.wait()
            return c

        lax.fori_loop(0, tm, one, 0, unroll=DMA_UNROLL)

    @pl.when(i == 0)
    def _():
        gather(0, 0, True)

    @pl.when(i + 1 < pl.num_programs(0))
    def _():
        gather(i + 1, 1 - slot, True)

    gather(i, slot, False)

    w1 = w1_ref[...]
    w2 = w2_ref[...]
    y = jnp.concatenate(
        [w1 * buf_ref[slot, 0, :, c * LANES:(c + 1) * LANES]
         + w2 * buf_ref[slot, 1, :, c * LANES:(c + 1) * LANES]
         for c in range(h_ref.shape[1] // LANES)], axis=1)
    hn = _layer_norm(ALPHA * h_ref[...] + y, g_ref[...], b_ref[...])
    out_ref[...] = hn
    outb_ref[...] = hn.astype(BF16)


def _combine_call(dest, ys, h, w1b, w2b, g, b, tm):
    S, D = h.shape
    row = pl.BlockSpec((tm, D), lambda i, d: (i, 0))
    lrow = pl.BlockSpec((tm, LANES), lambda i, d: (i, 0))
    vec = pl.BlockSpec((1, D), lambda i, d: (0, 0))
    return pl.pallas_call(
        functools.partial(_combine_kernel, tm, S),
        grid_spec=pltpu.PrefetchScalarGridSpec(
            num_scalar_prefetch=1,
            grid=(S // tm,),
            in_specs=[pl.BlockSpec(memory_space=pl.ANY), row, lrow, lrow, vec, vec],
            out_specs=[row, row],
            scratch_shapes=[pltpu.VMEM((2, 2, tm, D), F32), pltpu.SemaphoreType.DMA((2,))],
        ),
        out_shape=[jax.ShapeDtypeStruct((S, D), F32), jax.ShapeDtypeStruct((S, D), BF16)],
        compiler_params=_cparams(("arbitrary",)),
        name="combine",
    )(dest, ys, h, w1b, w2b, g, b)


def _moe_call(h, w_router, b_router, wg, wu, wd, g, b, tm, tf):
    S, D = h.shape
    tmx = MOE_TILE
    sel, i1b, i2b, w1b, w2b = _route_call(h, w_router, b_router, tm)
    ranks, counts = _rank_call(sel, i1b, i2b, min(256, S))
    cnt = counts[0, :N_EXPERTS].astype(jnp.int32)
    padded = ((cnt + tmx - 1) // tmx) * tmx
    ends = jnp.cumsum(padded)
    off = ends - padded
    dest = jnp.concatenate([off[i1b[:, 0]] + ranks[:, 0].astype(jnp.int32),
                            off[i2b[:, 0]] + ranks[:, 1].astype(jnp.int32)])
    n_tiles = (2 * S) // tmx + N_EXPERTS
    tile_start = jnp.arange(n_tiles, dtype=jnp.int32) * tmx
    tile_expert = jnp.minimum(jnp.sum(tile_start[:, None] >= ends[None, :], axis=1),
                              N_EXPERTS - 1).astype(jnp.int32)
    n_used = (ends[-1] // tmx).astype(jnp.int32).reshape(1)
    xs = _dispatch_call(dest, ends.astype(jnp.int32), h, n_tiles * tmx, tm, tmx)
    ys = _experts_call(tile_expert, n_used, xs, wg, wu, wd, tmx, tf)
    return _combine_call(dest, ys, h, w1b, w2b, g, b, min(256, S))


def _rope_tables(S):
    half = HEAD_DIM // 2
    pos = jnp.arange(S, dtype=F32)
    inv = ROPE_THETA ** (-jnp.arange(half, dtype=F32) / half)
    ang = pos[:, None] * inv[None, :]
    cos, sin = jnp.cos(ang), jnp.sin(ang)
    reps = LANES // HEAD_DIM
    cos_t = jnp.tile(jnp.concatenate([cos, cos], axis=1), (1, reps))
    sin_t = jnp.tile(jnp.concatenate([-sin, sin], axis=1), (1, reps))
    return cos_t, sin_t


def _pick(S, pref):
    t = min(pref, S)
    assert S % t == 0
    return t


def _forward(x, ln_in_g, ln_in_b, w_in, b_forget, lam_q1, lam_k1, lam_q2, lam_k2,
             diff_norm_g, w_up_diff, w_up_sb, w_up_fox, w_out, ln_mix_g, ln_mix_b,
             ln_ffn_g, ln_ffn_b, w_gate_dense, w_up_dense, w_down_dense, w_router,
             b_router, w_gate_moe, w_up_moe, w_down_moe):
    B, S, D = x.shape
    assert B == 1 and D == D_MODEL and S % LANES == 0
    depth = w_in.shape[0]
    tm = _pick(S, 512)
    tq = _pick(S, 512)
    tm_ffn = _pick(S, 1024)
    tf = D_FF // 7
    tf_moe = D_FF // 2
    vec = lambda a: a.reshape(1, -1).astype(F32)

    cos_t, sin_t = _rope_tables(S)
    h, hb = _ln_call(x.reshape(S, D), ln_in_g, ln_in_b, tm)
    w_atts, w_forgets, w_gate_logits = _wsplit_call(w_in, 128)
    for l in range(depth):
        w_att, w_forget, w_gate_logit = w_atts[l], w_forgets[l], w_gate_logits[l]
        fbias = jnp.pad(b_forget[l].astype(F32), (0, LANES - N_HEADS_FOX)).reshape(1, LANES)
        dq, dk, dv, sq, sk, sv, fq, fk, fv, logf = _proj_call(hb, w_att, w_forget, cos_t, sin_t,
                                                              fbias, tm)

        lf = logf[:, :N_HEADS_FOX].T.reshape(N_HEADS_FOX, S // LANES, LANES)
        fcum = _fcum_call(lf)

        lam_init = 0.8 - 0.6 * math.exp(-0.3 * l)
        lamv = jnp.stack([lam_q1[l], lam_k1[l], lam_q2[l], lam_k2[l]]).astype(F32)
        o_diff = _diff_call(dq, dk, dv, lamv, vec(diff_norm_g[l]), lam_init, tq)
        o_sb = _sb_call(sq, sk, sv, tq, tq)
        o_fox = _fox_call(fq, fk, fv, fcum, tq)

        h, hb = _merge_call(hb, h, o_diff, o_sb, o_fox, w_gate_logit,
                            w_up_diff[l].astype(BF16), w_up_sb[l].astype(BF16),
                            w_up_fox[l].astype(BF16), w_out[l].astype(BF16),
                            vec(ln_mix_g[l]), vec(ln_mix_b[l]), tm)
        j = l // 2
        if l % 2 == 0:
            h, hb = _ffn_call(hb, h, w_gate_dense[j].astype(BF16), w_up_dense[j].astype(BF16),
                              w_down_dense[j].astype(BF16), vec(ln_ffn_g[l]), vec(ln_ffn_b[l]),
                              tm_ffn, tf)
        else:
            h, hb = _moe_call(h, w_router[j], b_router[j], w_gate_moe[j].astype(BF16),
                              w_up_moe[j].astype(BF16), w_down_moe[j].astype(BF16),
                              vec(ln_ffn_g[l]), vec(ln_ffn_b[l]), tm, tf_moe)
    return h.reshape(B, S, D)


def kernel(x, ln_in_g, ln_in_b, w_in, b_forget, lam_q1, lam_k1, lam_q2, lam_k2, diff_norm_g,
           w_up_diff, w_up_sb, w_up_fox, w_out, ln_mix_g, ln_mix_b, ln_ffn_g, ln_ffn_b,
           w_gate_dense, w_up_dense, w_down_dense, w_router, b_router, w_gate_moe, w_up_moe,
           w_down_moe):
    return _forward(x, ln_in_g, ln_in_b, w_in, b_forget, lam_q1, lam_k1, lam_q2, lam_k2,
                    diff_norm_g, w_up_diff, w_up_sb, w_up_fox, w_out, ln_mix_g, ln_mix_b,
                    ln_ffn_g, ln_ffn_b, w_gate_dense, w_up_dense, w_down_dense, w_router,
                    b_router, w_gate_moe, w_up_moe, w_down_moe)
```

```python
import functools
import math

import jax
import jax.numpy as jnp
import numpy as np
from jax import lax
from jax.experimental import pallas as pl
from jax.experimental.pallas import tpu as pltpu

D_MODEL = 1024
DEPTH = 2
CHUNK = 64
HEAD_DIM = 64
N_HEADS_DIFF = 4
N_HEADS_SB = 4
N_HEADS_FOX = 4
WIDTH_DIFF = N_HEADS_DIFF * 2 * HEAD_DIM
WIDTH_SB = N_HEADS_SB * HEAD_DIM
WIDTH_FOX = N_HEADS_FOX * HEAD_DIM
ROPE_THETA = 10000.0
N_EXPERTS = 8
D_FF = 3584
ALPHA = (2 * DEPTH) ** 0.25
Q_SCALE = HEAD_DIM ** -0.5
LOG2E = 1.4426950408889634
N_ATT = 3 * WIDTH_DIFF + 3 * WIDTH_SB + 3 * WIDTH_FOX
LANES = 128
NEG = -1e30

F32 = jnp.float32
BF16 = jnp.bfloat16

VMEM_LIMIT = 56 * 1024 * 1024


def _cparams(sem):
    return pltpu.CompilerParams(dimension_semantics=sem, vmem_limit_bytes=VMEM_LIMIT)


def _layer_norm(z, g, b, eps=1e-5):
    mu = jnp.mean(z, axis=-1, keepdims=True)
    zc = z - mu
    var = jnp.mean(zc * zc, axis=-1, keepdims=True)
    return zc * lax.rsqrt(var + eps) * g + b


def _sigmoid(x):
    return 1.0 / (1.0 + jnp.exp(-x))


def _softplus(x):
    return jnp.maximum(x, 0.0) + jnp.log1p(jnp.exp(-jnp.abs(x)))


def _dot(a, b):
    return jnp.dot(a, b, preferred_element_type=F32)


def _dot_nt(a, b):
    return lax.dot_general(a, b, (((1,), (1,)), ((), ())), preferred_element_type=F32)


def _split3(x):
    x1 = x.astype(BF16)
    r = x - x1.astype(F32)
    x2 = r.astype(BF16)
    x3 = (r - x2.astype(F32)).astype(BF16)
    return x1, x2, x3


def _ln_kernel(x_ref, g_ref, b_ref, h_ref, hb_ref):
    h = _layer_norm(x_ref[...], g_ref[...], b_ref[...])
    h_ref[...] = h
    hb_ref[...] = h.astype(BF16)


def _ln_call(x, g, b, tm):
    S, D = x.shape
    row = pl.BlockSpec((tm, D), lambda i: (i, 0))
    vec = pl.BlockSpec((1, D), lambda i: (0, 0))
    return pl.pallas_call(
        _ln_kernel,
        grid=(S // tm,),
        in_specs=[row, vec, vec],
        out_specs=[row, row],
        out_shape=[jax.ShapeDtypeStruct((S, D), F32), jax.ShapeDtypeStruct((S, D), BF16)],
        compiler_params=_cparams(("parallel",)),
        name="ln_in",
    )(x, g.reshape(1, D), b.reshape(1, D))


def _wsplit_kernel(w_ref, att_ref, fgt_ref, gate_ref):
    att_ref[0] = w_ref[0, :, :N_ATT].astype(BF16)
    lane = lax.broadcasted_iota(jnp.int32, (1, LANES), 1)
    fgt_ref[0] = jnp.where(lane < N_HEADS_FOX, w_ref[0, :, N_ATT:N_ATT + LANES], 0.0).astype(BF16)
    gate_ref[0] = w_ref[0, :, N_ATT + N_HEADS_FOX:].astype(BF16)


def _wsplit_call(w_in, tr):
    depth, D, n_in = w_in.shape
    n_gate = n_in - N_ATT - N_HEADS_FOX
    spec = lambda n: pl.BlockSpec((1, tr, n), lambda l, i: (l, i, 0))
    return pl.pallas_call(
        _wsplit_kernel,
        grid=(depth, D // tr),
        in_specs=[spec(n_in)],
        out_specs=[spec(N_ATT), spec(LANES), spec(n_gate)],
        out_shape=[jax.ShapeDtypeStruct((depth, D, N_ATT), BF16),
                   jax.ShapeDtypeStruct((depth, D, LANES), BF16),
                   jax.ShapeDtypeStruct((depth, D, n_gate), BF16)],
        compiler_params=_cparams(("parallel", "parallel")),
        name="wsplit",
    )(w_in)


def _proj_kernel(hb_ref, w_ref, wf_ref, cos_ref, sin_ref, fb_ref,
                 dq_ref, dk_ref, dv_ref, sq_ref, sk_ref, sv_ref, fq_ref, fk_ref, fv_ref, lf_ref):
    x = hb_ref[...]
    cos = cos_ref[...]
    sin = sin_ref[...]
    lane = lax.broadcasted_iota(jnp.int32, (1, LANES), 1)
    lower = (lane & (HEAD_DIM // 2)) == 0

    def mm(c0, c1):
        return _dot(x, w_ref[:, c0:c1])

    def rope_store(y, out_ref, scale):
        for g in range(y.shape[1] // LANES):
            yg = y[:, g * LANES:(g + 1) * LANES]
            partner = jnp.where(lower, pltpu.roll(yg, LANES - HEAD_DIM // 2, 1),
                                pltpu.roll(yg, HEAD_DIM // 2, 1))
            r = yg * cos + partner * sin
            if scale != 1.0:
                r = r * scale
            out_ref[:, g * LANES:(g + 1) * LANES] = r.astype(BF16)

    rope_store(mm(0, 512), dq_ref, Q_SCALE * LOG2E)
    rope_store(mm(512, 1024), dk_ref, 1.0)
    dv_ref[...] = mm(1024, 1536).astype(BF16)
    sq_ref[...] = (mm(1536, 1792) * (Q_SCALE * LOG2E)).astype(BF16)
    sk_ref[...] = mm(1792, 2048).astype(BF16)
    sv_ref[...] = mm(2048, 2304).astype(BF16)
    fq_ref[...] = (mm(2304, 2560) * (Q_SCALE * LOG2E)).astype(BF16)
    fk_ref[...] = mm(2560, 2816).astype(BF16)
    fv_ref[...] = mm(2816, 3072).astype(BF16)
    lf_ref[...] = -_softplus(-(_dot(x, wf_ref[...]) + fb_ref[...]))


def _proj_call(hb, w_att, w_forget, cos, sin, fbias, tm):
    S, D = hb.shape
    row = lambda n: pl.BlockSpec((tm, n), lambda i: (i, 0))
    full = lambda a: pl.BlockSpec(a.shape, lambda i: (0, 0))
    widths = [512, 512, 512, 256, 256, 256, 256, 256, 256]
    out_shape = [jax.ShapeDtypeStruct((S, n), BF16) for n in widths]
    out_shape.append(jax.ShapeDtypeStruct((S, LANES), F32))
    return pl.pallas_call(
        _proj_kernel,
        grid=(S // tm,),
        in_specs=[row(D), full(w_att), full(w_forget), row(LANES), row(LANES), full(fbias)],
        out_specs=[row(n) for n in widths] + [row(LANES)],
        out_shape=out_shape,
        compiler_params=_cparams(("parallel",)),
        name="proj",
    )(hb, w_att, w_forget, cos, sin, fbias)


def _fcum_kernel(lf_ref, f_ref):
    nh, R, _ = lf_ref.shape
    r0 = lax.broadcasted_iota(jnp.int32, (LANES, LANES), 0)
    c0 = lax.broadcasted_iota(jnp.int32, (LANES, LANES), 1)
    upper = jnp.where(r0 <= c0, 1.0, 0.0).astype(BF16)
    ones = jnp.ones((LANES, LANES), BF16)
    r1 = lax.broadcasted_iota(jnp.int32, (R, R), 0)
    c1 = lax.broadcasted_iota(jnp.int32, (R, R), 1)
    below = jnp.where(c1 < r1, 1.0, 0.0).astype(BF16)
    for h in range(nh):
        parts = _split3(lf_ref[h])
        within = sum(_dot(p, upper) for p in parts)
        totals = sum(_dot(p, ones) for p in parts)
        offs = sum(_dot(below, t) for t in _split3(totals))
        f_ref[h] = within + offs


def _fcum_call(lf):
    return pl.pallas_call(
        _fcum_kernel,
        out_shape=jax.ShapeDtypeStruct(lf.shape, F32),
        name="fcum",
    )(lf)


def _split_q(q):
    lane = lax.broadcasted_iota(jnp.int32, (1, LANES), 1)
    lo = lane < HEAD_DIM
    zero = jnp.zeros_like(q)
    return jnp.where(lo, q, zero), jnp.where(lo, zero, q)


STRIP = 16
TRI = 256
PACE_LAG = 128


def _chunk(ref, r0, c):
    return ref[r0:r0 + STRIP, c * LANES:(c + 1) * LANES]


def _zero_after(x):
    half = jnp.uint32(16)
    bits = lax.shift_right_logical(lax.shift_right_logical(pltpu.bitcast(x, jnp.uint32), half), half)
    return pltpu.bitcast(bits, F32)


def _softmax_tile(s_ref, p_ref, m_ref, l_ref, chunk_state, pace, finish_strip):
    tq, tk = s_ref.shape
    for r0 in range(0, tq, STRIP):
        states = [chunk_state(r0, c) for c in range(tk // LANES)]
        sc = []
        for c, st in enumerate(states):
            if st is False:
                sc.append(None)
            elif st is None:
                sc.append(_chunk(s_ref, r0, c))
            else:
                sc.append(jnp.where(st, _chunk(s_ref, r0, c), NEG))
        live = [s for s in sc if s is not None]
        m_prev = m_ref[r0:r0 + STRIP, :]
        zero = pace(r0)
        if zero is not None:
            m_prev = m_prev + zero
        if live:
            mx = functools.reduce(jnp.maximum, live)
            m_new = jnp.maximum(m_prev, jnp.max(mx, axis=1, keepdims=True))
        else:
            m_new = m_prev
        alpha = jnp.exp2(m_prev - m_new)
        psum = jnp.zeros((STRIP, LANES), F32)
        for c, s in enumerate(sc):
            if s is None:
                p_ref[r0:r0 + STRIP, c * LANES:(c + 1) * LANES] = jnp.zeros((STRIP, LANES), BF16)
            else:
                p = jnp.exp2(s - m_new)
                psum = psum + p
                p_ref[r0:r0 + STRIP, c * LANES:(c + 1) * LANES] = p.astype(BF16)
        m_ref[r0:r0 + STRIP, :] = m_new
        l_ref[r0:r0 + STRIP, :] = alpha * l_ref[r0:r0 + STRIP, :] + psum
        finish_strip(r0, alpha)


def _all_visible(r0, c):
    return None


def _init_softmax_state(m_refs, l_refs, acc_refs):
    for m_ref, l_ref, acc_ref in zip(m_refs, l_refs, acc_refs):
        m_ref[...] = jnp.full(m_ref.shape, NEG, F32)
        l_ref[...] = jnp.zeros(l_ref.shape, F32)
        acc_ref[...] = jnp.zeros(acc_ref.shape, F32)


def _softmax_scratch(t):
    per_stream = [pltpu.VMEM((t, t), F32), pltpu.VMEM((t, t), F32),
                  pltpu.VMEM((t, t), BF16), pltpu.VMEM((t, t), BF16),
                  pltpu.VMEM((t, LANES), F32), pltpu.VMEM((t, LANES), F32),
                  pltpu.VMEM((t, LANES), F32)]
    return per_stream + per_stream


def _softmax_state_refs(scratch):
    return scratch[4::7], scratch[5::7], scratch[6::7]


def _row_total(l_ref):
    return jnp.sum(l_ref[...], axis=1, keepdims=True)


def _sweep(qi, t, scores, v_ref, scratch, diag_chunk_state):
    s0, s1, p0, p1, m_refs, l_refs, acc_refs = (scratch[i::7] for i in range(7))
    s_bufs, p_bufs = (s0, s1), (p0, p1)

    def step(j, par, chunk_state, prefetch, earlier_pv=None):
        s_next = scores(j + 1) if prefetch else None
        vb = v_ref[pl.ds(pl.multiple_of(j * t, t), t), :]
        pvs = []
        for x in range(2):
            if prefetch:
                s_bufs[1 - par][x][...] = s_next[x]

            def pace(r0, x=x):
                zero = None
                if earlier_pv is not None:
                    zero = _zero_after(earlier_pv[x][r0:r0 + STRIP, :])
                if prefetch and r0 >= PACE_LAG:
                    rows = slice(r0 - PACE_LAG, r0 - PACE_LAG + STRIP)
                    z2 = _zero_after(s_bufs[1 - par][x][rows, :LANES])
                    zero = z2 if zero is None else zero + z2
                return zero

            def finish_strip(r0, alpha, x=x):
                rows = slice(r0, r0 + STRIP)
                acc_refs[x][rows, :] = alpha * acc_refs[x][rows, :]

            _softmax_tile(s_bufs[par][x], p_bufs[par][x], m_refs[x], l_refs[x], chunk_state,
                          pace, finish_strip)
            pv = _dot(p_bufs[par][x][...], vb)
            acc_refs[x][...] += pv
            pvs.append(pv)
        return pvs

    first = scores(0)
    for x in range(2):
        s0[x][...] = first[x]

    def pair(i, c):
        pvs = step(2 * i, 0, _all_visible, True)
        step(2 * i + 1, 1, _all_visible, True, pvs)
        return c

    lax.fori_loop(0, qi // 2, pair, 0)

    @pl.when(qi % 2 == 1)
    def _():
        pvs = step(qi - 1, 0, _all_visible, True)
        step(qi, 1, diag_chunk_state, False, pvs)

    @pl.when(qi % 2 == 0)
    def _():
        step(qi, 0, diag_chunk_state, False)


def _diff_chunk_state(col0):
    def state(r0, c):
        q_chunk = r0 // CHUNK
        k_lo = (col0 + c * LANES) // CHUNK
        k_hi = (col0 + c * LANES + LANES - 1) // CHUNK
        if k_hi <= q_chunk:
            return None
        if k_lo > q_chunk:
            return False
        return lax.broadcasted_iota(jnp.int32, (STRIP, LANES), 1) < CHUNK
    return state


def _diff_kernel(t, lam_init, q_ref, k_ref, v_ref, lamv_ref, g_ref, o_ref, *scratch):
    m_refs, l_refs, acc_refs = _softmax_state_refs(scratch)
    qi = pl.program_id(1)
    qs = _split_q(q_ref[...])
    _init_softmax_state(m_refs, l_refs, acc_refs)

    def scores(j):
        kb = k_ref[pl.ds(pl.multiple_of(j * t, t), t), :]
        return [_dot_nt(qs[x], kb) for x in range(2)]

    _sweep(qi, t, scores, v_ref, scratch, _diff_chunk_state(0))

    lv = lamv_ref[...]
    lam = (jnp.exp(jnp.sum(lv[0:1] * lv[1:2], axis=1, keepdims=True))
           - jnp.exp(jnp.sum(lv[2:3] * lv[3:4], axis=1, keepdims=True)) + lam_init)
    o = (acc_refs[0][...] / _row_total(l_refs[0])
         - lam * (acc_refs[1][...] / _row_total(l_refs[1])))
    o = o * lax.rsqrt(jnp.mean(o * o, axis=-1, keepdims=True) + 1e-6) * g_ref[...]
    o_ref[...] = (o * (1.0 - lam_init)).astype(BF16)


def _diff_call(dq, dk, dv, lamv, gnorm, lam_init, tq):
    S = dq.shape[0]
    nh = dq.shape[1] // LANES
    qspec = pl.BlockSpec((tq, LANES), lambda h, i: (i, h))
    kvspec = pl.BlockSpec((S, LANES), lambda h, i: (0, h))
    full = lambda a: pl.BlockSpec(a.shape, lambda h, i: (0, 0))
    return pl.pallas_call(
        functools.partial(_diff_kernel, tq, lam_init),
        grid=(nh, S // tq),
        in_specs=[qspec, kvspec, kvspec, full(lamv), full(gnorm)],
        out_specs=qspec,
        out_shape=jax.ShapeDtypeStruct(dq.shape, BF16),
        scratch_shapes=_softmax_scratch(tq),
        compiler_params=_cparams(("parallel", "arbitrary")),
        name="diff_attn",
    )(dq, dk, dv, lamv, gnorm)


def _causal_chunk_state(strict, col0=0):
    def state(r0, c):
        k_lo, k_hi = col0 + c * LANES, col0 + c * LANES + LANES - 1
        q_lo, q_hi = r0, r0 + STRIP - 1
        if k_hi < q_lo or (not strict and k_hi <= q_lo):
            return None
        if k_lo > q_hi or (strict and k_lo >= q_hi):
            return False
        row = r0 + lax.broadcasted_iota(jnp.int32, (STRIP, LANES), 0)
        col = k_lo + lax.broadcasted_iota(jnp.int32, (STRIP, LANES), 1)
        return col < row if strict else col <= row
    return state


def _fox_kernel(t, q_ref, k_ref, v_ref, f_ref, o_ref, *scratch):
    m_refs, l_refs, acc_refs = _softmax_state_refs(scratch)
    qi = pl.program_id(1)
    qs = _split_q(q_ref[...])
    _init_softmax_state(m_refs, l_refs, acc_refs)
    f0 = [f_ref[x, qi][:, 0:1] for x in range(2)]

    def scores(j):
        kb = k_ref[pl.ds(pl.multiple_of(j * t, t), t), :]
        return [_dot_nt(qs[x], kb) + (f0[x] - f_ref[x, j]) * LOG2E for x in range(2)]

    _sweep(qi, t, scores, v_ref, scratch, _causal_chunk_state(strict=False))

    lane = lax.broadcasted_iota(jnp.int32, (1, LANES), 1)
    o = jnp.where(lane < HEAD_DIM, acc_refs[0][...] / _row_total(l_refs[0]),
                  acc_refs[1][...] / _row_total(l_refs[1]))
    o_ref[...] = o.astype(BF16)


def _fox_call(fq, fk, fv, fcum, tq):
    S = fq.shape[0]
    npair = fq.shape[1] // LANES
    qspec = pl.BlockSpec((tq, LANES), lambda p, i: (i, p))
    kvspec = pl.BlockSpec((S, LANES), lambda p, i: (0, p))
    nh = fcum.shape[0]
    fspec = pl.BlockSpec((2, S // tq, 1, tq), lambda p, i: (p, 0, 0, 0))
    return pl.pallas_call(
        functools.partial(_fox_kernel, tq),
        grid=(npair, S // tq),
        in_specs=[qspec, kvspec, kvspec, fspec],
        out_specs=qspec,
        out_shape=jax.ShapeDtypeStruct(fq.shape, BF16),
        scratch_shapes=_softmax_scratch(tq),
        compiler_params=_cparams(("parallel", "arbitrary")),
        name="fox_attn",
    )(fq, fk, fv, fcum.reshape(nh, S // tq, 1, tq))


def _sb_kernel(tq, tk, q_ref, k_ref, v_ref, tri_ref, o_ref, *scratch):
    (z0, z1, u0, u1, lb0, lb1, a0, a1, tot_refs, carry_refs, acc_refs) = (scratch[i::11]
                                                                          for i in range(11))
    z_bufs, u_bufs, lb_bufs, a_bufs = (z0, z1), (u0, u1), (lb0, lb1), (a0, a1)
    qi = pl.program_id(1)
    qs = _split_q(q_ref[...])
    for x in range(2):
        carry_refs[x][...] = jnp.zeros((tq, LANES), F32)
        acc_refs[x][...] = jnp.zeros((tq, LANES), F32)
    assert tk == tq and tk == 2 * TRI
    nc = tk // LANES

    def scores(j, par):
        kb = k_ref[pl.ds(pl.multiple_of(j * tk, tk), tk), :]
        for x in range(2):
            z_bufs[par][x][...] = _dot_nt(qs[x], kb)

    def gates(par, chunk_state, earlier_pv):
        for x in range(2):
            z_ref, u_ref, lb_ref = z_bufs[par][x], u_bufs[par][x], lb_bufs[par][x]
            for r0 in range(0, tq, STRIP):
                usum = jnp.zeros((STRIP, LANES), F32)
                if earlier_pv is not None:
                    usum = _zero_after(earlier_pv[x][r0:r0 + STRIP, :])
                if r0 >= PACE_LAG:
                    rows = slice(r0 - PACE_LAG, r0 - PACE_LAG + STRIP)
                    usum = usum + _zero_after(z_bufs[1 - par][x][rows, :LANES])
                for c in range(nc):
                    st = chunk_state(r0, c)
                    if st is False:
                        u_ref[r0:r0 + STRIP, c * LANES:(c + 1) * LANES] = jnp.zeros((STRIP, LANES), BF16)
                        lb_ref[r0:r0 + STRIP, c * LANES:(c + 1) * LANES] = jnp.zeros((STRIP, LANES), F32)
                        continue
                    z = _chunk(z_ref, r0, c)
                    zb = z.astype(BF16)
                    corr = jnp.log(1.0 + jnp.exp2(-jnp.abs(zb))) * LOG2E
                    sp = jnp.maximum(z, 0.0) + corr.astype(F32)
                    u = sp if st is None else jnp.where(st, sp, 0.0)
                    u_ref[r0:r0 + STRIP, c * LANES:(c + 1) * LANES] = u.astype(BF16)
                    lb_ref[r0:r0 + STRIP, c * LANES:(c + 1) * LANES] = z - sp
                    usum = usum + u
                tot_refs[x][r0:r0 + STRIP, :] = (jnp.sum(usum, axis=1, keepdims=True)
                                                 + jnp.zeros((STRIP, LANES), F32))

    def weights_and_accumulate(j, par, masked):
        vb = v_ref[pl.ds(pl.multiple_of(j * tk, tk), tk), :]
        tri_stack = tri_ref[...]
        pvs = []
        for x in range(2):
            u_ref, lb_ref, a_ref = u_bufs[par][x], lb_bufs[par][x], a_bufs[par][x]
            suffix = (_dot(u_ref[...], tri_stack), _dot(u_ref[:, TRI:], tri_stack[:TRI]))
            carry = carry_refs[x][...]
            for c in range(nc):
                cols = slice(c * LANES, (c + 1) * LANES)
                blk, off = divmod(c * LANES, TRI)
                suf = suffix[blk][:, off:off + LANES]
                a = jnp.exp2(lb_ref[:, cols] - suf - carry)
                if masked:
                    row = lax.broadcasted_iota(jnp.int32, (tq, LANES), 0)
                    col = c * LANES + lax.broadcasted_iota(jnp.int32, (tq, LANES), 1)
                    a = jnp.where(col < row, a, 0.0)
                a_ref[:, cols] = a.astype(BF16)
            pv = _dot(a_ref[...], vb)
            acc_refs[x][...] += pv
            carry_refs[x][...] = carry + tot_refs[x][...]
            pvs.append(pv)
        return pvs

    def step(j, par, masked, earlier_pv=None):
        scores(jnp.maximum(j - 1, 0), 1 - par)
        gates(par, _causal_chunk_state(True) if masked else _all_visible, earlier_pv)
        return weights_and_accumulate(j, par, masked)

    scores(qi, 0)
    step(qi, 0, True)

    def pair(i, c):
        j = qi - 1 - 2 * i
        pvs = step(j, 1, False)
        step(j - 1, 0, False, pvs)
        return c

    lax.fori_loop(0, qi // 2, pair, 0)

    @pl.when(qi % 2 == 1)
    def _():
        step(0, 1, False)

    lane = lax.broadcasted_iota(jnp.int32, (1, LANES), 1)
    o_ref[...] = jnp.where(lane < HEAD_DIM, acc_refs[0][...], acc_refs[1][...]).astype(BF16)


def _sb_call(sq, sk, sv, tq, tk):
    S = sq.shape[0]
    npair = sq.shape[1] // LANES
    qspec = pl.BlockSpec((tq, LANES), lambda p, i: (i, p))
    kvspec = pl.BlockSpec((S, LANES), lambda p, i: (0, p))
    tri = jnp.asarray(np.concatenate([np.tril(np.ones((TRI, TRI), np.float32), -1),
                                      np.ones((TRI, TRI), np.float32)]), BF16)
    return pl.pallas_call(
        functools.partial(_sb_kernel, tq, tk),
        grid=(npair, S // tq),
        in_specs=[qspec, kvspec, kvspec, pl.BlockSpec((2 * TRI, TRI), lambda p, i: (0, 0))],
        out_specs=qspec,
        out_shape=jax.ShapeDtypeStruct(sq.shape, BF16),
        scratch_shapes=2 * [pltpu.VMEM((tq, tk), F32), pltpu.VMEM((tq, tk), F32),
                            pltpu.VMEM((tq, tk), BF16), pltpu.VMEM((tq, tk), BF16),
                            pltpu.VMEM((tq, tk), F32), pltpu.VMEM((tq, tk), F32),
                            pltpu.VMEM((tq, tk), BF16), pltpu.VMEM((tq, tk), BF16),
                            pltpu.VMEM((tq, LANES), F32), pltpu.VMEM((tq, LANES), F32),
                            pltpu.VMEM((tq, LANES), F32)],
        compiler_params=_cparams(("parallel", "arbitrary")),
        name="sb_attn",
    )(sq, sk, sv, tri)


def _merge_kernel(hb_ref, h_ref, od_ref, os_ref, of_ref, wg_ref, wud_ref, wus_ref, wuf_ref, wo_ref,
                  g_ref, b_ref, out_ref, outb_ref):
    x = hb_ref[...]
    D = x.shape[1]
    merged = None
    for c, (o_ref, wu_ref) in enumerate(((od_ref, wud_ref), (os_ref, wus_ref), (of_ref, wuf_ref))):
        gate = _sigmoid(_dot(x, wg_ref[:, c * D:(c + 1) * D]))
        term = gate * _dot(o_ref[...], wu_ref[...])
        merged = term if merged is None else merged + term
    y = _dot(merged.astype(BF16), wo_ref[...])
    hn = _layer_norm(ALPHA * h_ref[...] + y, g_ref[...], b_ref[...])
    out_ref[...] = hn
    outb_ref[...] = hn.astype(BF16)


def _merge_call(hb, h, od, osb, of, wg, wud, wus, wuf, wo, g, b, tm):
    S, D = h.shape
    row = lambda a: pl.BlockSpec((tm, a.shape[1]), lambda i: (i, 0))
    full = lambda a: pl.BlockSpec(a.shape, lambda i: (0, 0))
    args = (hb, h, od, osb, of, wg, wud, wus, wuf, wo, g, b)
    in_specs = [row(a) for a in args[:5]] + [full(a) for a in args[5:]]
    return pl.pallas_call(
        _merge_kernel,
        grid=(S // tm,),
        in_specs=in_specs,
        out_specs=[row(h), row(h)],
        out_shape=[jax.ShapeDtypeStruct((S, D), F32), jax.ShapeDtypeStruct((S, D), BF16)],
        compiler_params=_cparams(("parallel",)),
        name="merge",
    )(*args)


def _swiglu_hidden(x, wg, wu):
    gate = _dot(x, wg)
    return (gate * _sigmoid(gate)) * _dot(x, wu)


def _ffn_kernel(hb_ref, h_ref, wg_ref, wu_ref, wd_ref, g_ref, b_ref, out_ref, outb_ref, acc_ref):
    f = pl.program_id(1)

    @pl.when(f == 0)
    def _():
        acc_ref[...] = jnp.zeros(acc_ref.shape, F32)

    hid = _swiglu_hidden(hb_ref[...], wg_ref[...], wu_ref[...])
    acc_ref[...] += _dot(hid.astype(BF16), wd_ref[...])

    @pl.when(f == pl.num_programs(1) - 1)
    def _():
        hn = _layer_norm(ALPHA * h_ref[...] + acc_ref[...], g_ref[...], b_ref[...])
        out_ref[...] = hn
        outb_ref[...] = hn.astype(BF16)


def _ffn_call(hb, h, wg, wu, wd, g, b, tm, tf):
    S, D = h.shape
    F = wg.shape[1]
    row = pl.BlockSpec((tm, D), lambda i, f: (i, 0))
    vec = pl.BlockSpec((1, D), lambda i, f: (0, 0))
    return pl.pallas_call(
        _ffn_kernel,
        grid=(S // tm, F // tf),
        in_specs=[row, row,
                  pl.BlockSpec((D, tf), lambda i, f: (0, f)),
                  pl.BlockSpec((D, tf), lambda i, f: (0, f)),
                  pl.BlockSpec((tf, D), lambda i, f: (f, 0)),
                  vec, vec],
        out_specs=[row, row],
        out_shape=[jax.ShapeDtypeStruct((S, D), F32), jax.ShapeDtypeStruct((S, D), BF16)],
        scratch_shapes=[pltpu.VMEM((tm, D), F32)],
        compiler_params=_cparams(("parallel", "arbitrary")),
        name="ffn",
    )(hb, h, wg, wu, wd, g, b)


MOE_TILE = 512
DMA_UNROLL = 8


def _route_kernel(h_ref, w_ref, b_ref, sel_ref, i1_ref, i2_ref, w1_ref, w2_ref):
    logits = jnp.dot(h_ref[...], w_ref[...], precision=lax.Precision.HIGHEST,
                     preferred_element_type=F32) + b_ref[...]
    lane = lax.broadcasted_iota(jnp.int32, logits.shape, 1)
    logits = jnp.where(lane < N_EXPERTS, logits, NEG)
    v1 = jnp.max(logits, axis=1, keepdims=True)
    i1 = jnp.min(jnp.where(logits == v1, lane, LANES), axis=1, keepdims=True)
    rest = jnp.where(lane == i1, NEG, logits)
    v2 = jnp.max(rest, axis=1, keepdims=True)
    i2 = jnp.min(jnp.where(rest == v2, lane, LANES), axis=1, keepdims=True)
    e2 = jnp.exp(v2 - v1)
    zero = jnp.zeros(logits.shape, F32)
    sel_ref[...] = jnp.where((lane == i1) | (lane == i2), 1.0, 0.0).astype(BF16)
    i1_ref[...] = i1 + jnp.zeros(logits.shape, jnp.int32)
    i2_ref[...] = i2 + jnp.zeros(logits.shape, jnp.int32)
    w1_ref[...] = 1.0 / (1.0 + e2) + zero
    w2_ref[...] = e2 / (1.0 + e2) + zero


def _route_call(h, w_router, b_router, tm):
    S, D = h.shape
    wpad = jnp.zeros((D, LANES), F32).at[:, :N_EXPERTS].set(w_router)
    bpad = jnp.zeros((1, LANES), F32).at[0, :N_EXPERTS].set(b_router)
    row = pl.BlockSpec((tm, LANES), lambda i: (i, 0))
    return pl.pallas_call(
        _route_kernel,
        grid=(S // tm,),
        in_specs=[pl.BlockSpec((tm, D), lambda i: (i, 0)),
                  pl.BlockSpec((D, LANES), lambda i: (0, 0)),
                  pl.BlockSpec((1, LANES), lambda i: (0, 0))],
        out_specs=[row] * 5,
        out_shape=[jax.ShapeDtypeStruct((S, LANES), BF16),
                   jax.ShapeDtypeStruct((S, LANES), jnp.int32),
                   jax.ShapeDtypeStruct((S, LANES), jnp.int32),
                   jax.ShapeDtypeStruct((S, LANES), F32),
                   jax.ShapeDtypeStruct((S, LANES), F32)],
        compiler_params=_cparams(("parallel",)),
        name="route",
    )(h, wpad, bpad)


def _rank_kernel(sel_ref, i1_ref, i2_ref, tril_ref, rank_ref, cnt_ref):
    @pl.when(pl.program_id(0) == 0)
    def _():
        cnt_ref[...] = jnp.zeros(cnt_ref.shape, F32)

    sel = sel_ref[...]
    rank = _dot(tril_ref[...], sel) + cnt_ref[...]
    lane = lax.broadcasted_iota(jnp.int32, rank.shape, 1)
    r1 = jnp.sum(jnp.where(lane == i1_ref[...], rank, 0.0), axis=1, keepdims=True)
    r2 = jnp.sum(jnp.where(lane == i2_ref[...], rank, 0.0), axis=1, keepdims=True)
    rank_ref[...] = jnp.where(lane == 0, r1, jnp.where(lane == 1, r2, 0.0))
    cnt_ref[...] += jnp.sum(sel.astype(F32), axis=0, keepdims=True)


def _rank_call(sel, i1b, i2b, tb):
    S = sel.shape[0]
    tril = jnp.asarray(np.tril(np.ones((tb, tb), np.float32), -1), BF16)
    row = pl.BlockSpec((tb, LANES), lambda i: (i, 0))
    return pl.pallas_call(
        _rank_kernel,
        grid=(S // tb,),
        in_specs=[row, row, row, pl.BlockSpec((tb, tb), lambda i: (0, 0))],
        out_specs=[row, pl.BlockSpec((1, LANES), lambda i: (0, 0))],
        out_shape=[jax.ShapeDtypeStruct((S, LANES), F32), jax.ShapeDtypeStruct((1, LANES), F32)],
        compiler_params=_cparams(("arbitrary",)),
        name="rank",
    )(sel, i1b, i2b, tril)


def _row_copy(src_ref, s, dst_ref, d, sem):
    return pltpu.make_async_copy(src_ref.at[pl.ds(s, 1)], dst_ref.at[pl.ds(d, 1)], sem)


def _dispatch_kernel(tb, tmx, n_tok, dest_ref, ends_ref, h_ref, xs_hbm, zero_ref, sem, zsem):
    base = pl.program_id(0) * tb

    @pl.when(pl.program_id(0) == 0)
    def _():
        zero_ref[...] = jnp.zeros(zero_ref.shape, F32)

        def fill(e, wait):
            start = 0 if e == 0 else ends_ref[e - 1]
            end = ends_ref[e]

            @pl.when(end > start)
            def _():
                cp = pltpu.make_async_copy(
                    zero_ref, xs_hbm.at[pl.ds(pl.multiple_of(end - tmx, tmx), tmx)], zsem)
                if wait:
                    cp.wait()
                else:
                    cp.start()

        def fill_tail(k, wait):
            start = ends_ref[N_EXPERTS - 1] + k * tmx

            @pl.when(start < xs_hbm.shape[0])
            def _():
                cp = pltpu.make_async_copy(
                    zero_ref, xs_hbm.at[pl.ds(pl.multiple_of(start, tmx), tmx)], zsem)
                if wait:
                    cp.wait()
                else:
                    cp.start()

        for wait in (False, True):
            for e in range(N_EXPERTS):
                fill(e, wait)
                fill_tail(e, wait)

    def issue(t, c):
        _row_copy(h_ref, t, xs_hbm, dest_ref[base + t], sem).start(priority=0)
        _row_copy(h_ref, t, xs_hbm, dest_ref[n_tok + base + t], sem).start(priority=1)
        return c

    lax.fori_loop(0, tb, issue, 0, unroll=DMA_UNROLL)

    def drain(t, c):
        _row_copy(h_ref, 0, xs_hbm, 0, sem).wait()
        _row_copy(h_ref, 0, xs_hbm, 0, sem).wait()
        return c

    lax.fori_loop(0, tb, drain, 0, unroll=DMA_UNROLL)


def _dispatch_call(dest, ends, h, n_rows, tb, tmx):
    S, D = h.shape
    return pl.pallas_call(
        functools.partial(_dispatch_kernel, tb, tmx, S),
        grid_spec=pltpu.PrefetchScalarGridSpec(
            num_scalar_prefetch=2,
            grid=(S // tb,),
            in_specs=[pl.BlockSpec((tb, D), lambda i, d, e: (i, 0))],
            out_specs=pl.BlockSpec(memory_space=pl.ANY),
            scratch_shapes=[pltpu.VMEM((tmx, D), F32), pltpu.SemaphoreType.DMA,
                            pltpu.SemaphoreType.DMA],
        ),
        out_shape=jax.ShapeDtypeStruct((n_rows, D), F32),
        compiler_params=_cparams(("arbitrary",)),
        name="dispatch",
    )(dest, ends, h)


def _experts_kernel(te_ref, nu_ref, xs_ref, wg_ref, wu_ref, wd_ref, ys_ref, xb_ref, acc_ref):
    i = pl.program_id(0)
    f = pl.program_id(1)
    last_f = pl.num_programs(1) - 1
    used = i < nu_ref[0]

    @pl.when(used & (f == 0))
    def _():
        xb_ref[...] = xs_ref[...].astype(BF16)

    @pl.when(used)
    def _():
        hid = _swiglu_hidden(xb_ref[...], wg_ref[0], wu_ref[0])
        part = _dot(hid.astype(BF16), wd_ref[0])

        @pl.when(f == 0)
        def _():
            acc_ref[...] = part

        @pl.when(f > 0)
        def _():
            acc_ref[...] += part

    @pl.when(used & (f == last_f))
    def _():
        ys_ref[...] = acc_ref[...]

    @pl.when(jnp.logical_not(used) & (f == last_f))
    def _():
        ys_ref[...] = jnp.zeros(ys_ref.shape, F32)


def _experts_call(tile_expert, n_used, xs, wg, wu, wd, tmx, tf):
    P, D = xs.shape
    F = wg.shape[2]
    nf = F // tf

    def fsel(i, f, nu):
        return jnp.where(i < nu[0], f, nf - 1)

    return pl.pallas_call(
        _experts_kernel,
        grid_spec=pltpu.PrefetchScalarGridSpec(
            num_scalar_prefetch=2,
            grid=(P // tmx, nf),
            in_specs=[pl.BlockSpec((tmx, D), lambda i, f, te, nu: (jnp.where(i < nu[0], i, 0), 0)),
                      pl.BlockSpec((1, D, tf), lambda i, f, te, nu: (te[i], 0, fsel(i, f, nu))),
                      pl.BlockSpec((1, D, tf), lambda i, f, te, nu: (te[i], 0, fsel(i, f, nu))),
                      pl.BlockSpec((1, tf, D), lambda i, f, te, nu: (te[i], fsel(i, f, nu), 0))],
            out_specs=pl.BlockSpec((tmx, D), lambda i, f, te, nu: (i, 0)),
            scratch_shapes=[pltpu.VMEM((tmx, D), BF16), pltpu.VMEM((tmx, D), F32)],
        ),
        out_shape=jax.ShapeDtypeStruct((P, D), F32),
        compiler_params=_cparams(("arbitrary", "arbitrary")),
        name="experts",
    )(tile_expert, n_used, xs, wg, wu, wd)


def _combine_kernel(tm, n_tok, dest_ref, ys_hbm, h_ref, w1_ref, w2_ref, g_ref, b_ref,
                    out_ref, outb_ref, buf_ref, sem):
    i = pl.program_id(0)
    slot = i % 2

    def gather(tile, slot_, start):
        base = tile * tm

        def one(t, c):
            for k in range(2):
                src_row = dest_ref[k * n_tok + base + t] if start else 0
                cp = pltpu.make_async_copy(ys_hbm.at[pl.ds(src_row, 1)],
                                           buf_ref.at[slot_, k, pl.ds(t, 1)], sem.at[slot_])
                if start:
                    cp.start(priority=k)
                else:
                    cp.wait()
            return c

        lax.fori_loop(0, tm, one, 0, unroll=DMA_UNROLL)

    @pl.when(i == 0)
    def _():
        gather(0, 0, True)

    @pl.when(i + 1 < pl.num_programs(0))
    def _():
        gather(i + 1, 1 - slot, True)

    gather(i, slot, False)

    w1 = w1_ref[...]
    w2 = w2_ref[...]
    y = jnp.concatenate(
        [w1 * buf_ref[slot, 0, :, c * LANES:(c + 1) * LANES]
         + w2 * buf_ref[slot, 1, :, c * LANES:(c + 1) * LANES]
         for c in range(h_ref.shape[1] // LANES)], axis=1)
    hn = _layer_norm(ALPHA * h_ref[...] + y, g_ref[...], b_ref[...])
    out_ref[...] = hn
    outb_ref[...] = hn.astype(BF16)


def _combine_call(dest, ys, h, w1b, w2b, g, b, tm):
    S, D = h.shape
    row = pl.BlockSpec((tm, D), lambda i, d: (i, 0))
    lrow = pl.BlockSpec((tm, LANES), lambda i, d: (i, 0))
    vec = pl.BlockSpec((1, D), lambda i, d: (0, 0))
    return pl.pallas_call(
        functools.partial(_combine_kernel, tm, S),
        grid_spec=pltpu.PrefetchScalarGridSpec(
            num_scalar_prefetch=1,
            grid=(S // tm,),
            in_specs=[pl.BlockSpec(memory_space=pl.ANY), row, lrow, lrow, vec, vec],
            out_specs=[row, row],
            scratch_shapes=[pltpu.VMEM((2, 2, tm, D), F32), pltpu.SemaphoreType.DMA((2,))],
        ),
        out_shape=[jax.ShapeDtypeStruct((S, D), F32), jax.ShapeDtypeStruct((S, D), BF16)],
        compiler_params=_cparams(("arbitrary",)),
        name="combine",
    )(dest, ys, h, w1b, w2b, g, b)


def _moe_call(h, w_router, b_router, wg, wu, wd, g, b, tm, tf):
    S, D = h.shape
    tmx = MOE_TILE
    sel, i1b, i2b, w1b, w2b = _route_call(h, w_router, b_router, tm)
    ranks, counts = _rank_call(sel, i1b, i2b, min(256, S))
    cnt = counts[0, :N_EXPERTS].astype(jnp.int32)
    padded = ((cnt + tmx - 1) // tmx) * tmx
    ends = jnp.cumsum(padded)
    off = ends - padded
    dest = jnp.concatenate([off[i1b[:, 0]] + ranks[:, 0].astype(jnp.int32),
                            off[i2b[:, 0]] + ranks[:, 1].astype(jnp.int32)])
    n_tiles = (2 * S) // tmx + N_EXPERTS
    tile_start = jnp.arange(n_tiles, dtype=jnp.int32) * tmx
    tile_expert = jnp.minimum(jnp.sum(tile_start[:, None] >= ends[None, :], axis=1),
                              N_EXPERTS - 1).astype(jnp.int32)
    n_used = (ends[-1] // tmx).astype(jnp.int32).reshape(1)
    xs = _dispatch_call(dest, ends.astype(jnp.int32), h, n_tiles * tmx, tm, tmx)
    ys = _experts_call(tile_expert, n_used, xs, wg, wu, wd, tmx, tf)
    return _combine_call(dest, ys, h, w1b, w2b, g, b, min(256, S))


def _rope_tables(S):
    half = HEAD_DIM // 2
    pos = jnp.arange(S, dtype=F32)
    inv = ROPE_THETA ** (-jnp.arange(half, dtype=F32) / half)
    ang = pos[:, None] * inv[None, :]
    cos, sin = jnp.cos(ang), jnp.sin(ang)
    reps = LANES // HEAD_DIM
    cos_t = jnp.tile(jnp.concatenate([cos, cos], axis=1), (1, reps))
    sin_t = jnp.tile(jnp.concatenate([-sin, sin], axis=1), (1, reps))
    return cos_t, sin_t


def _pick(S, pref):
    t = min(pref, S)
    assert S % t == 0
    return t


def _forward(x, ln_in_g, ln_in_b, w_in, b_forget, lam_q1, lam_k1, lam_q2, lam_k2,
             diff_norm_g, w_up_diff, w_up_sb, w_up_fox, w_out, ln_mix_g, ln_mix_b,
             ln_ffn_g, ln_ffn_b, w_gate_dense, w_up_dense, w_down_dense, w_router,
             b_router, w_gate_moe, w_up_moe, w_down_moe):
    B, S, D = x.shape
    assert B == 1 and D == D_MODEL and S % LANES == 0
    depth = w_in.shape[0]
    tm = _pick(S, 512)
    tq = _pick(S, 512)
    tm_ffn = _pick(S, 1024)
    tf = D_FF // 7
    tf_moe = D_FF // 2
    vec = lambda a: a.reshape(1, -1).astype(F32)

    cos_t, sin_t = _rope_tables(S)
    h, hb = _ln_call(x.reshape(S, D), ln_in_g, ln_in_b, tm)
    w_atts, w_forgets, w_gate_logits = _wsplit_call(w_in, 128)
    for l in range(depth):
        w_att, w_forget, w_gate_logit = w_atts[l], w_forgets[l], w_gate_logits[l]
        fbias = jnp.pad(b_forget[l].astype(F32), (0, LANES - N_HEADS_FOX)).reshape(1, LANES)
        dq, dk, dv, sq, sk, sv, fq, fk, fv, logf = _proj_call(hb, w_att, w_forget, cos_t, sin_t,
                                                              fbias, tm)

        lf = logf[:, :N_HEADS_FOX].T.reshape(N_HEADS_FOX, S // LANES, LANES)
        fcum = _fcum_call(lf)

        lam_init = 0.8 - 0.6 * math.exp(-0.3 * l)
        lamv = jnp.stack([lam_q1[l], lam_k1[l], lam_q2[l], lam_k2[l]]).astype(F32)
        o_diff = _diff_call(dq, dk, dv, lamv, vec(diff_norm_g[l]), lam_init, tq)
        o_sb = _sb_call(sq, sk, sv, tq, tq)
        o_fox = _fox_call(fq, fk, fv, fcum, tq)

        h, hb = _merge_call(hb, h, o_diff, o_sb, o_fox, w_gate_logit,
                            w_up_diff[l].astype(BF16), w_up_sb[l].astype(BF16),
                            w_up_fox[l].astype(BF16), w_out[l].astype(BF16),
                            vec(ln_mix_g[l]), vec(ln_mix_b[l]), tm)
        j = l // 2
        if l % 2 == 0:
            h, hb = _ffn_call(hb, h, w_gate_dense[j].astype(BF16), w_up_dense[j].astype(BF16),
                              w_down_dense[j].astype(BF16), vec(ln_ffn_g[l]), vec(ln_ffn_b[l]),
                              tm_ffn, tf)
        else:
            h, hb = _moe_call(h, w_router[j], b_router[j], w_gate_moe[j].astype(BF16),
                              w_up_moe[j].astype(BF16), w_down_moe[j].astype(BF16),
                              vec(ln_ffn_g[l]), vec(ln_ffn_b[l]), tm, tf_moe)
    return h.reshape(B, S, D)


def kernel(x, ln_in_g, ln_in_b, w_in, b_forget, lam_q1, lam_k1, lam_q2, lam_k2, diff_norm_g,
           w_up_diff, w_up_sb, w_up_fox, w_out, ln_mix_g, ln_mix_b, ln_ffn_g, ln_ffn_b,
           w_gate_dense, w_up_dense, w_down_dense, w_router, b_router, w_gate_moe, w_up_moe,
           w_down_moe):
    return _forward(x, ln_in_g, ln_in_b, w_in, b_forget, lam_q1, lam_k1, lam_q2, lam_k2,
                    diff_norm_g, w_up_diff, w_up_sb, w_up_fox, w_out, ln_mix_g, ln_mix_b,
                    ln_ffn_g, ln_ffn_b, w_gate_dense, w_up_dense, w_down_dense, w_router,
                    b_router, w_gate_moe, w_up_moe, w_down_moe)
```

```python
import functools
import math

import jax
import jax.numpy as jnp
import numpy as np
from jax import lax
from jax.experimental import pallas as pl
from jax.experimental.pallas import tpu as pltpu

D_MODEL = 1024
DEPTH = 2
CHUNK = 64
HEAD_DIM = 64
N_HEADS_DIFF = 4
N_HEADS_SB = 4
N_HEADS_FOX = 4
WIDTH_DIFF = N_HEADS_DIFF * 2 * HEAD_DIM
WIDTH_SB = N_HEADS_SB * HEAD_DIM
WIDTH_FOX = N_HEADS_FOX * HEAD_DIM
ROPE_THETA = 10000.0
N_EXPERTS = 8
D_FF = 3584
ALPHA = (2 * DEPTH) ** 0.25
Q_SCALE = HEAD_DIM ** -0.5
LOG2E = 1.4426950408889634
N_ATT = 3 * WIDTH_DIFF + 3 * WIDTH_SB + 3 * WIDTH_FOX
LANES = 128
NEG = -1e30

F32 = jnp.float32
BF16 = jnp.bfloat16

VMEM_LIMIT = 56 * 1024 * 1024


def _cparams(sem):
    return pltpu.CompilerParams(dimension_semantics=sem, vmem_limit_bytes=VMEM_LIMIT)


def _layer_norm(z, g, b, eps=1e-5):
    mu = jnp.mean(z, axis=-1, keepdims=True)
    zc = z - mu
    var = jnp.mean(zc * zc, axis=-1, keepdims=True)
    return zc * lax.rsqrt(var + eps) * g + b


def _sigmoid(x):
    return 1.0 / (1.0 + jnp.exp(-x))


def _softplus(x):
    return jnp.maximum(x, 0.0) + jnp.log1p(jnp.exp(-jnp.abs(x)))


def _dot(a, b):
    return jnp.dot(a, b, preferred_element_type=F32)


def _dot_nt(a, b):
    return lax.dot_general(a, b, (((1,), (1,)), ((), ())), preferred_element_type=F32)


def _split3(x):
    x1 = x.astype(BF16)
    r = x - x1.astype(F32)
    x2 = r.astype(BF16)
    x3 = (r - x2.astype(F32)).astype(BF16)
    return x1, x2, x3


def _ln_kernel(x_ref, g_ref, b_ref, h_ref, hb_ref):
    h = _layer_norm(x_ref[...], g_ref[...], b_ref[...])
    h_ref[...] = h
    hb_ref[...] = h.astype(BF16)


def _ln_call(x, g, b, tm):
    S, D = x.shape
    row = pl.BlockSpec((tm, D), lambda i: (i, 0))
    vec = pl.BlockSpec((1, D), lambda i: (0, 0))
    return pl.pallas_call(
        _ln_kernel,
        grid=(S // tm,),
        in_specs=[row, vec, vec],
        out_specs=[row, row],
        out_shape=[jax.ShapeDtypeStruct((S, D), F32), jax.ShapeDtypeStruct((S, D), BF16)],
        compiler_params=_cparams(("parallel",)),
        name="ln_in",
    )(x, g.reshape(1, D), b.reshape(1, D))


def _wsplit_kernel(w_ref, att_ref, fgt_ref, gate_ref):
    att_ref[0] = w_ref[0, :, :N_ATT].astype(BF16)
    lane = lax.broadcasted_iota(jnp.int32, (1, LANES), 1)
    fgt_ref[0] = jnp.where(lane < N_HEADS_FOX, w_ref[0, :, N_ATT:N_ATT + LANES], 0.0).astype(BF16)
    gate_ref[0] = w_ref[0, :, N_ATT + N_HEADS_FOX:].astype(BF16)


def _wsplit_call(w_in, tr):
    depth, D, n_in = w_in.shape
    n_gate = n_in - N_ATT - N_HEADS_FOX
    spec = lambda n: pl.BlockSpec((1, tr, n), lambda l, i: (l, i, 0))
    return pl.pallas_call(
        _wsplit_kernel,
        grid=(depth, D // tr),
        in_specs=[spec(n_in)],
        out_specs=[spec(N_ATT), spec(LANES), spec(n_gate)],
        out_shape=[jax.ShapeDtypeStruct((depth, D, N_ATT), BF16),
                   jax.ShapeDtypeStruct((depth, D, LANES), BF16),
                   jax.ShapeDtypeStruct((depth, D, n_gate), BF16)],
        compiler_params=_cparams(("parallel", "parallel")),
        name="wsplit",
    )(w_in)


def _proj_kernel(hb_ref, w_ref, wf_ref, cos_ref, sin_ref, fb_ref,
                 dq_ref, dk_ref, dv_ref, sq_ref, sk_ref, sv_ref, fq_ref, fk_ref, fv_ref, lf_ref):
    x = hb_ref[...]
    cos = cos_ref[...]
    sin = sin_ref[...]
    lane = lax.broadcasted_iota(jnp.int32, (1, LANES), 1)
    lower = (lane & (HEAD_DIM // 2)) == 0

    def mm(c0, c1):
        return _dot(x, w_ref[:, c0:c1])

    def rope_store(y, out_ref, scale):
        for g in range(y.shape[1] // LANES):
            yg = y[:, g * LANES:(g + 1) * LANES]
            partner = jnp.where(lower, pltpu.roll(yg, LANES - HEAD_DIM // 2, 1),
                                pltpu.roll(yg, HEAD_DIM // 2, 1))
            r = yg * cos + partner * sin
            if scale != 1.0:
                r = r * scale
            out_ref[:, g * LANES:(g + 1) * LANES] = r.astype(BF16)

    rope_store(mm(0, 512), dq_ref, Q_SCALE * LOG2E)
    rope_store(mm(512, 1024), dk_ref, 1.0)
    dv_ref[...] = mm(1024, 1536).astype(BF16)
    sq_ref[...] = (mm(1536, 1792) * (Q_SCALE * LOG2E)).astype(BF16)
    sk_ref[...] = mm(1792, 2048).astype(BF16)
    sv_ref[...] = mm(2048, 2304).astype(BF16)
    fq_ref[...] = (mm(2304, 2560) * (Q_SCALE * LOG2E)).astype(BF16)
    fk_ref[...] = mm(2560, 2816).astype(BF16)
    fv_ref[...] = mm(2816, 3072).astype(BF16)
    lf_ref[...] = -_softplus(-(_dot(x, wf_ref[...]) + fb_ref[...]))


def _proj_call(hb, w_att, w_forget, cos, sin, fbias, tm):
    S, D = hb.shape
    row = lambda n: pl.BlockSpec((tm, n), lambda i: (i, 0))
    full = lambda a: pl.BlockSpec(a.shape, lambda i: (0, 0))
    widths = [512, 512, 512, 256, 256, 256, 256, 256, 256]
    out_shape = [jax.ShapeDtypeStruct((S, n), BF16) for n in widths]
    out_shape.append(jax.ShapeDtypeStruct((S, LANES), F32))
    return pl.pallas_call(
        _proj_kernel,
        grid=(S // tm,),
        in_specs=[row(D), full(w_att), full(w_forget), row(LANES), row(LANES), full(fbias)],
        out_specs=[row(n) for n in widths] + [row(LANES)],
        out_shape=out_shape,
        compiler_params=_cparams(("parallel",)),
        name="proj",
    )(hb, w_att, w_forget, cos, sin, fbias)


def _fcum_kernel(lf_ref, f_ref):
    nh, R, _ = lf_ref.shape
    r0 = lax.broadcasted_iota(jnp.int32, (LANES, LANES), 0)
    c0 = lax.broadcasted_iota(jnp.int32, (LANES, LANES), 1)
    upper = jnp.where(r0 <= c0, 1.0, 0.0).astype(BF16)
    ones = jnp.ones((LANES, LANES), BF16)
    r1 = lax.broadcasted_iota(jnp.int32, (R, R), 0)
    c1 = lax.broadcasted_iota(jnp.int32, (R, R), 1)
    below = jnp.where(c1 < r1, 1.0, 0.0).astype(BF16)
    for h in range(nh):
        parts = _split3(lf_ref[h])
        within = sum(_dot(p, upper) for p in parts)
        totals = sum(_dot(p, ones) for p in parts)
        offs = sum(_dot(below, t) for t in _split3(totals))
        f_ref[h] = within + offs


def _fcum_call(lf):
    return pl.pallas_call(
        _fcum_kernel,
        out_shape=jax.ShapeDtypeStruct(lf.shape, F32),
        name="fcum",
    )(lf)


def _split_q(q):
    lane = lax.broadcasted_iota(jnp.int32, (1, LANES), 1)
    lo = lane < HEAD_DIM
    zero = jnp.zeros_like(q)
    return jnp.where(lo, q, zero), jnp.where(lo, zero, q)


STRIP = 16
TRI = 256
PACE_LAG = 128


def _chunk(ref, r0, c):
    return ref[r0:r0 + STRIP, c * LANES:(c + 1) * LANES]


def _zero_after(x):
    half = jnp.uint32(16)
    bits = lax.shift_right_logical(lax.shift_right_logical(pltpu.bitcast(x, jnp.uint32), half), half)
    return pltpu.bitcast(bits, F32)


def _softmax_tile(s_ref, p_ref, m_ref, l_ref, chunk_state, pace, finish_strip):
    tq, tk = s_ref.shape
    for r0 in range(0, tq, STRIP):
        states = [chunk_state(r0, c) for c in range(tk // LANES)]
        sc = []
        for c, st in enumerate(states):
            if st is False:
                sc.append(None)
            elif st is None:
                sc.append(_chunk(s_ref, r0, c))
            else:
                sc.append(jnp.where(st, _chunk(s_ref, r0, c), NEG))
        live = [s for s in sc if s is not None]
        m_prev = m_ref[r0:r0 + STRIP, :]
        zero = pace(r0)
        if zero is not None:
            m_prev = m_prev + zero
        if live:
            mx = functools.reduce(jnp.maximum, live)
            m_new = jnp.maximum(m_prev, jnp.max(mx, axis=1, keepdims=True))
        else:
            m_new = m_prev
        alpha = jnp.exp2(m_prev - m_new)
        psum = jnp.zeros((STRIP, LANES), F32)
        for c, s in enumerate(sc):
            if s is None:
                p_ref[r0:r0 + STRIP, c * LANES:(c + 1) * LANES] = jnp.zeros((STRIP, LANES), BF16)
            else:
                p = jnp.exp2(s - m_new)
                psum = psum + p
                p_ref[r0:r0 + STRIP, c * LANES:(c + 1) * LANES] = p.astype(BF16)
        m_ref[r0:r0 + STRIP, :] = m_new
        l_ref[r0:r0 + STRIP, :] = alpha * l_ref[r0:r0 + STRIP, :] + psum
        finish_strip(r0, alpha)


def _all_visible(r0, c):
    return None


def _init_softmax_state(m_refs, l_refs, acc_refs):
    for m_ref, l_ref, acc_ref in zip(m_refs, l_refs, acc_refs):
        m_ref[...] = jnp.full(m_ref.shape, NEG, F32)
        l_ref[...] = jnp.zeros(l_ref.shape, F32)
        acc_ref[...] = jnp.zeros(acc_ref.shape, F32)


def _softmax_scratch(t):
    per_stream = [pltpu.VMEM((t, t), F32), pltpu.VMEM((t, t), F32),
                  pltpu.VMEM((t, t), BF16), pltpu.VMEM((t, t), BF16),
                  pltpu.VMEM((t, LANES), F32), pltpu.VMEM((t, LANES), F32),
                  pltpu.VMEM((t, LANES), F32), pltpu.VMEM((t, LANES), F32)]
    return per_stream + per_stream


def _softmax_state_refs(scratch):
    return scratch[4::8], scratch[5::8], scratch[6::8]


def _row_total(l_ref):
    return jnp.sum(l_ref[...], axis=1, keepdims=True)


def _sweep(qi, t, scores, v_ref, scratch, diag_chunk_state):
    s0, s1, p0, p1, m_refs, l_refs, acc_refs, al_refs = (scratch[i::8] for i in range(8))
    s_bufs, p_bufs = (s0, s1), (p0, p1)

    def step(j, par, chunk_state, prefetch, earlier_pv=None):
        s_next = scores(j + 1) if prefetch else None
        vb = v_ref[pl.ds(pl.multiple_of(j * t, t), t), :]
        pvs = []
        for x in range(2):
            if prefetch:
                s_bufs[1 - par][x][...] = s_next[x]

            def pace(r0, x=x):
                zero = None
                if earlier_pv is not None:
                    zero = _zero_after(earlier_pv[x][r0:r0 + STRIP, :])
                if prefetch and r0 >= PACE_LAG:
                    rows = slice(r0 - PACE_LAG, r0 - PACE_LAG + STRIP)
                    z2 = _zero_after(s_bufs[1 - par][x][rows, :LANES])
                    zero = z2 if zero is None else zero + z2
                return zero

            def finish_strip(r0, alpha, x=x):
                al_refs[x][r0:r0 + STRIP, :] = alpha

            _softmax_tile(s_bufs[par][x], p_bufs[par][x], m_refs[x], l_refs[x], chunk_state,
                          pace, finish_strip)
            pv = _dot(p_bufs[par][x][...], vb)
            acc_refs[x][...] = al_refs[x][...] * acc_refs[x][...] + pv
            pvs.append(pv)
        return pvs

    first = scores(0)
    for x in range(2):
        s0[x][...] = first[x]

    def pair(i, c):
        pvs = step(2 * i, 0, _all_visible, True)
        step(2 * i + 1, 1, _all_visible, True, pvs)
        return c

    lax.fori_loop(0, qi // 2, pair, 0)

    @pl.when(qi % 2 == 1)
    def _():
        pvs = step(qi - 1, 0, _all_visible, True)
        step(qi, 1, diag_chunk_state, False, pvs)

    @pl.when(qi % 2 == 0)
    def _():
        step(qi, 0, diag_chunk_state, False)


def _diff_chunk_state(col0):
    def state(r0, c):
        q_chunk = r0 // CHUNK
        k_lo = (col0 + c * LANES) // CHUNK
        k_hi = (col0 + c * LANES + LANES - 1) // CHUNK
        if k_hi <= q_chunk:
            return None
        if k_lo > q_chunk:
            return False
        return lax.broadcasted_iota(jnp.int32, (STRIP, LANES), 1) < CHUNK
    return state


def _diff_kernel(t, lam_init, q_ref, k_ref, v_ref, lamv_ref, g_ref, o_ref, *scratch):
    m_refs, l_refs, acc_refs = _softmax_state_refs(scratch)
    qi = pl.program_id(1)
    qs = _split_q(q_ref[...])
    _init_softmax_state(m_refs, l_refs, acc_refs)

    def scores(j):
        kb = k_ref[pl.ds(pl.multiple_of(j * t, t), t), :]
        return [_dot_nt(qs[x], kb) for x in range(2)]

    _sweep(qi, t, scores, v_ref, scratch, _diff_chunk_state(0))

    lv = lamv_ref[...]
    lam = (jnp.exp(jnp.sum(lv[0:1] * lv[1:2], axis=1, keepdims=True))
           - jnp.exp(jnp.sum(lv[2:3] * lv[3:4], axis=1, keepdims=True)) + lam_init)
    o = (acc_refs[0][...] / _row_total(l_refs[0])
         - lam * (acc_refs[1][...] / _row_total(l_refs[1])))
    o = o * lax.rsqrt(jnp.mean(o * o, axis=-1, keepdims=True) + 1e-6) * g_ref[...]
    o_ref[...] = (o * (1.0 - lam_init)).astype(BF16)


def _diff_call(dq, dk, dv, lamv, gnorm, lam_init, tq):
    S = dq.shape[0]
    nh = dq.shape[1] // LANES
    qspec = pl.BlockSpec((tq, LANES), lambda h, i: (i, h))
    kvspec = pl.BlockSpec((S, LANES), lambda h, i: (0, h))
    full = lambda a: pl.BlockSpec(a.shape, lambda h, i: (0, 0))
    return pl.pallas_call(
        functools.partial(_diff_kernel, tq, lam_init),
        grid=(nh, S // tq),
        in_specs=[qspec, kvspec, kvspec, full(lamv), full(gnorm)],
        out_specs=qspec,
        out_shape=jax.ShapeDtypeStruct(dq.shape, BF16),
        scratch_shapes=_softmax_scratch(tq),
        compiler_params=_cparams(("parallel", "arbitrary")),
        name="diff_attn",
    )(dq, dk, dv, lamv, gnorm)


def _causal_chunk_state(strict, col0=0):
    def state(r0, c):
        k_lo, k_hi = col0 + c * LANES, col0 + c * LANES + LANES - 1
        q_lo, q_hi = r0, r0 + STRIP - 1
        if k_hi < q_lo or (not strict and k_hi <= q_lo):
            return None
        if k_lo > q_hi or (strict and k_lo >= q_hi):
            return False
        row = r0 + lax.broadcasted_iota(jnp.int32, (STRIP, LANES), 0)
        col = k_lo + lax.broadcasted_iota(jnp.int32, (STRIP, LANES), 1)
        return col < row if strict else col <= row
    return state


def _fox_kernel(t, q_ref, k_ref, v_ref, f_ref, o_ref, *scratch):
    m_refs, l_refs, acc_refs = _softmax_state_refs(scratch)
    qi = pl.program_id(1)
    qs = _split_q(q_ref[...])
    _init_softmax_state(m_refs, l_refs, acc_refs)
    f0 = [f_ref[x, qi][:, 0:1] for x in range(2)]

    def scores(j):
        kb = k_ref[pl.ds(pl.multiple_of(j * t, t), t), :]
        return [_dot_nt(qs[x], kb) + (f0[x] - f_ref[x, j]) * LOG2E for x in range(2)]

    _sweep(qi, t, scores, v_ref, scratch, _causal_chunk_state(strict=False))

    lane = lax.broadcasted_iota(jnp.int32, (1, LANES), 1)
    o = jnp.where(lane < HEAD_DIM, acc_refs[0][...] / _row_total(l_refs[0]),
                  acc_refs[1][...] / _row_total(l_refs[1]))
    o_ref[...] = o.astype(BF16)


def _fox_call(fq, fk, fv, fcum, tq):
    S = fq.shape[0]
    npair = fq.shape[1] // LANES
    qspec = pl.BlockSpec((tq, LANES), lambda p, i: (i, p))
    kvspec = pl.BlockSpec((S, LANES), lambda p, i: (0, p))
    nh = fcum.shape[0]
    fspec = pl.BlockSpec((2, S // tq, 1, tq), lambda p, i: (p, 0, 0, 0))
    return pl.pallas_call(
        functools.partial(_fox_kernel, tq),
        grid=(npair, S // tq),
        in_specs=[qspec, kvspec, kvspec, fspec],
        out_specs=qspec,
        out_shape=jax.ShapeDtypeStruct(fq.shape, BF16),
        scratch_shapes=_softmax_scratch(tq),
        compiler_params=_cparams(("parallel", "arbitrary")),
        name="fox_attn",
    )(fq, fk, fv, fcum.reshape(nh, S // tq, 1, tq))


def _sb_kernel(tq, tk, q_ref, k_ref, v_ref, tri_ref, o_ref, *scratch):
    (z0, z1, u0, u1, lb0, lb1, a0, a1, tot_refs, carry_refs, acc_refs) = (scratch[i::11]
                                                                          for i in range(11))
    z_bufs, u_bufs, lb_bufs, a_bufs = (z0, z1), (u0, u1), (lb0, lb1), (a0, a1)
    qi = pl.program_id(1)
    qs = _split_q(q_ref[...])
    for x in range(2):
        carry_refs[x][...] = jnp.zeros((tq, LANES), F32)
        acc_refs[x][...] = jnp.zeros((tq, LANES), F32)
    assert tk == tq and tk == 2 * TRI
    nc = tk // LANES

    def scores(j, par):
        kb = k_ref[pl.ds(pl.multiple_of(j * tk, tk), tk), :]
        for x in range(2):
            z_bufs[par][x][...] = _dot_nt(qs[x], kb)

    def gates(par, chunk_state, earlier_pv):
        for x in range(2):
            z_ref, u_ref, lb_ref = z_bufs[par][x], u_bufs[par][x], lb_bufs[par][x]
            for r0 in range(0, tq, STRIP):
                usum = jnp.zeros((STRIP, LANES), F32)
                if earlier_pv is not None:
                    usum = _zero_after(earlier_pv[x][r0:r0 + STRIP, :])
                if r0 >= PACE_LAG:
                    rows = slice(r0 - PACE_LAG, r0 - PACE_LAG + STRIP)
                    usum = usum + _zero_after(z_bufs[1 - par][x][rows, :LANES])
                for c in range(nc):
                    st = chunk_state(r0, c)
                    if st is False:
                        u_ref[r0:r0 + STRIP, c * LANES:(c + 1) * LANES] = jnp.zeros((STRIP, LANES), BF16)
                        lb_ref[r0:r0 + STRIP, c * LANES:(c + 1) * LANES] = jnp.zeros((STRIP, LANES), F32)
                        continue
                    z = _chunk(z_ref, r0, c)
                    zb = z.astype(BF16)
                    corr = jnp.log(1.0 + jnp.exp2(-jnp.abs(zb))) * LOG2E
                    sp = jnp.maximum(z, 0.0) + corr.astype(F32)
                    u = sp if st is None else jnp.where(st, sp, 0.0)
                    u_ref[r0:r0 + STRIP, c * LANES:(c + 1) * LANES] = u.astype(BF16)
                    lb_ref[r0:r0 + STRIP, c * LANES:(c + 1) * LANES] = z - sp
                    usum = usum + u
                tot_refs[x][r0:r0 + STRIP, :] = (jnp.sum(usum, axis=1, keepdims=True)
                                                 + jnp.zeros((STRIP, LANES), F32))

    def weights_and_accumulate(j, par, masked):
        vb = v_ref[pl.ds(pl.multiple_of(j * tk, tk), tk), :]
        tri_stack = tri_ref[...]
        pvs = []
        for x in range(2):
            u_ref, lb_ref, a_ref = u_bufs[par][x], lb_bufs[par][x], a_bufs[par][x]
            suffix = (_dot(u_ref[...], tri_stack), _dot(u_ref[:, TRI:], tri_stack[:TRI]))
            carry = carry_refs[x][...]
            for c in range(nc):
                cols = slice(c * LANES, (c + 1) * LANES)
                blk, off = divmod(c * LANES, TRI)
                suf = suffix[blk][:, off:off + LANES]
                a = jnp.exp2(lb_ref[:, cols] - suf - carry)
                if masked:
                    row = lax.broadcasted_iota(jnp.int32, (tq, LANES), 0)
                    col = c * LANES + lax.broadcasted_iota(jnp.int32, (tq, LANES), 1)
                    a = jnp.where(col < row, a, 0.0)
                a_ref[:, cols] = a.astype(BF16)
            pv = _dot(a_ref[...], vb)
            acc_refs[x][...] += pv
            carry_refs[x][...] = carry + tot_refs[x][...]
            pvs.append(pv)
        return pvs

    def step(j, par, masked, earlier_pv=None):
        scores(jnp.maximum(j - 1, 0), 1 - par)
        gates(par, _causal_chunk_state(True) if masked else _all_visible, earlier_pv)
        return weights_and_accumulate(j, par, masked)

    scores(qi, 0)
    step(qi, 0, True)

    def pair(i, c):
        j = qi - 1 - 2 * i
        pvs = step(j, 1, False)
        step(j - 1, 0, False, pvs)
        return c

    lax.fori_loop(0, qi // 2, pair, 0)

    @pl.when(qi % 2 == 1)
    def _():
        step(0, 1, False)

    lane = lax.broadcasted_iota(jnp.int32, (1, LANES), 1)
    o_ref[...] = jnp.where(lane < HEAD_DIM, acc_refs[0][...], acc_refs[1][...]).astype(BF16)


def _sb_call(sq, sk, sv, tq, tk):
    S = sq.shape[0]
    npair = sq.shape[1] // LANES
    qspec = pl.BlockSpec((tq, LANES), lambda p, i: (i, p))
    kvspec = pl.BlockSpec((S, LANES), lambda p, i: (0, p))
    tri = jnp.asarray(np.concatenate([np.tril(np.ones((TRI, TRI), np.float32), -1),
                                      np.ones((TRI, TRI), np.float32)]), BF16)
    return pl.pallas_call(
        functools.partial(_sb_kernel, tq, tk),
        grid=(npair, S // tq),
        in_specs=[qspec, kvspec, kvspec, pl.BlockSpec((2 * TRI, TRI), lambda p, i: (0, 0))],
        out_specs=qspec,
        out_shape=jax.ShapeDtypeStruct(sq.shape, BF16),
        scratch_shapes=2 * [pltpu.VMEM((tq, tk), F32), pltpu.VMEM((tq, tk), F32),
                            pltpu.VMEM((tq, tk), BF16), pltpu.VMEM((tq, tk), BF16),
                            pltpu.VMEM((tq, tk), F32), pltpu.VMEM((tq, tk), F32),
                            pltpu.VMEM((tq, tk), BF16), pltpu.VMEM((tq, tk), BF16),
                            pltpu.VMEM((tq, LANES), F32), pltpu.VMEM((tq, LANES), F32),
                            pltpu.VMEM((tq, LANES), F32)],
        compiler_params=_cparams(("parallel", "arbitrary")),
        name="sb_attn",
    )(sq, sk, sv, tri)


def _merge_kernel(hb_ref, h_ref, od_ref, os_ref, of_ref, wg_ref, wud_ref, wus_ref, wuf_ref, wo_ref,
                  g_ref, b_ref, out_ref, outb_ref):
    x = hb_ref[...]
    D = x.shape[1]
    merged = None
    for c, (o_ref, wu_ref) in enumerate(((od_ref, wud_ref), (os_ref, wus_ref), (of_ref, wuf_ref))):
        gate = _sigmoid(_dot(x, wg_ref[:, c * D:(c + 1) * D]))
        term = gate * _dot(o_ref[...], wu_ref[...])
        merged = term if merged is None else merged + term
    y = _dot(merged.astype(BF16), wo_ref[...])
    hn = _layer_norm(ALPHA * h_ref[...] + y, g_ref[...], b_ref[...])
    out_ref[...] = hn
    outb_ref[...] = hn.astype(BF16)


def _merge_call(hb, h, od, osb, of, wg, wud, wus, wuf, wo, g, b, tm):
    S, D = h.shape
    row = lambda a: pl.BlockSpec((tm, a.shape[1]), lambda i: (i, 0))
    full = lambda a: pl.BlockSpec(a.shape, lambda i: (0, 0))
    args = (hb, h, od, osb, of, wg, wud, wus, wuf, wo, g, b)
    in_specs = [row(a) for a in args[:5]] + [full(a) for a in args[5:]]
    return pl.pallas_call(
        _merge_kernel,
        grid=(S // tm,),
        in_specs=in_specs,
        out_specs=[row(h), row(h)],
        out_shape=[jax.ShapeDtypeStruct((S, D), F32), jax.ShapeDtypeStruct((S, D), BF16)],
        compiler_params=_cparams(("parallel",)),
        name="merge",
    )(*args)


def _swiglu_hidden(x, wg, wu):
    gate = _dot(x, wg)
    return (gate * _sigmoid(gate)) * _dot(x, wu)


def _ffn_kernel(hb_ref, h_ref, wg_ref, wu_ref, wd_ref, g_ref, b_ref, out_ref, outb_ref, acc_ref):
    f = pl.program_id(1)

    @pl.when(f == 0)
    def _():
        acc_ref[...] = jnp.zeros(acc_ref.shape, F32)

    hid = _swiglu_hidden(hb_ref[...], wg_ref[...], wu_ref[...])
    acc_ref[...] += _dot(hid.astype(BF16), wd_ref[...])

    @pl.when(f == pl.num_programs(1) - 1)
    def _():
        hn = _layer_norm(ALPHA * h_ref[...] + acc_ref[...], g_ref[...], b_ref[...])
        out_ref[...] = hn
        outb_ref[...] = hn.astype(BF16)


def _ffn_call(hb, h, wg, wu, wd, g, b, tm, tf):
    S, D = h.shape
    F = wg.shape[1]
    row = pl.BlockSpec((tm, D), lambda i, f: (i, 0))
    vec = pl.BlockSpec((1, D), lambda i, f: (0, 0))
    return pl.pallas_call(
        _ffn_kernel,
        grid=(S // tm, F // tf),
        in_specs=[row, row,
                  pl.BlockSpec((D, tf), lambda i, f: (0, f)),
                  pl.BlockSpec((D, tf), lambda i, f: (0, f)),
                  pl.BlockSpec((tf, D), lambda i, f: (f, 0)),
                  vec, vec],
        out_specs=[row, row],
        out_shape=[jax.ShapeDtypeStruct((S, D), F32), jax.ShapeDtypeStruct((S, D), BF16)],
        scratch_shapes=[pltpu.VMEM((tm, D), F32)],
        compiler_params=_cparams(("parallel", "arbitrary")),
        name="ffn",
    )(hb, h, wg, wu, wd, g, b)


MOE_TILE = 512
DMA_UNROLL = 8


def _route_kernel(h_ref, w_ref, b_ref, sel_ref, i1_ref, i2_ref, w1_ref, w2_ref):
    logits = jnp.dot(h_ref[...], w_ref[...], precision=lax.Precision.HIGHEST,
                     preferred_element_type=F32) + b_ref[...]
    lane = lax.broadcasted_iota(jnp.int32, logits.shape, 1)
    logits = jnp.where(lane < N_EXPERTS, logits, NEG)
    v1 = jnp.max(logits, axis=1, keepdims=True)
    i1 = jnp.min(jnp.where(logits == v1, lane, LANES), axis=1, keepdims=True)
    rest = jnp.where(lane == i1, NEG, logits)
    v2 = jnp.max(rest, axis=1, keepdims=True)
    i2 = jnp.min(jnp.where(rest == v2, lane, LANES), axis=1, keepdims=True)
    e2 = jnp.exp(v2 - v1)
    zero = jnp.zeros(logits.shape, F32)
    sel_ref[...] = jnp.where((lane == i1) | (lane == i2), 1.0, 0.0).astype(BF16)
    i1_ref[...] = i1 + jnp.zeros(logits.shape, jnp.int32)
    i2_ref[...] = i2 + jnp.zeros(logits.shape, jnp.int32)
    w1_ref[...] = 1.0 / (1.0 + e2) + zero
    w2_ref[...] = e2 / (1.0 + e2) + zero


def _route_call(h, w_router, b_router, tm):
    S, D = h.shape
    wpad = jnp.zeros((D, LANES), F32).at[:, :N_EXPERTS].set(w_router)
    bpad = jnp.zeros((1, LANES), F32).at[0, :N_EXPERTS].set(b_router)
    row = pl.BlockSpec((tm, LANES), lambda i: (i, 0))
    return pl.pallas_call(
        _route_kernel,
        grid=(S // tm,),
        in_specs=[pl.BlockSpec((tm, D), lambda i: (i, 0)),
                  pl.BlockSpec((D, LANES), lambda i: (0, 0)),
                  pl.BlockSpec((1, LANES), lambda i: (0, 0))],
        out_specs=[row] * 5,
        out_shape=[jax.ShapeDtypeStruct((S, LANES), BF16),
                   jax.ShapeDtypeStruct((S, LANES), jnp.int32),
                   jax.ShapeDtypeStruct((S, LANES), jnp.int32),
                   jax.ShapeDtypeStruct((S, LANES), F32),
                   jax.ShapeDtypeStruct((S, LANES), F32)],
        compiler_params=_cparams(("parallel",)),
        name="route",
    )(h, wpad, bpad)


def _rank_kernel(sel_ref, i1_ref, i2_ref, tril_ref, rank_ref, cnt_ref):
    @pl.when(pl.program_id(0) == 0)
    def _():
        cnt_ref[...] = jnp.zeros(cnt_ref.shape, F32)

    sel = sel_ref[...]
    rank = _dot(tril_ref[...], sel) + cnt_ref[...]
    lane = lax.broadcasted_iota(jnp.int32, rank.shape, 1)
    r1 = jnp.sum(jnp.where(lane == i1_ref[...], rank, 0.0), axis=1, keepdims=True)
    r2 = jnp.sum(jnp.where(lane == i2_ref[...], rank, 0.0), axis=1, keepdims=True)
    rank_ref[...] = jnp.where(lane == 0, r1, jnp.where(lane == 1, r2, 0.0))
    cnt_ref[...] += jnp.sum(sel.astype(F32), axis=0, keepdims=True)


def _rank_call(sel, i1b, i2b, tb):
    S = sel.shape[0]
    tril = jnp.asarray(np.tril(np.ones((tb, tb), np.float32), -1), BF16)
    row = pl.BlockSpec((tb, LANES), lambda i: (i, 0))
    return pl.pallas_call(
        _rank_kernel,
        grid=(S // tb,),
        in_specs=[row, row, row, pl.BlockSpec((tb, tb), lambda i: (0, 0))],
        out_specs=[row, pl.BlockSpec((1, LANES), lambda i: (0, 0))],
        out_shape=[jax.ShapeDtypeStruct((S, LANES), F32), jax.ShapeDtypeStruct((1, LANES), F32)],
        compiler_params=_cparams(("arbitrary",)),
        name="rank",
    )(sel, i1b, i2b, tril)


def _row_copy(src_ref, s, dst_ref, d, sem):
    return pltpu.make_async_copy(src_ref.at[pl.ds(s, 1)], dst_ref.at[pl.ds(d, 1)], sem)


def _dispatch_kernel(tb, tmx, n_tok, dest_ref, ends_ref, h_ref, xs_hbm, zero_ref, sem, zsem):
    base = pl.program_id(0) * tb

    @pl.when(pl.program_id(0) == 0)
    def _():
        zero_ref[...] = jnp.zeros(zero_ref.shape, F32)

        def fill(e, wait):
            start = 0 if e == 0 else ends_ref[e - 1]
            end = ends_ref[e]

            @pl.when(end > start)
            def _():
                cp = pltpu.make_async_copy(
                    zero_ref, xs_hbm.at[pl.ds(pl.multiple_of(end - tmx, tmx), tmx)], zsem)
                if wait:
                    cp.wait()
                else:
                    cp.start()

        def fill_tail(k, wait):
            start = ends_ref[N_EXPERTS - 1] + k * tmx

            @pl.when(start < xs_hbm.shape[0])
            def _():
                cp = pltpu.make_async_copy(
                    zero_ref, xs_hbm.at[pl.ds(pl.multiple_of(start, tmx), tmx)], zsem)
                if wait:
                    cp.wait()
                else:
                    cp.start()

        for wait in (False, True):
            for e in range(N_EXPERTS):
                fill(e, wait)
                fill_tail(e, wait)

    def issue(t, c):
        _row_copy(h_ref, t, xs_hbm, dest_ref[base + t], sem).start(priority=0)
        _row_copy(h_ref, t, xs_hbm, dest_ref[n_tok + base + t], sem).start(priority=1)
        return c

    lax.fori_loop(0, tb, issue, 0, unroll=DMA_UNROLL)

    def drain(t, c):
        _row_copy(h_ref, 0, xs_hbm, 0, sem).wait()
        _row_copy(h_ref, 0, xs_hbm, 0, sem).wait()
        return c

    lax.fori_loop(0, tb, drain, 0, unroll=DMA_UNROLL)


def _dispatch_call(dest, ends, h, n_rows, tb, tmx):
    S, D = h.shape
    return pl.pallas_call(
        functools.partial(_dispatch_kernel, tb, tmx, S),
        grid_spec=pltpu.PrefetchScalarGridSpec(
            num_scalar_prefetch=2,
            grid=(S // tb,),
            in_specs=[pl.BlockSpec((tb, D), lambda i, d, e: (i, 0))],
            out_specs=pl.BlockSpec(memory_space=pl.ANY),
            scratch_shapes=[pltpu.VMEM((tmx, D), F32), pltpu.SemaphoreType.DMA,
                            pltpu.SemaphoreType.DMA],
        ),
        out_shape=jax.ShapeDtypeStruct((n_rows, D), F32),
        compiler_params=_cparams(("arbitrary",)),
        name="dispatch",
    )(dest, ends, h)


def _experts_kernel(te_ref, nu_ref, xs_ref, wg_ref, wu_ref, wd_ref, ys_ref, xb_ref, acc_ref):
    i = pl.program_id(0)
    f = pl.program_id(1)
    last_f = pl.num_programs(1) - 1
    used = i < nu_ref[0]

    @pl.when(used & (f == 0))
    def _():
        xb_ref[...] = xs_ref[...].astype(BF16)

    @pl.when(used)
    def _():
        hid = _swiglu_hidden(xb_ref[...], wg_ref[0], wu_ref[0])
        part = _dot(hid.astype(BF16), wd_ref[0])

        @pl.when(f == 0)
        def _():
            acc_ref[...] = part

        @pl.when(f > 0)
        def _():
            acc_ref[...] += part

    @pl.when(used & (f == last_f))
    def _():
        ys_ref[...] = acc_ref[...]

    @pl.when(jnp.logical_not(used) & (f == last_f))
    def _():
        ys_ref[...] = jnp.zeros(ys_ref.shape, F32)


def _experts_call(tile_expert, n_used, xs, wg, wu, wd, tmx, tf):
    P, D = xs.shape
    F = wg.shape[2]
    nf = F // tf

    def fsel(i, f, nu):
        return jnp.where(i < nu[0], f, nf - 1)

    return pl.pallas_call(
        _experts_kernel,
        grid_spec=pltpu.PrefetchScalarGridSpec(
            num_scalar_prefetch=2,
            grid=(P // tmx, nf),
            in_specs=[pl.BlockSpec((tmx, D), lambda i, f, te, nu: (jnp.where(i < nu[0], i, 0), 0)),
                      pl.BlockSpec((1, D, tf), lambda i, f, te, nu: (te[i], 0, fsel(i, f, nu))),
                      pl.BlockSpec((1, D, tf), lambda i, f, te, nu: (te[i], 0, fsel(i, f, nu))),
                      pl.BlockSpec((1, tf, D), lambda i, f, te, nu: (te[i], fsel(i, f, nu), 0))],
            out_specs=pl.BlockSpec((tmx, D), lambda i, f, te, nu: (i, 0)),
            scratch_shapes=[pltpu.VMEM((tmx, D), BF16), pltpu.VMEM((tmx, D), F32)],
        ),
        out_shape=jax.ShapeDtypeStruct((P, D), F32),
        compiler_params=_cparams(("arbitrary", "arbitrary")),
        name="experts",
    )(tile_expert, n_used, xs, wg, wu, wd)


def _combine_kernel(tm, n_tok, dest_ref, ys_hbm, h_ref, w1_ref, w2_ref, g_ref, b_ref,
                    out_ref, outb_ref, buf_ref, sem):
    i = pl.program_id(0)
    slot = i % 2

    def gather(tile, slot_, start):
        base = tile * tm

        def one(t, c):
            for k in range(2):
                src_row = dest_ref[k * n_tok + base + t] if start else 0
                cp = pltpu.make_async_copy(ys_hbm.at[pl.ds(src_row, 1)],
                                           buf_ref.at[slot_, k, pl.ds(t, 1)], sem.at[slot_])
                if start:
                    cp.start(priority=k)
                else:
                    cp.wait()
            return c

        lax.fori_loop(0, tm, one, 0, unroll=DMA_UNROLL)

    @pl.when(i == 0)
    def _():
        gather(0, 0, True)

    @pl.when(i + 1 < pl.num_programs(0))
    def _():
        gather(i + 1, 1 - slot, True)

    gather(i, slot, False)

    w1 = w1_ref[...]
    w2 = w2_ref[...]
    y = jnp.concatenate(
        [w1 * buf_ref[slot, 0, :, c * LANES:(c + 1) * LANES]
         + w2 * buf_ref[slot, 1, :, c * LANES:(c + 1) * LANES]
         for c in range(h_ref.shape[1] // LANES)], axis=1)
    hn = _layer_norm(ALPHA * h_ref[...] + y, g_ref[...], b_ref[...])
    out_ref[...] = hn
    outb_ref[...] = hn.astype(BF16)


def _combine_call(dest, ys, h, w1b, w2b, g, b, tm):
    S, D = h.shape
    row = pl.BlockSpec((tm, D), lambda i, d: (i, 0))
    lrow = pl.BlockSpec((tm, LANES), lambda i, d: (i, 0))
    vec = pl.BlockSpec((1, D), lambda i, d: (0, 0))
    return pl.pallas_call(
        functools.partial(_combine_kernel, tm, S),
        grid_spec=pltpu.PrefetchScalarGridSpec(
            num_scalar_prefetch=1,
            grid=(S // tm,),
            in_specs=[pl.BlockSpec(memory_space=pl.ANY), row, lrow, lrow, vec, vec],
            out_specs=[row, row],
            scratch_shapes=[pltpu.VMEM((2, 2, tm, D), F32), pltpu.SemaphoreType.DMA((2,))],
        ),
        out_shape=[jax.ShapeDtypeStruct((S, D), F32), jax.ShapeDtypeStruct((S, D), BF16)],
        compiler_params=_cparams(("arbitrary",)),
        name="combine",
    )(dest, ys, h, w1b, w2b, g, b)


def _moe_call(h, w_router, b_router, wg, wu, wd, g, b, tm, tf):
    S, D = h.shape
    tmx = MOE_TILE
    sel, i1b, i2b, w1b, w2b = _route_call(h, w_router, b_router, tm)
    ranks, counts = _rank_call(sel, i1b, i2b, min(256, S))
    cnt = counts[0, :N_EXPERTS].astype(jnp.int32)
    padded = ((cnt + tmx - 1) // tmx) * tmx
    ends = jnp.cumsum(padded)
    off = ends - padded
    dest = jnp.concatenate([off[i1b[:, 0]] + ranks[:, 0].astype(jnp.int32),
                            off[i2b[:, 0]] + ranks[:, 1].astype(jnp.int32)])
    n_tiles = (2 * S) // tmx + N_EXPERTS
    tile_start = jnp.arange(n_tiles, dtype=jnp.int32) * tmx
    tile_expert = jnp.minimum(jnp.sum(tile_start[:, None] >= ends[None, :], axis=1),
                              N_EXPERTS - 1).astype(jnp.int32)
    n_used = (ends[-1] // tmx).astype(jnp.int32).reshape(1)
    xs = _dispatch_call(dest, ends.astype(jnp.int32), h, n_tiles * tmx, tm, tmx)
    ys = _experts_call(tile_expert, n_used, xs, wg, wu, wd, tmx, tf)
    return _combine_call(dest, ys, h, w1b, w2b, g, b, min(256, S))


def _rope_tables(S):
    half = HEAD_DIM // 2
    pos = jnp.arange(S, dtype=F32)
    inv = ROPE_THETA ** (-jnp.arange(half, dtype=F32) / half)
    ang = pos[:, None] * inv[None, :]
    cos, sin = jnp.cos(ang), jnp.sin(ang)
    reps = LANES // HEAD_DIM
    cos_t = jnp.tile(jnp.concatenate([cos, cos], axis=1), (1, reps))
    sin_t = jnp.tile(jnp.concatenate([-sin, sin], axis=1), (1, reps))
    return cos_t, sin_t


def _pick(S, pref):
    t = min(pref, S)
    assert S % t == 0
    return t


def _forward(x, ln_in_g, ln_in_b, w_in, b_forget, lam_q1, lam_k1, lam_q2, lam_k2,
             diff_norm_g, w_up_diff, w_up_sb, w_up_fox, w_out, ln_mix_g, ln_mix_b,
             ln_ffn_g, ln_ffn_b, w_gate_dense, w_up_dense, w_down_dense, w_router,
             b_router, w_gate_moe, w_up_moe, w_down_moe):
    B, S, D = x.shape
    assert B == 1 and D == D_MODEL and S % LANES == 0
    depth = w_in.shape[0]
    tm = _pick(S, 512)
    tq = _pick(S, 512)
    tm_ffn = _pick(S, 1024)
    tf = D_FF // 7
    tf_moe = D_FF // 2
    vec = lambda a: a.reshape(1, -1).astype(F32)

    cos_t, sin_t = _rope_tables(S)
    h, hb = _ln_call(x.reshape(S, D), ln_in_g, ln_in_b, tm)
    w_atts, w_forgets, w_gate_logits = _wsplit_call(w_in, 128)
    for l in range(depth):
        w_att, w_forget, w_gate_logit = w_atts[l], w_forgets[l], w_gate_logits[l]
        fbias = jnp.pad(b_forget[l].astype(F32), (0, LANES - N_HEADS_FOX)).reshape(1, LANES)
        dq, dk, dv, sq, sk, sv, fq, fk, fv, logf = _proj_call(hb, w_att, w_forget, cos_t, sin_t,
                                                              fbias, tm)

        lf = logf[:, :N_HEADS_FOX].T.reshape(N_HEADS_FOX, S // LANES, LANES)
        fcum = _fcum_call(lf)

        lam_init = 0.8 - 0.6 * math.exp(-0.3 * l)
        lamv = jnp.stack([lam_q1[l], lam_k1[l], lam_q2[l], lam_k2[l]]).astype(F32)
        o_diff = _diff_call(dq, dk, dv, lamv, vec(diff_norm_g[l]), lam_init, tq)
        o_sb = _sb_call(sq, sk, sv, tq, tq)
        o_fox = _fox_call(fq, fk, fv, fcum, tq)

        h, hb = _merge_call(hb, h, o_diff, o_sb, o_fox, w_gate_logit,
                            w_up_diff[l].astype(BF16), w_up_sb[l].astype(BF16),
                            w_up_fox[l].astype(BF16), w_out[l].astype(BF16),
                            vec(ln_mix_g[l]), vec(ln_mix_b[l]), tm)
        j = l // 2
        if l % 2 == 0:
            h, hb = _ffn_call(hb, h, w_gate_dense[j].astype(BF16), w_up_dense[j].astype(BF16),
                              w_down_dense[j].astype(BF16), vec(ln_ffn_g[l]), vec(ln_ffn_b[l]),
                              tm_ffn, tf)
        else:
            h, hb = _moe_call(h, w_router[j], b_router[j], w_gate_moe[j].astype(BF16),
                              w_up_moe[j].astype(BF16), w_down_moe[j].astype(BF16),
                              vec(ln_ffn_g[l]), vec(ln_ffn_b[l]), tm, tf_moe)
    return h.reshape(B, S, D)


def kernel(x, ln_in_g, ln_in_b, w_in, b_forget, lam_q1, lam_k1, lam_q2, lam_k2, diff_norm_g,
           w_up_diff, w_up_sb, w_up_fox, w_out, ln_mix_g, ln_mix_b, ln_ffn_g, ln_ffn_b,
           w_gate_dense, w_up_dense, w_down_dense, w_router, b_router, w_gate_moe, w_up_moe,
           w_down_moe):
    return _forward(x, ln_in_g, ln_in_b, w_in, b_forget, lam_q1, lam_k1, lam_q2, lam_k2,
                    diff_norm_g, w_up_diff, w_up_sb, w_up_fox, w_out, ln_mix_g, ln_mix_b,
                    ln_ffn_g, ln_ffn_b, w_gate_dense, w_up_dense, w_down_dense, w_router,
                    b_router, w_gate_moe, w_up_moe, w_down_moe)
```

```python
import functools
import math

import jax
import jax.numpy as jnp
import numpy as np
from jax import lax
from jax.experimental import pallas as pl
from jax.experimental.pallas import tpu as pltpu

D_MODEL = 1024
DEPTH = 2
CHUNK = 64
HEAD_DIM = 64
N_HEADS_DIFF = 4
N_HEADS_SB = 4
N_HEADS_FOX = 4
WIDTH_DIFF = N_HEADS_DIFF * 2 * HEAD_DIM
WIDTH_SB = N_HEADS_SB * HEAD_DIM
WIDTH_FOX = N_HEADS_FOX * HEAD_DIM
ROPE_THETA = 10000.0
N_EXPERTS = 8
D_FF = 3584
ALPHA = (2 * DEPTH) ** 0.25
Q_SCALE = HEAD_DIM ** -0.5
LOG2E = 1.4426950408889634
N_ATT = 3 * WIDTH_DIFF + 3 * WIDTH_SB + 3 * WIDTH_FOX
LANES = 128
NEG = -1e30

F32 = jnp.float32
BF16 = jnp.bfloat16

VMEM_LIMIT = 56 * 1024 * 1024


def _cparams(sem):
    return pltpu.CompilerParams(dimension_semantics=sem, vmem_limit_bytes=VMEM_LIMIT)


def _layer_norm(z, g, b, eps=1e-5):
    mu = jnp.mean(z, axis=-1, keepdims=True)
    zc = z - mu
    var = jnp.mean(zc * zc, axis=-1, keepdims=True)
    return zc * lax.rsqrt(var + eps) * g + b


def _sigmoid(x):
    return 1.0 / (1.0 + jnp.exp(-x))


def _softplus(x):
    return jnp.maximum(x, 0.0) + jnp.log1p(jnp.exp(-jnp.abs(x)))


def _dot(a, b):
    return jnp.dot(a, b, preferred_element_type=F32)


def _dot_nt(a, b):
    return lax.dot_general(a, b, (((1,), (1,)), ((), ())), preferred_element_type=F32)


def _split3(x):
    x1 = x.astype(BF16)
    r = x - x1.astype(F32)
    x2 = r.astype(BF16)
    x3 = (r - x2.astype(F32)).astype(BF16)
    return x1, x2, x3


def _ln_kernel(x_ref, g_ref, b_ref, h_ref, hb_ref):
    h = _layer_norm(x_ref[...], g_ref[...], b_ref[...])
    h_ref[...] = h
    hb_ref[...] = h.astype(BF16)


def _ln_call(x, g, b, tm):
    S, D = x.shape
    row = pl.BlockSpec((tm, D), lambda i: (i, 0))
    vec = pl.BlockSpec((1, D), lambda i: (0, 0))
    return pl.pallas_call(
        _ln_kernel,
        grid=(S // tm,),
        in_specs=[row, vec, vec],
        out_specs=[row, row],
        out_shape=[jax.ShapeDtypeStruct((S, D), F32), jax.ShapeDtypeStruct((S, D), BF16)],
        compiler_params=_cparams(("parallel",)),
        name="ln_in",
    )(x, g.reshape(1, D), b.reshape(1, D))


def _wsplit_kernel(w_ref, att_ref, fgt_ref, gate_ref):
    att_ref[0] = w_ref[0, :, :N_ATT].astype(BF16)
    lane = lax.broadcasted_iota(jnp.int32, (1, LANES), 1)
    fgt_ref[0] = jnp.where(lane < N_HEADS_FOX, w_ref[0, :, N_ATT:N_ATT + LANES], 0.0).astype(BF16)
    gate_ref[0] = w_ref[0, :, N_ATT + N_HEADS_FOX:].astype(BF16)


def _wsplit_call(w_in, tr):
    depth, D, n_in = w_in.shape
    n_gate = n_in - N_ATT - N_HEADS_FOX
    spec = lambda n: pl.BlockSpec((1, tr, n), lambda l, i: (l, i, 0))
    return pl.pallas_call(
        _wsplit_kernel,
        grid=(depth, D // tr),
        in_specs=[spec(n_in)],
        out_specs=[spec(N_ATT), spec(LANES), spec(n_gate)],
        out_shape=[jax.ShapeDtypeStruct((depth, D, N_ATT), BF16),
                   jax.ShapeDtypeStruct((depth, D, LANES), BF16),
                   jax.ShapeDtypeStruct((depth, D, n_gate), BF16)],
        compiler_params=_cparams(("parallel", "parallel")),
        name="wsplit",
    )(w_in)


def _proj_kernel(hb_ref, w_ref, wf_ref, cos_ref, sin_ref, fb_ref,
                 dq_ref, dk_ref, dv_ref, sq_ref, sk_ref, sv_ref, fq_ref, fk_ref, fv_ref, lf_ref):
    x = hb_ref[...]
    cos = cos_ref[...]
    sin = sin_ref[...]
    lane = lax.broadcasted_iota(jnp.int32, (1, LANES), 1)
    lower = (lane & (HEAD_DIM // 2)) == 0

    def mm(c0, c1):
        return _dot(x, w_ref[:, c0:c1])

    def rope_store(y, out_ref, scale):
        for g in range(y.shape[1] // LANES):
            yg = y[:, g * LANES:(g + 1) * LANES]
            partner = jnp.where(lower, pltpu.roll(yg, LANES - HEAD_DIM // 2, 1),
                                pltpu.roll(yg, HEAD_DIM // 2, 1))
            r = yg * cos + partner * sin
            if scale != 1.0:
                r = r * scale
            out_ref[:, g * LANES:(g + 1) * LANES] = r.astype(BF16)

    rope_store(mm(0, 512), dq_ref, Q_SCALE * LOG2E)
    rope_store(mm(512, 1024), dk_ref, 1.0)
    dv_ref[...] = mm(1024, 1536).astype(BF16)
    sq_ref[...] = (mm(1536, 1792) * (Q_SCALE * LOG2E)).astype(BF16)
    sk_ref[...] = mm(1792, 2048).astype(BF16)
    sv_ref[...] = mm(2048, 2304).astype(BF16)
    fq_ref[...] = (mm(2304, 2560) * (Q_SCALE * LOG2E)).astype(BF16)
    fk_ref[...] = mm(2560, 2816).astype(BF16)
    fv_ref[...] = mm(2816, 3072).astype(BF16)
    lf_ref[...] = -_softplus(-(_dot(x, wf_ref[...]) + fb_ref[...]))


def _proj_call(hb, w_att, w_forget, cos, sin, fbias, tm):
    S, D = hb.shape
    row = lambda n: pl.BlockSpec((tm, n), lambda i: (i, 0))
    full = lambda a: pl.BlockSpec(a.shape, lambda i: (0, 0))
    widths = [512, 512, 512, 256, 256, 256, 256, 256, 256]
    out_shape = [jax.ShapeDtypeStruct((S, n), BF16) for n in widths]
    out_shape.append(jax.ShapeDtypeStruct((S, LANES), F32))
    return pl.pallas_call(
        _proj_kernel,
        grid=(S // tm,),
        in_specs=[row(D), full(w_att), full(w_forget), row(LANES), row(LANES), full(fbias)],
        out_specs=[row(n) for n in widths] + [row(LANES)],
        out_shape=out_shape,
        compiler_params=_cparams(("parallel",)),
        name="proj",
    )(hb, w_att, w_forget, cos, sin, fbias)


def _fcum_kernel(lf_ref, f_ref):
    nh, R, _ = lf_ref.shape
    r0 = lax.broadcasted_iota(jnp.int32, (LANES, LANES), 0)
    c0 = lax.broadcasted_iota(jnp.int32, (LANES, LANES), 1)
    upper = jnp.where(r0 <= c0, 1.0, 0.0).astype(BF16)
    ones = jnp.ones((LANES, LANES), BF16)
    r1 = lax.broadcasted_iota(jnp.int32, (R, R), 0)
    c1 = lax.broadcasted_iota(jnp.int32, (R, R), 1)
    below = jnp.where(c1 < r1, 1.0, 0.0).astype(BF16)
    for h in range(nh):
        parts = _split3(lf_ref[h])
        within = sum(_dot(p, upper) for p in parts)
        totals = sum(_dot(p, ones) for p in parts)
        offs = sum(_dot(below, t) for t in _split3(totals))
        f_ref[h] = within + offs


def _fcum_call(lf):
    return pl.pallas_call(
        _fcum_kernel,
        out_shape=jax.ShapeDtypeStruct(lf.shape, F32),
        name="fcum",
    )(lf)


def _split_q(q):
    lane = lax.broadcasted_iota(jnp.int32, (1, LANES), 1)
    lo = lane < HEAD_DIM
    zero = jnp.zeros_like(q)
    return jnp.where(lo, q, zero), jnp.where(lo, zero, q)


STRIP = 16
TRI = 256
PACE_LAG = 128


def _chunk(ref, r0, c):
    return ref[r0:r0 + STRIP, c * LANES:(c + 1) * LANES]


def _zero_after(x):
    half = jnp.uint32(16)
    bits = lax.shift_right_logical(lax.shift_right_logical(pltpu.bitcast(x, jnp.uint32), half), half)
    return pltpu.bitcast(bits, F32)


def _softmax_tile(s_ref, p_ref, m_ref, l_ref, chunk_state, pace, finish_strip):
    tq, tk = s_ref.shape
    for r0 in range(0, tq, STRIP):
        states = [chunk_state(r0, c) for c in range(tk // LANES)]
        sc = []
        for c, st in enumerate(states):
            if st is False:
                sc.append(None)
            elif st is None:
                sc.append(_chunk(s_ref, r0, c))
            else:
                sc.append(jnp.where(st, _chunk(s_ref, r0, c), NEG))
        live = [s for s in sc if s is not None]
        m_prev = m_ref[r0:r0 + STRIP, :]
        zero = pace(r0)
        if zero is not None:
            m_prev = m_prev + zero
        if live:
            mx = functools.reduce(jnp.maximum, live)
            m_new = jnp.maximum(m_prev, jnp.max(mx, axis=1, keepdims=True))
        else:
            m_new = m_prev
        alpha = jnp.exp2(m_prev - m_new)
        psum = jnp.zeros((STRIP, LANES), F32)
        for c, s in enumerate(sc):
            if s is None:
                p_ref[r0:r0 + STRIP, c * LANES:(c + 1) * LANES] = jnp.zeros((STRIP, LANES), BF16)
            else:
                p = jnp.exp2(s - m_new)
                psum = psum + p
                p_ref[r0:r0 + STRIP, c * LANES:(c + 1) * LANES] = p.astype(BF16)
        m_ref[r0:r0 + STRIP, :] = m_new
        l_ref[r0:r0 + STRIP, :] = alpha * l_ref[r0:r0 + STRIP, :] + psum
        finish_strip(r0, alpha)


def _all_visible(r0, c):
    return None


def _init_softmax_state(m_refs, l_refs, acc_refs):
    for m_ref, l_ref, acc_ref in zip(m_refs, l_refs, acc_refs):
        m_ref[...] = jnp.full(m_ref.shape, NEG, F32)
        l_ref[...] = jnp.zeros(l_ref.shape, F32)
        acc_ref[...] = jnp.zeros(acc_ref.shape, F32)


def _softmax_scratch(t):
    per_stream = [pltpu.VMEM((t, t), F32), pltpu.VMEM((t, t), F32),
                  pltpu.VMEM((t, t), BF16), pltpu.VMEM((t, t), BF16),
                  pltpu.VMEM((t, LANES), F32), pltpu.VMEM((t, LANES), F32),
                  pltpu.VMEM((t, LANES), F32), pltpu.VMEM((t, LANES), F32)]
    return per_stream + per_stream


def _softmax_state_refs(scratch):
    return scratch[4::8], scratch[5::8], scratch[6::8]


def _row_total(l_ref):
    return jnp.sum(l_ref[...], axis=1, keepdims=True)


def _sweep(qi, t, scores, v_ref, scratch, diag_chunk_state):
    s0, s1, p0, p1, m_refs, l_refs, acc_refs, al_refs = (scratch[i::8] for i in range(8))
    s_bufs, p_bufs = (s0, s1), (p0, p1)

    def step(j, par, chunk_state, prefetch, earlier_pv=None):
        s_next = scores(j + 1) if prefetch else None
        vb = v_ref[pl.ds(pl.multiple_of(j * t, t), t), :]
        pvs = []
        for x in range(2):
            if prefetch:
                s_bufs[1 - par][x][...] = s_next[x]

            def pace(r0, x=x):
                zero = None
                if earlier_pv is not None:
                    zero = _zero_after(earlier_pv[x][r0:r0 + STRIP, :])
                if prefetch and r0 >= PACE_LAG:
                    rows = slice(r0 - PACE_LAG, r0 - PACE_LAG + STRIP)
                    z2 = _zero_after(s_bufs[1 - par][x][rows, :LANES])
                    zero = z2 if zero is None else zero + z2
                return zero

            def finish_strip(r0, alpha, x=x):
                al_refs[x][r0:r0 + STRIP, :] = alpha

            _softmax_tile(s_bufs[par][x], p_bufs[par][x], m_refs[x], l_refs[x], chunk_state,
                          pace, finish_strip)
            pv = _dot(p_bufs[par][x][...], vb)
            acc_refs[x][...] = al_refs[x][...] * acc_refs[x][...] + pv
            pvs.append(pv)
        return pvs

    first = scores(0)
    for x in range(2):
        s0[x][...] = first[x]

    def pair(i, c):
        pvs = step(2 * i, 0, _all_visible, True)
        step(2 * i + 1, 1, _all_visible, True, pvs)
        return c

    lax.fori_loop(0, qi // 2, pair, 0)

    @pl.when(qi % 2 == 1)
    def _():
        pvs = step(qi - 1, 0, _all_visible, True)
        step(qi, 1, diag_chunk_state, False, pvs)

    @pl.when(qi % 2 == 0)
    def _():
        step(qi, 0, diag_chunk_state, False)


def _diff_chunk_state(col0):
    def state(r0, c):
        q_chunk = r0 // CHUNK
        k_lo = (col0 + c * LANES) // CHUNK
        k_hi = (col0 + c * LANES + LANES - 1) // CHUNK
        if k_hi <= q_chunk:
            return None
        if k_lo > q_chunk:
            return False
        return lax.broadcasted_iota(jnp.int32, (STRIP, LANES), 1) < CHUNK
    return state


def _diff_kernel(t, lam_init, q_ref, k_ref, v_ref, lamv_ref, g_ref, o_ref, *scratch):
    m_refs, l_refs, acc_refs = _softmax_state_refs(scratch)
    qi = pl.program_id(1)
    qs = _split_q(q_ref[...])
    _init_softmax_state(m_refs, l_refs, acc_refs)

    def scores(j):
        kb = k_ref[pl.ds(pl.multiple_of(j * t, t), t), :]
        return [_dot_nt(qs[x], kb) for x in range(2)]

    _sweep(qi, t, scores, v_ref, scratch, _diff_chunk_state(0))

    lv = lamv_ref[...]
    lam = (jnp.exp(jnp.sum(lv[0:1] * lv[1:2], axis=1, keepdims=True))
           - jnp.exp(jnp.sum(lv[2:3] * lv[3:4], axis=1, keepdims=True)) + lam_init)
    o = (acc_refs[0][...] / _row_total(l_refs[0])
         - lam * (acc_refs[1][...] / _row_total(l_refs[1])))
    o = o * lax.rsqrt(jnp.mean(o * o, axis=-1, keepdims=True) + 1e-6) * g_ref[...]
    o_ref[...] = (o * (1.0 - lam_init)).astype(BF16)


def _diff_call(dq, dk, dv, lamv, gnorm, lam_init, tq):
    S = dq.shape[0]
    nh = dq.shape[1] // LANES
    qspec = pl.BlockSpec((tq, LANES), lambda h, i: (i, h))
    kvspec = pl.BlockSpec((S, LANES), lambda h, i: (0, h))
    full = lambda a: pl.BlockSpec(a.shape, lambda h, i: (0, 0))
    return pl.pallas_call(
        functools.partial(_diff_kernel, tq, lam_init),
        grid=(nh, S // tq),
        in_specs=[qspec, kvspec, kvspec, full(lamv), full(gnorm)],
        out_specs=qspec,
        out_shape=jax.ShapeDtypeStruct(dq.shape, BF16),
        scratch_shapes=_softmax_scratch(tq),
        compiler_params=_cparams(("parallel", "arbitrary")),
        name="diff_attn",
    )(dq, dk, dv, lamv, gnorm)


def _causal_chunk_state(strict, col0=0):
    def state(r0, c):
        k_lo, k_hi = col0 + c * LANES, col0 + c * LANES + LANES - 1
        q_lo, q_hi = r0, r0 + STRIP - 1
        if k_hi < q_lo or (not strict and k_hi <= q_lo):
            return None
        if k_lo > q_hi or (strict and k_lo >= q_hi):
            return False
        row = r0 + lax.broadcasted_iota(jnp.int32, (STRIP, LANES), 0)
        col = k_lo + lax.broadcasted_iota(jnp.int32, (STRIP, LANES), 1)
        return col < row if strict else col <= row
    return state


def _fox_kernel(t, q_ref, k_ref, v_ref, f_ref, o_ref, *scratch):
    m_refs, l_refs, acc_refs = _softmax_state_refs(scratch)
    qi = pl.program_id(1)
    qs = _split_q(q_ref[...])
    _init_softmax_state(m_refs, l_refs, acc_refs)
    f0 = [f_ref[x, qi][:, 0:1] for x in range(2)]

    def scores(j):
        kb = k_ref[pl.ds(pl.multiple_of(j * t, t), t), :]
        return [_dot_nt(qs[x], kb) + (f0[x] - f_ref[x, j]) * LOG2E for x in range(2)]

    _sweep(qi, t, scores, v_ref, scratch, _causal_chunk_state(strict=False))

    lane = lax.broadcasted_iota(jnp.int32, (1, LANES), 1)
    o = jnp.where(lane < HEAD_DIM, acc_refs[0][...] / _row_total(l_refs[0]),
                  acc_refs[1][...] / _row_total(l_refs[1]))
    o_ref[...] = o.astype(BF16)


def _fox_call(fq, fk, fv, fcum, tq):
    S = fq.shape[0]
    npair = fq.shape[1] // LANES
    qspec = pl.BlockSpec((tq, LANES), lambda p, i: (i, p))
    kvspec = pl.BlockSpec((S, LANES), lambda p, i: (0, p))
    nh = fcum.shape[0]
    fspec = pl.BlockSpec((2, S // tq, 1, tq), lambda p, i: (p, 0, 0, 0))
    return pl.pallas_call(
        functools.partial(_fox_kernel, tq),
        grid=(npair, S // tq),
        in_specs=[qspec, kvspec, kvspec, fspec],
        out_specs=qspec,
        out_shape=jax.ShapeDtypeStruct(fq.shape, BF16),
        scratch_shapes=_softmax_scratch(tq),
        compiler_params=_cparams(("parallel", "arbitrary")),
        name="fox_attn",
    )(fq, fk, fv, fcum.reshape(nh, S // tq, 1, tq))


def _sb_kernel(tq, tk, q_ref, k_ref, v_ref, tri_ref, o_ref, *scratch):
    (z0, z1, u0, u1, lb0, lb1, a0, a1, tot_refs, carry_refs, acc_refs) = (scratch[i::11]
                                                                          for i in range(11))
    z_bufs, u_bufs, lb_bufs, a_bufs = (z0, z1), (u0, u1), (lb0, lb1), (a0, a1)
    qi = pl.program_id(1)
    qs = _split_q(q_ref[...])
    for x in range(2):
        carry_refs[x][...] = jnp.zeros((tq, LANES), F32)
        acc_refs[x][...] = jnp.zeros((tq, LANES), F32)
    assert tk == tq and tk == 2 * TRI
    nc = tk // LANES

    def scores(j, par):
        kb = k_ref[pl.ds(pl.multiple_of(j * tk, tk), tk), :]
        for x in range(2):
            z_bufs[par][x][...] = _dot_nt(qs[x], kb)

    def gates(par, chunk_state, earlier_pv):
        for x in range(2):
            z_ref, u_ref, lb_ref = z_bufs[par][x], u_bufs[par][x], lb_bufs[par][x]
            for r0 in range(0, tq, STRIP):
                usum = jnp.zeros((STRIP, LANES), F32)
                if earlier_pv is not None:
                    usum = _zero_after(earlier_pv[x][r0:r0 + STRIP, :])
                if r0 >= PACE_LAG:
                    rows = slice(r0 - PACE_LAG, r0 - PACE_LAG + STRIP)
                    usum = usum + _zero_after(z_bufs[1 - par][x][rows, :LANES])
                for c in range(nc):
                    st = chunk_state(r0, c)
                    if st is False:
                        u_ref[r0:r0 + STRIP, c * LANES:(c + 1) * LANES] = jnp.zeros((STRIP, LANES), BF16)
                        lb_ref[r0:r0 + STRIP, c * LANES:(c + 1) * LANES] = jnp.zeros((STRIP, LANES), F32)
                        continue
                    z = _chunk(z_ref, r0, c)
                    zb = z.astype(BF16)
                    corr = jnp.log(1.0 + jnp.exp2(-jnp.abs(zb))) * LOG2E
                    sp = jnp.maximum(z, 0.0) + corr.astype(F32)
                    u = sp if st is None else jnp.where(st, sp, 0.0)
                    u_ref[r0:r0 + STRIP, c * LANES:(c + 1) * LANES] = u.astype(BF16)
                    lb_ref[r0:r0 + STRIP, c * LANES:(c + 1) * LANES] = z - sp
                    usum = usum + u
                tot_refs[x][r0:r0 + STRIP, :] = (jnp.sum(usum, axis=1, keepdims=True)
                                                 + jnp.zeros((STRIP, LANES), F32))

    def weights_and_accumulate(j, par, masked):
        vb = v_ref[pl.ds(pl.multiple_of(j * tk, tk), tk), :]
        tri_stack = tri_ref[...]
        pvs = []
        for x in range(2):
            u_ref, lb_ref, a_ref = u_bufs[par][x], lb_bufs[par][x], a_bufs[par][x]
            suffix = (_dot(u_ref[...], tri_stack), _dot(u_ref[:, TRI:], tri_stack[:TRI]))
            carry = carry_refs[x][...]
            for c in range(nc):
                cols = slice(c * LANES, (c + 1) * LANES)
                blk, off = divmod(c * LANES, TRI)
                suf = suffix[blk][:, off:off + LANES]
                a = jnp.exp2(lb_ref[:, cols] - suf - carry)
                if masked:
                    row = lax.broadcasted_iota(jnp.int32, (tq, LANES), 0)
                    col = c * LANES + lax.broadcasted_iota(jnp.int32, (tq, LANES), 1)
                    a = jnp.where(col < row, a, 0.0)
                a_ref[:, cols] = a.astype(BF16)
            pv = _dot(a_ref[...], vb)
            acc_refs[x][...] += pv
            carry_refs[x][...] = carry + tot_refs[x][...]
            pvs.append(pv)
        return pvs

    def step(j, par, masked, earlier_pv=None):
        scores(jnp.maximum(j - 1, 0), 1 - par)
        gates(par, _causal_chunk_state(True) if masked else _all_visible, earlier_pv)
        return weights_and_accumulate(j, par, masked)

    scores(qi, 0)
    step(qi, 0, True)

    def pair(i, c):
        j = qi - 1 - 2 * i
        pvs = step(j, 1, False)
        step(j - 1, 0, False, pvs)
        return c

    lax.fori_loop(0, qi // 2, pair, 0)

    @pl.when(qi % 2 == 1)
    def _():
        step(0, 1, False)

    lane = lax.broadcasted_iota(jnp.int32, (1, LANES), 1)
    o_ref[...] = jnp.where(lane < HEAD_DIM, acc_refs[0][...], acc_refs[1][...]).astype(BF16)


def _sb_call(sq, sk, sv, tq, tk):
    S = sq.shape[0]
    npair = sq.shape[1] // LANES
    qspec = pl.BlockSpec((tq, LANES), lambda p, i: (i, p))
    kvspec = pl.BlockSpec((S, LANES), lambda p, i: (0, p))
    tri = jnp.asarray(np.concatenate([np.tril(np.ones((TRI, TRI), np.float32), -1),
                                      np.ones((TRI, TRI), np.float32)]), BF16)
    return pl.pallas_call(
        functools.partial(_sb_kernel, tq, tk),
        grid=(npair, S // tq),
        in_specs=[qspec, kvspec, kvspec, pl.BlockSpec((2 * TRI, TRI), lambda p, i: (0, 0))],
        out_specs=qspec,
        out_shape=jax.ShapeDtypeStruct(sq.shape, BF16),
        scratch_shapes=2 * [pltpu.VMEM((tq, tk), F32), pltpu.VMEM((tq, tk), F32),
                            pltpu.VMEM((tq, tk), BF16), pltpu.VMEM((tq, tk), BF16),
                            pltpu.VMEM((tq, tk), F32), pltpu.VMEM((tq, tk), F32),
                            pltpu.VMEM((tq, tk), BF16), pltpu.VMEM((tq, tk), BF16),
                            pltpu.VMEM((tq, LANES), F32), pltpu.VMEM((tq, LANES), F32),
                            pltpu.VMEM((tq, LANES), F32)],
        compiler_params=_cparams(("parallel", "arbitrary")),
        name="sb_attn",
    )(sq, sk, sv, tri)


MERGE_COLS = 256


def _merge_kernel(hb_ref, h_ref, od_ref, os_ref, of_ref, wg_ref, wud_ref, wus_ref, wuf_ref, wo_ref,
                  g_ref, b_ref, out_ref, outb_ref, merged_ref):
    x = hb_ref[...]
    D = x.shape[1]
    for n0 in range(0, D, MERGE_COLS):
        merged = None
        for c, (o_ref, wu_ref) in enumerate(((od_ref, wud_ref), (os_ref, wus_ref), (of_ref, wuf_ref))):
            gate = _sigmoid(_dot(x, wg_ref[:, c * D + n0:c * D + n0 + MERGE_COLS]))
            term = gate * _dot(o_ref[...], wu_ref[:, n0:n0 + MERGE_COLS])
            merged = term if merged is None else merged + term
        merged_ref[:, n0:n0 + MERGE_COLS] = merged.astype(BF16)
    y = _dot(merged_ref[...], wo_ref[...])
    hn = _layer_norm(ALPHA * h_ref[...] + y, g_ref[...], b_ref[...])
    out_ref[...] = hn
    outb_ref[...] = hn.astype(BF16)


def _merge_call(hb, h, od, osb, of, wg, wud, wus, wuf, wo, g, b, tm):
    S, D = h.shape
    row = lambda a: pl.BlockSpec((tm, a.shape[1]), lambda i: (i, 0))
    full = lambda a: pl.BlockSpec(a.shape, lambda i: (0, 0))
    args = (hb, h, od, osb, of, wg, wud, wus, wuf, wo, g, b)
    in_specs = [row(a) for a in args[:5]] + [full(a) for a in args[5:]]
    return pl.pallas_call(
        _merge_kernel,
        grid=(S // tm,),
        in_specs=in_specs,
        out_specs=[row(h), row(h)],
        out_shape=[jax.ShapeDtypeStruct((S, D), F32), jax.ShapeDtypeStruct((S, D), BF16)],
        scratch_shapes=[pltpu.VMEM((tm, D), BF16)],
        compiler_params=_cparams(("parallel",)),
        name="merge",
    )(*args)


def _swiglu_hidden(x, wg, wu):
    gate = _dot(x, wg)
    return (gate * _sigmoid(gate)) * _dot(x, wu)


def _ffn_kernel(hb_ref, h_ref, wg_ref, wu_ref, wd_ref, g_ref, b_ref, out_ref, outb_ref, acc_ref):
    f = pl.program_id(1)

    @pl.when(f == 0)
    def _():
        acc_ref[...] = jnp.zeros(acc_ref.shape, F32)

    hid = _swiglu_hidden(hb_ref[...], wg_ref[...], wu_ref[...])
    acc_ref[...] += _dot(hid.astype(BF16), wd_ref[...])

    @pl.when(f == pl.num_programs(1) - 1)
    def _():
        hn = _layer_norm(ALPHA * h_ref[...] + acc_ref[...], g_ref[...], b_ref[...])
        out_ref[...] = hn
        outb_ref[...] = hn.astype(BF16)


def _ffn_call(hb, h, wg, wu, wd, g, b, tm, tf):
    S, D = h.shape
    F = wg.shape[1]
    row = pl.BlockSpec((tm, D), lambda i, f: (i, 0))
    vec = pl.BlockSpec((1, D), lambda i, f: (0, 0))
    return pl.pallas_call(
        _ffn_kernel,
        grid=(S // tm, F // tf),
        in_specs=[row, row,
                  pl.BlockSpec((D, tf), lambda i, f: (0, f)),
                  pl.BlockSpec((D, tf), lambda i, f: (0, f)),
                  pl.BlockSpec((tf, D), lambda i, f: (f, 0)),
                  vec, vec],
        out_specs=[row, row],
        out_shape=[jax.ShapeDtypeStruct((S, D), F32), jax.ShapeDtypeStruct((S, D), BF16)],
        scratch_shapes=[pltpu.VMEM((tm, D), F32)],
        compiler_params=_cparams(("parallel", "arbitrary")),
        name="ffn",
    )(hb, h, wg, wu, wd, g, b)


MOE_TILE = 512
DMA_UNROLL = 8


def _route_kernel(h_ref, w_ref, b_ref, sel_ref, i1_ref, i2_ref, w1_ref, w2_ref):
    logits = jnp.dot(h_ref[...], w_ref[...], precision=lax.Precision.HIGHEST,
                     preferred_element_type=F32) + b_ref[...]
    lane = lax.broadcasted_iota(jnp.int32, logits.shape, 1)
    logits = jnp.where(lane < N_EXPERTS, logits, NEG)
    v1 = jnp.max(logits, axis=1, keepdims=True)
    i1 = jnp.min(jnp.where(logits == v1, lane, LANES), axis=1, keepdims=True)
    rest = jnp.where(lane == i1, NEG, logits)
    v2 = jnp.max(rest, axis=1, keepdims=True)
    i2 = jnp.min(jnp.where(rest == v2, lane, LANES), axis=1, keepdims=True)
    e2 = jnp.exp(v2 - v1)
    zero = jnp.zeros(logits.shape, F32)
    sel_ref[...] = jnp.where((lane == i1) | (lane == i2), 1.0, 0.0).astype(BF16)
    i1_ref[...] = i1 + jnp.zeros(logits.shape, jnp.int32)
    i2_ref[...] = i2 + jnp.zeros(logits.shape, jnp.int32)
    w1_ref[...] = 1.0 / (1.0 + e2) + zero
    w2_ref[...] = e2 / (1.0 + e2) + zero


def _route_call(h, w_router, b_router, tm):
    S, D = h.shape
    wpad = jnp.zeros((D, LANES), F32).at[:, :N_EXPERTS].set(w_router)
    bpad = jnp.zeros((1, LANES), F32).at[0, :N_EXPERTS].set(b_router)
    row = pl.BlockSpec((tm, LANES), lambda i: (i, 0))
    return pl.pallas_call(
        _route_kernel,
        grid=(S // tm,),
        in_specs=[pl.BlockSpec((tm, D), lambda i: (i, 0)),
                  pl.BlockSpec((D, LANES), lambda i: (0, 0)),
                  pl.BlockSpec((1, LANES), lambda i: (0, 0))],
        out_specs=[row] * 5,
        out_shape=[jax.ShapeDtypeStruct((S, LANES), BF16),
                   jax.ShapeDtypeStruct((S, LANES), jnp.int32),
                   jax.ShapeDtypeStruct((S, LANES), jnp.int32),
                   jax.ShapeDtypeStruct((S, LANES), F32),
                   jax.ShapeDtypeStruct((S, LANES), F32)],
        compiler_params=_cparams(("parallel",)),
        name="route",
    )(h, wpad, bpad)


def _rank_kernel(sel_ref, i1_ref, i2_ref, tril_ref, rank_ref, cnt_ref):
    @pl.when(pl.program_id(0) == 0)
    def _():
        cnt_ref[...] = jnp.zeros(cnt_ref.shape, F32)

    sel = sel_ref[...]
    rank = _dot(tril_ref[...], sel) + cnt_ref[...]
    lane = lax.broadcasted_iota(jnp.int32, rank.shape, 1)
    r1 = jnp.sum(jnp.where(lane == i1_ref[...], rank, 0.0), axis=1, keepdims=True)
    r2 = jnp.sum(jnp.where(lane == i2_ref[...], rank, 0.0), axis=1, keepdims=True)
    rank_ref[...] = jnp.where(lane == 0, r1, jnp.where(lane == 1, r2, 0.0))
    cnt_ref[...] += jnp.sum(sel.astype(F32), axis=0, keepdims=True)


def _rank_call(sel, i1b, i2b, tb):
    S = sel.shape[0]
    tril = jnp.asarray(np.tril(np.ones((tb, tb), np.float32), -1), BF16)
    row = pl.BlockSpec((tb, LANES), lambda i: (i, 0))
    return pl.pallas_call(
        _rank_kernel,
        grid=(S // tb,),
        in_specs=[row, row, row, pl.BlockSpec((tb, tb), lambda i: (0, 0))],
        out_specs=[row, pl.BlockSpec((1, LANES), lambda i: (0, 0))],
        out_shape=[jax.ShapeDtypeStruct((S, LANES), F32), jax.ShapeDtypeStruct((1, LANES), F32)],
        compiler_params=_cparams(("arbitrary",)),
        name="rank",
    )(sel, i1b, i2b, tril)


def _row_copy(src_ref, s, dst_ref, d, sem):
    return pltpu.make_async_copy(src_ref.at[pl.ds(s, 1)], dst_ref.at[pl.ds(d, 1)], sem)


def _dispatch_kernel(tb, tmx, n_tok, dest_ref, ends_ref, h_ref, xs_hbm, zero_ref, sem, zsem):
    base = pl.program_id(0) * tb

    @pl.when(pl.program_id(0) == 0)
    def _():
        zero_ref[...] = jnp.zeros(zero_ref.shape, F32)

        def fill(e, wait):
            start = 0 if e == 0 else ends_ref[e - 1]
            end = ends_ref[e]

            @pl.when(end > start)
            def _():
                cp = pltpu.make_async_copy(
                    zero_ref, xs_hbm.at[pl.ds(pl.multiple_of(end - tmx, tmx), tmx)], zsem)
                if wait:
                    cp.wait()
                else:
                    cp.start()

        def fill_tail(k, wait):
            start = ends_ref[N_EXPERTS - 1] + k * tmx

            @pl.when(start < xs_hbm.shape[0])
            def _():
                cp = pltpu.make_async_copy(
                    zero_ref, xs_hbm.at[pl.ds(pl.multiple_of(start, tmx), tmx)], zsem)
                if wait:
                    cp.wait()
                else:
                    cp.start()

        for wait in (False, True):
            for e in range(N_EXPERTS):
                fill(e, wait)
                fill_tail(e, wait)

    def issue(t, c):
        _row_copy(h_ref, t, xs_hbm, dest_ref[base + t], sem).start(priority=0)
        _row_copy(h_ref, t, xs_hbm, dest_ref[n_tok + base + t], sem).start(priority=1)
        return c

    lax.fori_loop(0, tb, issue, 0, unroll=DMA_UNROLL)

    def drain(t, c):
        _row_copy(h_ref, 0, xs_hbm, 0, sem).wait()
        _row_copy(h_ref, 0, xs_hbm, 0, sem).wait()
        return c

    lax.fori_loop(0, tb, drain, 0, unroll=DMA_UNROLL)


def _dispatch_call(dest, ends, h, n_rows, tb, tmx):
    S, D = h.shape
    return pl.pallas_call(
        functools.partial(_dispatch_kernel, tb, tmx, S),
        grid_spec=pltpu.PrefetchScalarGridSpec(
            num_scalar_prefetch=2,
            grid=(S // tb,),
            in_specs=[pl.BlockSpec((tb, D), lambda i, d, e: (i, 0))],
            out_specs=pl.BlockSpec(memory_space=pl.ANY),
            scratch_shapes=[pltpu.VMEM((tmx, D), F32), pltpu.SemaphoreType.DMA,
                            pltpu.SemaphoreType.DMA],
        ),
        out_shape=jax.ShapeDtypeStruct((n_rows, D), F32),
        compiler_params=_cparams(("arbitrary",)),
        name="dispatch",
    )(dest, ends, h)


def _experts_kernel(te_ref, nu_ref, xs_ref, wg_ref, wu_ref, wd_ref, ys_ref, xb_ref, acc_ref):
    i = pl.program_id(0)
    f = pl.program_id(1)
    last_f = pl.num_programs(1) - 1
    used = i < nu_ref[0]

    @pl.when(used & (f == 0))
    def _():
        xb_ref[...] = xs_ref[...].astype(BF16)

    @pl.when(used)
    def _():
        hid = _swiglu_hidden(xb_ref[...], wg_ref[0], wu_ref[0])
        part = _dot(hid.astype(BF16), wd_ref[0])

        @pl.when(f == 0)
        def _():
            acc_ref[...] = part

        @pl.when(f > 0)
        def _():
            acc_ref[...] += part

    @pl.when(used & (f == last_f))
    def _():
        ys_ref[...] = acc_ref[...]

    @pl.when(jnp.logical_not(used) & (f == last_f))
    def _():
        ys_ref[...] = jnp.zeros(ys_ref.shape, F32)


def _experts_call(tile_expert, n_used, xs, wg, wu, wd, tmx, tf):
    P, D = xs.shape
    F = wg.shape[2]
    nf = F // tf

    def fsel(i, f, nu):
        return jnp.where(i < nu[0], f, nf - 1)

    return pl.pallas_call(
        _experts_kernel,
        grid_spec=pltpu.PrefetchScalarGridSpec(
            num_scalar_prefetch=2,
            grid=(P // tmx, nf),
            in_specs=[pl.BlockSpec((tmx, D), lambda i, f, te, nu: (jnp.where(i < nu[0], i, 0), 0)),
                      pl.BlockSpec((1, D, tf), lambda i, f, te, nu: (te[i], 0, fsel(i, f, nu))),
                      pl.BlockSpec((1, D, tf), lambda i, f, te, nu: (te[i], 0, fsel(i, f, nu))),
                      pl.BlockSpec((1, tf, D), lambda i, f, te, nu: (te[i], fsel(i, f, nu), 0))],
            out_specs=pl.BlockSpec((tmx, D), lambda i, f, te, nu: (i, 0)),
            scratch_shapes=[pltpu.VMEM((tmx, D), BF16), pltpu.VMEM((tmx, D), F32)],
        ),
        out_shape=jax.ShapeDtypeStruct((P, D), F32),
        compiler_params=_cparams(("arbitrary", "arbitrary")),
        name="experts",
    )(tile_expert, n_used, xs, wg, wu, wd)


def _combine_kernel(tm, n_tok, dest_ref, ys_hbm, h_ref, w1_ref, w2_ref, g_ref, b_ref,
                    out_ref, outb_ref, buf_ref, sem):
    i = pl.program_id(0)
    slot = i % 2

    def gather(tile, slot_, start):
        base = tile * tm

        def one(t, c):
            for k in range(2):
                src_row = dest_ref[k * n_tok + base + t] if start else 0
                cp = pltpu.make_async_copy(ys_hbm.at[pl.ds(src_row, 1)],
                                           buf_ref.at[slot_, k, pl.ds(t, 1)], sem.at[slot_])
                if start:
                    cp.start(priority=k)
                else:
                    cp.wait()
            return c

        lax.fori_loop(0, tm, one, 0, unroll=DMA_UNROLL)

    @pl.when(i == 0)
    def _():
        gather(0, 0, True)

    @pl.when(i + 1 < pl.num_programs(0))
    def _():
        gather(i + 1, 1 - slot, True)

    gather(i, slot, False)

    w1 = w1_ref[...]
    w2 = w2_ref[...]
    y = jnp.concatenate(
        [w1 * buf_ref[slot, 0, :, c * LANES:(c + 1) * LANES]
         + w2 * buf_ref[slot, 1, :, c * LANES:(c + 1) * LANES]
         for c in range(h_ref.shape[1] // LANES)], axis=1)
    hn = _layer_norm(ALPHA * h_ref[...] + y, g_ref[...], b_ref[...])
    out_ref[...] = hn
    outb_ref[...] = hn.astype(BF16)


def _combine_call(dest, ys, h, w1b, w2b, g, b, tm):
    S, D = h.shape
    row = pl.BlockSpec((tm, D), lambda i, d: (i, 0))
    lrow = pl.BlockSpec((tm, LANES), lambda i, d: (i, 0))
    vec = pl.BlockSpec((1, D), lambda i, d: (0, 0))
    return pl.pallas_call(
        functools.partial(_combine_kernel, tm, S),
        grid_spec=pltpu.PrefetchScalarGridSpec(
            num_scalar_prefetch=1,
            grid=(S // tm,),
            in_specs=[pl.BlockSpec(memory_space=pl.ANY), row, lrow, lrow, vec, vec],
            out_specs=[row, row],
            scratch_shapes=[pltpu.VMEM((2, 2, tm, D), F32), pltpu.SemaphoreType.DMA((2,))],
        ),
        out_shape=[jax.ShapeDtypeStruct((S, D), F32), jax.ShapeDtypeStruct((S, D), BF16)],
        compiler_params=_cparams(("arbitrary",)),
        name="combine",
    )(dest, ys, h, w1b, w2b, g, b)


def _moe_call(h, w_router, b_router, wg, wu, wd, g, b, tm, tf):
    S, D = h.shape
    tmx = MOE_TILE
    sel, i1b, i2b, w1b, w2b = _route_call(h, w_router, b_router, tm)
    ranks, counts = _rank_call(sel, i1b, i2b, min(256, S))
    cnt = counts[0, :N_EXPERTS].astype(jnp.int32)
    padded = ((cnt + tmx - 1) // tmx) * tmx
    ends = jnp.cumsum(padded)
    off = ends - padded
    dest = jnp.concatenate([off[i1b[:, 0]] + ranks[:, 0].astype(jnp.int32),
                            off[i2b[:, 0]] + ranks[:, 1].astype(jnp.int32)])
    n_tiles = (2 * S) // tmx + N_EXPERTS
    tile_start = jnp.arange(n_tiles, dtype=jnp.int32) * tmx
    tile_expert = jnp.minimum(jnp.sum(tile_start[:, None] >= ends[None, :], axis=1),
                              N_EXPERTS - 1).astype(jnp.int32)
    n_used = (ends[-1] // tmx).astype(jnp.int32).reshape(1)
    xs = _dispatch_call(dest, ends.astype(jnp.int32), h, n_tiles * tmx, tm, tmx)
    ys = _experts_call(tile_expert, n_used, xs, wg, wu, wd, tmx, tf)
    return _combine_call(dest, ys, h, w1b, w2b, g, b, min(256, S))


def _rope_tables(S):
    half = HEAD_DIM // 2
    pos = jnp.arange(S, dtype=F32)
    inv = ROPE_THETA ** (-jnp.arange(half, dtype=F32) / half)
    ang = pos[:, None] * inv[None, :]
    cos, sin = jnp.cos(ang), jnp.sin(ang)
    reps = LANES // HEAD_DIM
    cos_t = jnp.tile(jnp.concatenate([cos, cos], axis=1), (1, reps))
    sin_t = jnp.tile(jnp.concatenate([-sin, sin], axis=1), (1, reps))
    return cos_t, sin_t


def _pick(S, pref):
    t = min(pref, S)
    assert S % t == 0
    return t


def _forward(x, ln_in_g, ln_in_b, w_in, b_forget, lam_q1, lam_k1, lam_q2, lam_k2,
             diff_norm_g, w_up_diff, w_up_sb, w_up_fox, w_out, ln_mix_g, ln_mix_b,
             ln_ffn_g, ln_ffn_b, w_gate_dense, w_up_dense, w_down_dense, w_router,
             b_router, w_gate_moe, w_up_moe, w_down_moe):
    B, S, D = x.shape
    assert B == 1 and D == D_MODEL and S % LANES == 0
    depth = w_in.shape[0]
    tm = _pick(S, 512)
    tq = _pick(S, 512)
    tm_ffn = _pick(S, 1024)
    tf = D_FF // 7
    tf_moe = D_FF // 2
    vec = lambda a: a.reshape(1, -1).astype(F32)

    cos_t, sin_t = _rope_tables(S)
    h, hb = _ln_call(x.reshape(S, D), ln_in_g, ln_in_b, tm)
    w_atts, w_forgets, w_gate_logits = _wsplit_call(w_in, 128)
    for l in range(depth):
        w_att, w_forget, w_gate_logit = w_atts[l], w_forgets[l], w_gate_logits[l]
        fbias = jnp.pad(b_forget[l].astype(F32), (0, LANES - N_HEADS_FOX)).reshape(1, LANES)
        dq, dk, dv, sq, sk, sv, fq, fk, fv, logf = _proj_call(hb, w_att, w_forget, cos_t, sin_t,
                                                              fbias, tm)

        lf = logf[:, :N_HEADS_FOX].T.reshape(N_HEADS_FOX, S // LANES, LANES)
        fcum = _fcum_call(lf)

        lam_init = 0.8 - 0.6 * math.exp(-0.3 * l)
        lamv = jnp.stack([lam_q1[l], lam_k1[l], lam_q2[l], lam_k2[l]]).astype(F32)
        o_diff = _diff_call(dq, dk, dv, lamv, vec(diff_norm_g[l]), lam_init, tq)
        o_sb = _sb_call(sq, sk, sv, tq, tq)
        o_fox = _fox_call(fq, fk, fv, fcum, tq)

        h, hb = _merge_call(hb, h, o_diff, o_sb, o_fox, w_gate_logit,
                            w_up_diff[l].astype(BF16), w_up_sb[l].astype(BF16),
                            w_up_fox[l].astype(BF16), w_out[l].astype(BF16),
                            vec(ln_mix_g[l]), vec(ln_mix_b[l]), tm)
        j = l // 2
        if l % 2 == 0:
            h, hb = _ffn_call(hb, h, w_gate_dense[j].astype(BF16), w_up_dense[j].astype(BF16),
                              w_down_dense[j].astype(BF16), vec(ln_ffn_g[l]), vec(ln_ffn_b[l]),
                              tm_ffn, tf)
        else:
            h, hb = _moe_call(h, w_router[j], b_router[j], w_gate_moe[j].astype(BF16),
                              w_up_moe[j].astype(BF16), w_down_moe[j].astype(BF16),
                              vec(ln_ffn_g[l]), vec(ln_ffn_b[l]), tm, tf_moe)
    return h.reshape(B, S, D)


def kernel(x, ln_in_g, ln_in_b, w_in, b_forget, lam_q1, lam_k1, lam_q2, lam_k2, diff_norm_g,
           w_up_diff, w_up_sb, w_up_fox, w_out, ln_mix_g, ln_mix_b, ln_ffn_g, ln_ffn_b,
           w_gate_dense, w_up_dense, w_down_dense, w_router, b_router, w_gate_moe, w_up_moe,
           w_down_moe):
    return _forward(x, ln_in_g, ln_in_b, w_in, b_forget, lam_q1, lam_k1, lam_q2, lam_k2,
                    diff_norm_g, w_up_diff, w_up_sb, w_up_fox, w_out, ln_mix_g, ln_mix_b,
                    ln_ffn_g, ln_ffn_b, w_gate_dense, w_up_dense, w_down_dense, w_router,
                    b_router, w_gate_moe, w_up_moe, w_down_moe)
```
